```python
import math
import jax, jax.numpy as jnp
from jax import lax
import numpy as np

D_MODEL = 1024
BATCH = 32
SEQ = 2048
DEPTH = 1

HEAD_DIM = 128
ATTN_SLOTS = 4
DIL_PATTERNS = ((128, 1), (512, 4), (2048, 16))
N_DIL = len(DIL_PATTERNS)
ATTN_QK = N_DIL * ATTN_SLOTS * HEAD_DIM
ATTN_OUT = ATTN_SLOTS * HEAD_DIM
BAND_BLOCK = 128
ROPE_THETA = 10000.0
CONV_CH = 512
CONV_WIDTH = 31
MEM_LEN = 256
MEM_HEADS = 4
MEM_HEAD_DIM = 128
MEM_W = MEM_HEADS * MEM_HEAD_DIM
N_BRANCH = 3
IN_COLS = 3 * ATTN_QK + 2 * CONV_CH + MEM_W + N_BRANCH * D_MODEL
N_GROUPS = 4
EXPERTS_PER_GROUP = 4
N_EXPERTS = N_GROUPS * EXPERTS_PER_GROUP
TOP_K = 2
EXPERT_FF = 512
LN_EPS = 1e-5
DN_ALPHA = (2.0 * DEPTH) ** 0.25
DN_BETA = (8.0 * DEPTH) ** -0.25

kernel_name = "hybrid_dilated_conformer_memory_hmoe_deepnorm"


def layer_norm(x, g, b):
    xf = x.astype(jnp.float32)
    mu = jnp.mean(xf, axis=-1, keepdims=True)
    var = jnp.mean(jnp.square(xf - mu), axis=-1, keepdims=True)
    y = (xf - mu) * lax.rsqrt(var + LN_EPS)
    return (y * g.astype(jnp.float32) + b.astype(jnp.float32)).astype(x.dtype)


def rope_tables(positions):
    half = HEAD_DIM // 2
    inv = ROPE_THETA ** (-jnp.arange(half, dtype=jnp.float32) / half)
    ang = positions.astype(jnp.float32)[..., None] * inv
    return jnp.cos(ang)[:, :, None, :], jnp.sin(ang)[:, :, None, :]


def apply_rope(t, cos, sin):
    half = HEAD_DIM // 2
    tf = t.astype(jnp.float32)
    t1, t2 = tf[..., :half], tf[..., half:]
    return jnp.concatenate([t1 * cos - t2 * sin, t2 * cos + t1 * sin], axis=-1).astype(t.dtype)


def dilated_group_attention(q, k, v, window, dilation):
    B, S, H, E = q.shape
    L = S // dilation
    nb = -(-L // BAND_BLOCK)
    Lp = nb * BAND_BLOCK
    reach = window // dilation

    def split(t):
        t = t.reshape(B, L, dilation, H, E).transpose(0, 2, 3, 1, 4)
        t = jnp.pad(t, ((0, 0), (0, 0), (0, 0), (0, Lp - L), (0, 0)))
        return t.reshape(B, dilation, H, nb, BAND_BLOCK, E)

    def with_prev(t):
        prev = jnp.pad(t, ((0, 0), (0, 0), (0, 0), (1, 0), (0, 0), (0, 0)))[:, :, :, :-1]
        return jnp.concatenate([prev, t], axis=4)

    qb = split(q)
    kk = with_prev(split(k))
    vv = with_prev(split(v))
    s = jnp.einsum('brhnqe,brhnke->brhnqk', qb, kk).astype(jnp.float32) * (E ** -0.5)
    qi = jnp.arange(BAND_BLOCK)[:, None]
    kj = jnp.arange(2 * BAND_BLOCK)[None, :]
    dist = BAND_BLOCK + qi - kj
    blk = jnp.arange(nb)[:, None, None]
    valid = (dist >= 0) & (dist <= reach) & (blk * BAND_BLOCK + kj - BAND_BLOCK >= 0)
    s = jnp.where(valid, s, -jnp.inf)
    m = jnp.max(s, axis=-1, keepdims=True)
    lse = m + jnp.log(jnp.sum(jnp.exp(s - m), axis=-1, keepdims=True))
    p = jnp.exp(s - lse)
    o = jnp.einsum('brhnqk,brhnke->brhnqe', p.astype(v.dtype), vv)
    o = o.reshape(B, dilation, H, Lp, E)[:, :, :, :L].transpose(0, 3, 1, 2, 4).reshape(B, S, H, E)
    lse = lse.reshape(B, dilation, H, Lp)[:, :, :, :L].transpose(0, 3, 1, 2).reshape(B, S, H)
    return o, lse


def hybrid_mixer(h, mem, cos, sin, w_in, b_in, conv_w, conv_b, conv_ln_g, conv_ln_b,
                 w_mem_kv, w_attn_o, w_conv_o, w_mem_o, w_out):
    B, S, D = h.shape
    u = h @ w_in + b_in
    cuts = [ATTN_QK, 2 * ATTN_QK, 3 * ATTN_QK, 3 * ATTN_QK + 2 * CONV_CH,
            3 * ATTN_QK + 2 * CONV_CH + MEM_W]
    q, k, v, glu_in, qm, gate_pre = jnp.split(u, cuts, axis=-1)

    nh = N_DIL * ATTN_SLOTS
    q = apply_rope(q.reshape(B, S, nh, HEAD_DIM), cos, sin).reshape(B, S, N_DIL, ATTN_SLOTS, HEAD_DIM)
    k = apply_rope(k.reshape(B, S, nh, HEAD_DIM), cos, sin).reshape(B, S, N_DIL, ATTN_SLOTS, HEAD_DIM)
    v = v.reshape(B, S, N_DIL, ATTN_SLOTS, HEAD_DIM)
    outs, lses = [], []
    for g, (window, dilation) in enumerate(DIL_PATTERNS):
        o_g, lse_g = dilated_group_attention(q[:, :, g], k[:, :, g], v[:, :, g], window, dilation)
        outs.append(o_g)
        lses.append(lse_g)
    wts = jax.nn.softmax(jnp.stack(lses, axis=0), axis=0)
    o_attn = jnp.sum(wts[..., None].astype(v.dtype) * jnp.stack(outs, axis=0), axis=0)
    y_attn = o_attn.reshape(B, S, ATTN_OUT) @ w_attn_o

    a, b = jnp.split(glu_in, 2, axis=-1)
    c = a * jax.nn.sigmoid(b)
    c = lax.conv_general_dilated(c, conv_w[:, None, :], window_strides=(1,),
                                 padding=[(CONV_WIDTH - 1, 0)],
                                 dimension_numbers=('NWC', 'WIO', 'NWC'),
                                 feature_group_count=CONV_CH) + conv_b
    c = jax.nn.silu(layer_norm(c, conv_ln_g, conv_ln_b))
    y_conv = c @ w_conv_o

    kv = mem @ w_mem_kv
    km, vm = jnp.split(kv, 2, axis=-1)
    Mlen = mem.shape[1]
    qm = qm.reshape(B, S, MEM_HEADS, MEM_HEAD_DIM)
    km = km.reshape(B, Mlen, MEM_HEADS, MEM_HEAD_DIM)
    vm = vm.reshape(B, Mlen, MEM_HEADS, MEM_HEAD_DIM)
    sm = jnp.einsum('bshe,bmhe->bhsm', qm, km).astype(jnp.float32) * (MEM_HEAD_DIM ** -0.5)
    pm = jax.nn.softmax(sm, axis=-1)
    om = jnp.einsum('bhsm,bmhe->bshe', pm.astype(vm.dtype), vm).reshape(B, S, MEM_W)
    y_mem = om @ w_mem_o

    gates = jax.nn.sigmoid(gate_pre).reshape(B, S, N_BRANCH, D)
    merged = gates[:, :, 0] * y_attn + gates[:, :, 1] * y_conv + gates[:, :, 2] * y_mem
    return merged @ w_out


def hierarchical_moe(h, w_gr, b_gr, w_er, b_er, w_g, w_u, w_d):
    B, S, D = h.shape
    t = h.reshape(B * S, D)
    gl = (t @ w_gr).astype(jnp.float32) + b_gr.astype(jnp.float32)
    gp = jax.nn.softmax(gl, axis=-1)
    g_sel = jnp.argmax(gl, axis=-1)
    g_prob = jnp.take_along_axis(gp, g_sel[:, None], axis=-1)[:, 0]
    el = jnp.einsum('td,gde->tge', t, w_er).astype(jnp.float32) + b_er.astype(jnp.float32)
    el_sel = jnp.take_along_axis(el, g_sel[:, None, None], axis=1)[:, 0]
    top_v, top_i = lax.top_k(el_sel, TOP_K)
    top_w = jax.nn.softmax(top_v, axis=-1) * g_prob[:, None]
    ew = jnp.sum(jax.nn.one_hot(top_i, EXPERTS_PER_GROUP, dtype=jnp.float32) * top_w[..., None], axis=1)
    full = (jax.nn.one_hot(g_sel, N_GROUPS, dtype=jnp.float32)[:, :, None] * ew[:, None, :])
    full = full.reshape(B * S, N_EXPERTS).astype(t.dtype)
    y = jnp.zeros_like(t)
    for e in range(N_EXPERTS):
        hid = jax.nn.silu(t @ w_g[e]) * (t @ w_u[e])
        y = y + full[:, e:e + 1] * (hid @ w_d[e])
    return y.reshape(B, S, D)


def setup_inputs(seed: int = 0) -> dict:
    key = jax.random.key(seed)
    ks = jax.random.split(key, 26)
    f32 = jnp.float32
    nrm = lambda k, shape, scale: jax.random.normal(k, shape, f32) * scale
    x = jax.random.normal(ks[0], (BATCH, SEQ, D_MODEL), f32)
    mem = jax.random.normal(ks[1], (BATCH, MEM_LEN, D_MODEL), f32)
    offset = jax.random.randint(ks[2], (BATCH, 1), 0, 1024, dtype=jnp.int32)
    positions = offset + jnp.arange(SEQ, dtype=jnp.int32)[None, :]
    w_in = nrm(ks[3], (DEPTH, D_MODEL, IN_COLS), D_MODEL ** -0.5)
    w_in = w_in.at[:, :, 2 * ATTN_QK:3 * ATTN_QK].multiply(DN_BETA)
    b_in = nrm(ks[4], (DEPTH, IN_COLS), 0.02)
    conv_w = nrm(ks[5], (DEPTH, CONV_WIDTH, CONV_CH), CONV_WIDTH ** -0.5)
    conv_b = nrm(ks[6], (DEPTH, CONV_CH), 0.02)
    conv_ln_g = 1.0 + nrm(ks[7], (DEPTH, CONV_CH), 0.02)
    conv_ln_b = nrm(ks[8], (DEPTH, CONV_CH), 0.02)
    w_mem_kv = nrm(ks[9], (DEPTH, D_MODEL, 2 * MEM_W), D_MODEL ** -0.5)
    w_attn_o = nrm(ks[10], (DEPTH, ATTN_OUT, D_MODEL), ATTN_OUT ** -0.5 * DN_BETA)
    w_conv_o = nrm(ks[11], (DEPTH, CONV_CH, D_MODEL), CONV_CH ** -0.5 * DN_BETA)
    w_mem_o = nrm(ks[12], (DEPTH, MEM_W, D_MODEL), MEM_W ** -0.5 * DN_BETA)
    w_out = nrm(ks[13], (DEPTH, D_MODEL, D_MODEL), D_MODEL ** -0.5 * DN_BETA)
    ln1_g = 1.0 + nrm(ks[14], (DEPTH, D_MODEL), 0.02)
    ln1_b = nrm(ks[15], (DEPTH, D_MODEL), 0.02)
    w_group_router = nrm(ks[16], (DEPTH, D_MODEL, N_GROUPS), D_MODEL ** -0.5)
    b_group_router = nrm(ks[17], (DEPTH, N_GROUPS), 0.01)
    w_expert_router = nrm(ks[18], (DEPTH, N_GROUPS, D_MODEL, EXPERTS_PER_GROUP), D_MODEL ** -0.5)
    b_expert_router = nrm(ks[19], (DEPTH, N_GROUPS, EXPERTS_PER_GROUP), 0.01)
    w_exp_gate = nrm(ks[20], (DEPTH, N_EXPERTS, D_MODEL, EXPERT_FF), D_MODEL ** -0.5)
    w_exp_up = nrm(ks[21], (DEPTH, N_EXPERTS, D_MODEL, EXPERT_FF), D_MODEL ** -0.5)
    w_exp_down = nrm(ks[22], (DEPTH, N_EXPERTS, EXPERT_FF, D_MODEL), EXPERT_FF ** -0.5 * DN_BETA)
    ln2_g = 1.0 + nrm(ks[23], (DEPTH, D_MODEL), 0.02)
    ln2_b = nrm(ks[24], (DEPTH, D_MODEL), 0.02)
    return {"x": x, "mem": mem, "positions": positions, "w_in": w_in, "b_in": b_in,
            "conv_w": conv_w, "conv_b": conv_b, "conv_ln_g": conv_ln_g, "conv_ln_b": conv_ln_b,
            "w_mem_kv": w_mem_kv, "w_attn_o": w_attn_o, "w_conv_o": w_conv_o, "w_mem_o": w_mem_o,
            "w_out": w_out, "ln1_g": ln1_g, "ln1_b": ln1_b,
            "w_group_router": w_group_router, "b_group_router": b_group_router,
            "w_expert_router": w_expert_router, "b_expert_router": b_expert_router,
            "w_exp_gate": w_exp_gate, "w_exp_up": w_exp_up, "w_exp_down": w_exp_down,
            "ln2_g": ln2_g, "ln2_b": ln2_b}


def reference(x, mem, positions, w_in, b_in, conv_w, conv_b, conv_ln_g, conv_ln_b,
              w_mem_kv, w_attn_o, w_conv_o, w_mem_o, w_out, ln1_g, ln1_b,
              w_group_router, b_group_router, w_expert_router, b_expert_router,
              w_exp_gate, w_exp_up, w_exp_down, ln2_g, ln2_b):
    cos, sin = rope_tables(positions)
    for l in range(DEPTH):
        mix = hybrid_mixer(x, mem, cos, sin, w_in[l], b_in[l], conv_w[l], conv_b[l],
                           conv_ln_g[l], conv_ln_b[l], w_mem_kv[l], w_attn_o[l], w_conv_o[l],
                           w_mem_o[l], w_out[l])
        x = layer_norm(DN_ALPHA * x + mix, ln1_g[l], ln1_b[l])
        ffn = hierarchical_moe(x, w_group_router[l], b_group_router[l], w_expert_router[l],
                               b_expert_router[l], w_exp_gate[l], w_exp_up[l], w_exp_down[l])
        x = layer_norm(DN_ALPHA * x + ffn, ln2_g[l], ln2_b[l])
    return x
```

```python
import functools
import math

import jax
import jax.numpy as jnp
from jax import lax
from jax.experimental import pallas as pl
from jax.experimental.pallas import tpu as pltpu

D_MODEL = 1024
HEAD_DIM = 128
ATTN_SLOTS = 4
DIL_PATTERNS = ((128, 1), (512, 4), (2048, 16))
N_DIL = len(DIL_PATTERNS)
ATTN_QK = N_DIL * ATTN_SLOTS * HEAD_DIM
ATTN_OUT = ATTN_SLOTS * HEAD_DIM
BAND_BLOCK = 128
ROPE_THETA = 10000.0
CONV_CH = 512
CONV_WIDTH = 31
MEM_HEADS = 4
MEM_HEAD_DIM = 128
MEM_W = MEM_HEADS * MEM_HEAD_DIM
N_BRANCH = 3
N_GROUPS = 4
EXPERTS_PER_GROUP = 4
N_EXPERTS = N_GROUPS * EXPERTS_PER_GROUP
EXPERT_FF = 512
LN_EPS = 1e-5

LANES = 128
CONV_PAD = 32
VMEM_LIMIT = 56 * 1024 * 1024

F32 = jnp.float32
BF16 = jnp.bfloat16
NEG = -1e30


def _cparams(sem):
    return pltpu.CompilerParams(dimension_semantics=sem, vmem_limit_bytes=VMEM_LIMIT)


def _full(shape):
    n = len(shape)
    return pl.BlockSpec(shape, lambda *_: (0,) * n)


def _rope_kernel(pos_ref, inv_ref, cos_ref, sin_ref):
    ang = pos_ref[...].astype(F32) * inv_ref[...]
    lane = lax.broadcasted_iota(jnp.int32, ang.shape, 1)
    s = jnp.sin(ang)
    cos_ref[...] = jnp.cos(ang)
    sin_ref[...] = jnp.where(lane < HEAD_DIM // 2, -s, s)


def _rope_tables(positions):
    B, S = positions.shape
    half = HEAD_DIM // 2
    inv = ROPE_THETA ** (-jnp.arange(half, dtype=F32) / half)
    inv = jnp.concatenate([inv, inv]).reshape(1, HEAD_DIM)
    pos3 = positions.reshape(B, S, 1)
    return pl.pallas_call(
        _rope_kernel,
        grid=(B,),
        in_specs=[pl.BlockSpec((None, S, 1), lambda b: (b, 0, 0)), _full((1, HEAD_DIM))],
        out_specs=[pl.BlockSpec((None, S, HEAD_DIM), lambda b: (b, 0, 0))] * 2,
        out_shape=[jax.ShapeDtypeStruct((B, S, HEAD_DIM), F32)] * 2,
        compiler_params=_cparams(("parallel",)),
        name="rope_tables",
    )(pos3, inv)


def _attn_kernel(x_ref, cos_ref, sin_ref, w_ref, b_ref, o_ref, lse_ref, q_s, k_s, v_s, *, L, rc):
    k_s[0:BAND_BLOCK, :] = jnp.zeros((BAND_BLOCK, ATTN_OUT), BF16)
    v_s[0:BAND_BLOCK, :] = jnp.zeros((BAND_BLOCK, ATTN_OUT), BF16)
    scale = HEAD_DIM ** -0.5

    def proj(i, carry):
        r0 = pl.multiple_of(i * rc, rc)
        xs = x_ref[pl.ds(r0, rc), :].astype(BF16)
        qkv = jnp.dot(xs, w_ref[...], preferred_element_type=F32) + b_ref[...]
        c = cos_ref[pl.ds(r0, rc), :]
        s = sin_ref[pl.ds(r0, rc), :]
        for h in range(ATTN_SLOTS):
            lo = h * HEAD_DIM
            qh = qkv[:, lo:lo + HEAD_DIM]
            qh = (qh * c + pltpu.roll(qh, HEAD_DIM // 2, 1) * s) * scale
            q_s[pl.ds(r0, rc), lo:lo + HEAD_DIM] = qh.astype(BF16)
            kh = qkv[:, ATTN_OUT + lo:ATTN_OUT + lo + HEAD_DIM]
            kh = kh * c + pltpu.roll(kh, HEAD_DIM // 2, 1) * s
            k_s[pl.ds(BAND_BLOCK + r0, rc), lo:lo + HEAD_DIM] = kh.astype(BF16)
        v_s[pl.ds(BAND_BLOCK + r0, rc), :] = qkv[:, 2 * ATTN_OUT:3 * ATTN_OUT].astype(BF16)
        return carry

    lax.fori_loop(0, L // rc, proj, 0)

    qi = lax.broadcasted_iota(jnp.int32, (BAND_BLOCK, 2 * BAND_BLOCK), 0)
    kj = lax.broadcasted_iota(jnp.int32, (BAND_BLOCK, 2 * BAND_BLOCK), 1)
    dist = BAND_BLOCK + qi - kj
    band = (dist >= 0) & (dist <= BAND_BLOCK)
    lane = lax.broadcasted_iota(jnp.int32, (BAND_BLOCK, LANES), 1)

    def blk(n, carry):
        r0 = pl.multiple_of(n * BAND_BLOCK, BAND_BLOCK)
        valid = band & ((kj >= BAND_BLOCK) | (n > 0))
        lse_tile = jnp.zeros((BAND_BLOCK, LANES), F32)
        for h in range(ATTN_SLOTS):
            lo = h * HEAD_DIM
            q = q_s[pl.ds(r0, BAND_BLOCK), lo:lo + HEAD_DIM]
            kk = k_s[pl.ds(r0, 2 * BAND_BLOCK), lo:lo + HEAD_DIM]
            vv = v_s[pl.ds(r0, 2 * BAND_BLOCK), lo:lo + HEAD_DIM]
            s = lax.dot_general(q, kk, (((1,), (1,)), ((), ())), preferred_element_type=F32)
            s = jnp.where(valid, s, NEG)
            m = jnp.max(s, axis=-1, keepdims=True)
            p = jnp.exp(s - m)
            l = jnp.sum(p, axis=-1, keepdims=True)
            o = jnp.dot(p.astype(BF16), vv, preferred_element_type=F32) / l
            o_ref[pl.ds(r0, BAND_BLOCK), lo:lo + HEAD_DIM] = o.astype(o_ref.dtype)
            lse_h = m + jnp.log(l)
            seg = LANES // ATTN_SLOTS
            lse_tile = jnp.where((lane >= h * seg) & (lane < (h + 1) * seg), lse_h, lse_tile)
        lse_ref[pl.ds(r0, BAND_BLOCK), :] = lse_tile
        return carry

    lax.fori_loop(0, L // BAND_BLOCK, blk, 0)


def _attn_group(x, cos2, sin2, w, b, dilation):
    B, S, D = x.shape
    r = dilation
    L = S // r
    rc = min(L, 256)
    xv = x.reshape(B, L, r * D)
    cv = cos2.reshape(B, L, r * HEAD_DIM)
    sv = sin2.reshape(B, L, r * HEAD_DIM)
    kern = functools.partial(_attn_kernel, L=L, rc=rc)
    o, lse = pl.pallas_call(
        kern,
        grid=(B, r),
        in_specs=[
            pl.BlockSpec((None, L, D), lambda bb, j: (bb, 0, j)),
            pl.BlockSpec((None, L, HEAD_DIM), lambda bb, j: (bb, 0, j)),
            pl.BlockSpec((None, L, HEAD_DIM), lambda bb, j: (bb, 0, j)),
            _full((D, 3 * ATTN_OUT)),
            _full((1, 3 * ATTN_OUT)),
        ],
        out_specs=[
            pl.BlockSpec((None, L, ATTN_OUT), lambda bb, j: (bb, 0, j)),
            pl.BlockSpec((None, L, LANES), lambda bb, j: (bb, 0, j)),
        ],
        out_shape=[
            jax.ShapeDtypeStruct((B, L, r * ATTN_OUT), BF16),
            jax.ShapeDtypeStruct((B, L, r * LANES), F32),
        ],
        scratch_shapes=[
            pltpu.VMEM((L, ATTN_OUT), BF16),
            pltpu.VMEM((BAND_BLOCK + L, ATTN_OUT), BF16),
            pltpu.VMEM((BAND_BLOCK + L, ATTN_OUT), BF16),
        ],
        compiler_params=_cparams(("parallel", "parallel")),
        name=f"dilated_attn_r{r}",
    )(xv, cv, sv, w, b)
    return o.reshape(B * S, ATTN_OUT), lse.reshape(B * S, LANES)


def _conv_kernel(x_ref, w_ref, b_ref, cw_ref, cb_ref, g_ref, beta_ref, out_ref, cpad, *, S, rc):
    cpad[0:CONV_PAD, :] = jnp.zeros((CONV_PAD, CONV_CH), F32)

    def glu(i, carry):
        r0 = pl.multiple_of(i * rc, rc)
        xs = x_ref[pl.ds(r0, rc), :].astype(BF16)
        u = jnp.dot(xs, w_ref[...], preferred_element_type=F32) + b_ref[...]
        cpad[pl.ds(CONV_PAD + r0, rc), :] = u[:, :CONV_CH] * jax.nn.sigmoid(u[:, CONV_CH:])
        return carry

    lax.fori_loop(0, S // rc, glu, 0)

    off = CONV_PAD - (CONV_WIDTH - 1)

    def conv(i, carry):
        r0 = pl.multiple_of(i * rc, rc)
        win = cpad[pl.ds(r0, rc + CONV_PAD), :]
        acc = jnp.zeros((rc, CONV_CH), F32) + cb_ref[...]
        for j in range(CONV_WIDTH):
            acc = acc + win[j + off:j + off + rc, :] * cw_ref[j:j + 1, :]
        mu = jnp.mean(acc, axis=-1, keepdims=True)
        d = acc - mu
        var = jnp.mean(d * d, axis=-1, keepdims=True)
        y = d * lax.rsqrt(var + LN_EPS) * g_ref[...] + beta_ref[...]
        out_ref[pl.ds(r0, rc), :] = (y * jax.nn.sigmoid(y)).astype(out_ref.dtype)
        return carry

    lax.fori_loop(0, S // rc, conv, 0)


def _conv_branch(x, w_glu, b_glu, conv_w, conv_b, ln_g, ln_b):
    B, S, D = x.shape
    rc = 256
    cw = jnp.pad(conv_w, ((0, CONV_PAD - CONV_WIDTH), (0, 0)))
    kern = functools.partial(_conv_kernel, S=S, rc=rc)
    out = pl.pallas_call(
        kern,
        grid=(B,),
        in_specs=[
            pl.BlockSpec((None, S, D), lambda b: (b, 0, 0)),
            _full((D, 2 * CONV_CH)), _full((1, 2 * CONV_CH)),
            _full((CONV_PAD, CONV_CH)), _full((1, CONV_CH)), _full((1, CONV_CH)), _full((1, CONV_CH)),
        ],
        out_specs=pl.BlockSpec((None, S, CONV_CH), lambda b: (b, 0, 0)),
        out_shape=jax.ShapeDtypeStruct((B, S, CONV_CH), BF16),
        scratch_shapes=[pltpu.VMEM((CONV_PAD + S, CONV_CH), F32)],
        compiler_params=_cparams(("parallel",)),
        name="conformer_conv",
    )(x, w_glu, b_glu, cw, conv_b, ln_g, ln_b)
    return out.reshape(B * S, CONV_CH)


def _mem_kernel(x_ref, mem_ref, wq_ref, bq_ref, wkv_ref, out_ref, *, S, rc):
    kv = jnp.dot(mem_ref[...].astype(BF16), wkv_ref[...], preferred_element_type=F32)
    km = kv[:, :MEM_W].astype(BF16)
    vm = kv[:, MEM_W:].astype(BF16)
    scale = MEM_HEAD_DIM ** -0.5

    def body(i, carry):
        r0 = pl.multiple_of(i * rc, rc)
        xs = x_ref[pl.ds(r0, rc), :].astype(BF16)
        q = (jnp.dot(xs, wq_ref[...], preferred_element_type=F32) + bq_ref[...]) * scale
        q = q.astype(BF16)
        for h in range(MEM_HEADS):
            lo = h * MEM_HEAD_DIM
            s = lax.dot_general(q[:, lo:lo + MEM_HEAD_DIM], km[:, lo:lo + MEM_HEAD_DIM],
                                (((1,), (1,)), ((), ())), preferred_element_type=F32)
            m = jnp.max(s, axis=-1, keepdims=True)
            p = jnp.exp(s - m)
            l = jnp.sum(p, axis=-1, keepdims=True)
            o = jnp.dot(p.astype(BF16), vm[:, lo:lo + MEM_HEAD_DIM], preferred_element_type=F32) / l
            out_ref[pl.ds(r0, rc), lo:lo + MEM_HEAD_DIM] = o.astype(out_ref.dtype)
        return carry

    lax.fori_loop(0, S // rc, body, 0)


def _mem_branch(x, mem, w_q, b_q, w_kv):
    B, S, D = x.shape
    M = mem.shape[1]
    rc = 256
    kern = functools.partial(_mem_kernel, S=S, rc=rc)
    out = pl.pallas_call(
        kern,
        grid=(B,),
        in_specs=[
            pl.BlockSpec((None, S, D), lambda b: (b, 0, 0)),
            pl.BlockSpec((None, M, D), lambda b: (b, 0, 0)),
            _full((D, MEM_W)), _full((1, MEM_W)), _full((D, 2 * MEM_W)),
        ],
        out_specs=pl.BlockSpec((None, S, MEM_W), lambda b: (b, 0, 0)),
        out_shape=jax.ShapeDtypeStruct((B, S, MEM_W), BF16),
        compiler_params=_cparams(("parallel",)),
        name="memory_attn",
    )(x, mem, w_q, b_q, w_kv)
    return out.reshape(B * S, MEM_W)


def _layer_norm(z, g, b):
    mu = jnp.mean(z, axis=-1, keepdims=True)
    d = z - mu
    var = jnp.mean(d * d, axis=-1, keepdims=True)
    return d * lax.rsqrt(var + LN_EPS) * g + b


def _merge_kernel(x_ref, o0_ref, o1_ref, o2_ref, l0_ref, l1_ref, l2_ref, c_ref, m_ref,
                  wg_ref, bg_ref, wa_ref, wc_ref, wm_ref, wo_ref, g1_ref, b1_ref, wr_ref, br_ref,
                  x1_ref, x1b_ref, gate_ref, *, alpha):
    x = x_ref[...]
    xb = x.astype(BF16)
    l0, l1, l2 = l0_ref[...], l1_ref[...], l2_ref[...]
    mx = jnp.maximum(jnp.maximum(l0, l1), l2)
    e0, e1, e2 = jnp.exp(l0 - mx), jnp.exp(l1 - mx), jnp.exp(l2 - mx)
    inv = 1.0 / (e0 + e1 + e2)
    seg = LANES // ATTN_SLOTS
    parts = []
    for h in range(ATTN_SLOTS):
        lo = h * HEAD_DIM
        w0 = (e0 * inv)[:, h * seg:h * seg + 1]
        w1 = (e1 * inv)[:, h * seg:h * seg + 1]
        w2 = (e2 * inv)[:, h * seg:h * seg + 1]
        parts.append(w0 * o0_ref[:, lo:lo + HEAD_DIM].astype(F32)
                     + w1 * o1_ref[:, lo:lo + HEAD_DIM].astype(F32)
                     + w2 * o2_ref[:, lo:lo + HEAD_DIM].astype(F32))
    o_attn = jnp.concatenate(parts, axis=-1).astype(BF16)

    def gate(k):
        lo = k * D_MODEL
        return jax.nn.sigmoid(jnp.dot(xb, wg_ref[:, lo:lo + D_MODEL], preferred_element_type=F32)
                              + bg_ref[:, lo:lo + D_MODEL])

    merged = gate(0) * jnp.dot(o_attn, wa_ref[...], preferred_element_type=F32)
    merged = merged + gate(1) * jnp.dot(c_ref[...], wc_ref[...], preferred_element_type=F32)
    merged = merged + gate(2) * jnp.dot(m_ref[...], wm_ref[...], preferred_element_type=F32)
    mix = jnp.dot(merged.astype(BF16), wo_ref[...], preferred_element_type=F32)
    x1 = _layer_norm(alpha * x + mix, g1_ref[...], b1_ref[...])
    x1_ref[...] = x1
    x1b_ref[...] = x1.astype(BF16)

    logits = jnp.dot(x1, wr_ref[...], preferred_element_type=F32,
                     precision=lax.Precision.HIGHEST) + br_ref[...]
    lane = lax.broadcasted_iota(jnp.int32, logits.shape, 1)
    is_g = lane < N_GROUPS
    gl = jnp.where(is_g, logits, NEG)
    gmax = jnp.max(gl, axis=-1, keepdims=True)
    g_sel = jnp.min(jnp.where(is_g & (gl == gmax), lane, LANES), axis=-1, keepdims=True)
    g_prob = 1.0 / jnp.sum(jnp.where(is_g, jnp.exp(gl - gmax), 0.0), axis=-1, keepdims=True)
    e_lo = N_GROUPS + g_sel * EXPERTS_PER_GROUP
    in_grp = (lane >= e_lo) & (lane < e_lo + EXPERTS_PER_GROUP)
    el = jnp.where(in_grp, logits, NEG)
    v1 = jnp.max(el, axis=-1, keepdims=True)
    i1 = jnp.min(jnp.where(in_grp & (el == v1), lane, LANES), axis=-1, keepdims=True)
    rest = in_grp & (lane != i1)
    el2 = jnp.where(rest, logits, NEG)
    v2 = jnp.max(el2, axis=-1, keepdims=True)
    i2 = jnp.min(jnp.where(rest & (el2 == v2), lane, LANES), axis=-1, keepdims=True)
    t = jnp.exp(v2 - v1)
    w1 = g_prob / (1.0 + t)
    w2 = g_prob * t / (1.0 + t)
    sel = lane + N_GROUPS
    gate_ref[...] = jnp.where(sel == i1, w1, 0.0) + jnp.where(sel == i2, w2, 0.0)


def _merge(x2, o_list, lse_list, c, om, wg, bg, wa, wc, wm, wo, g1, b1, wr, br, alpha, tm=256):
    T, D = x2.shape
    row = lambda w: pl.BlockSpec((tm, w), lambda i: (i, 0))
    kern = functools.partial(_merge_kernel, alpha=alpha)
    return pl.pallas_call(
        kern,
        grid=(T // tm,),
        in_specs=[row(D)] + [row(ATTN_OUT)] * 3 + [row(LANES)] * 3 + [row(CONV_CH), row(MEM_W),
                  _full((D, N_BRANCH * D)), _full((1, N_BRANCH * D)),
                  _full((ATTN_OUT, D)), _full((CONV_CH, D)), _full((MEM_W, D)), _full((D, D)),
                  _full((1, D)), _full((1, D)), _full((D, LANES)), _full((1, LANES))],
        out_specs=[row(D), row(D), row(LANES)],
        out_shape=[jax.ShapeDtypeStruct((T, D), F32), jax.ShapeDtypeStruct((T, D), BF16),
                   jax.ShapeDtypeStruct((T, LANES), F32)],
        compiler_params=_cparams(("parallel",)),
        name="merge_ln1_router",
    )(x2, *o_list, *lse_list, c, om, wg, bg, wa, wc, wm, wo, g1, b1, wr, br)


def _moe_kernel(xb_ref, x1_ref, gate_ref, wg_ref, wu_ref, wd_ref, g2_ref, b2_ref, out_ref, acc, *, alpha):
    e = pl.program_id(1)

    @pl.when(e == 0)
    def _():
        acc[...] = jnp.zeros_like(acc)

    xb = xb_ref[...]
    hg = jnp.dot(xb, wg_ref[...], preferred_element_type=F32)
    hu = jnp.dot(xb, wu_ref[...], preferred_element_type=F32)
    hid = hg * jax.nn.sigmoid(hg) * hu
    gates = gate_ref[...]
    lane = lax.broadcasted_iota(jnp.int32, gates.shape, 1)
    we = jnp.sum(jnp.where(lane == e, gates, 0.0), axis=-1, keepdims=True)
    acc[...] += we * jnp.dot(hid.astype(BF16), wd_ref[...], preferred_element_type=F32)

    @pl.when(e == N_EXPERTS - 1)
    def _():
        out_ref[...] = _layer_norm(alpha * x1_ref[...] + acc[...], g2_ref[...], b2_ref[...])


def _moe(x1b, x1, gates, wg, wu, wd, g2, b2, alpha, tm=1024):
    T, D = x1.shape
    kern = functools.partial(_moe_kernel, alpha=alpha)
    return pl.pallas_call(
        kern,
        grid=(T // tm, N_EXPERTS),
        in_specs=[
            pl.BlockSpec((tm, D), lambda i, e: (i, 0)),
            pl.BlockSpec((tm, D), lambda i, e: (i, 0)),
            pl.BlockSpec((tm, LANES), lambda i, e: (i, 0)),
            pl.BlockSpec((None, D, EXPERT_FF), lambda i, e: (e, 0, 0)),
            pl.BlockSpec((None, D, EXPERT_FF), lambda i, e: (e, 0, 0)),
            pl.BlockSpec((None, EXPERT_FF, D), lambda i, e: (e, 0, 0)),
            pl.BlockSpec((1, D), lambda i, e: (0, 0)),
            pl.BlockSpec((1, D), lambda i, e: (0, 0)),
        ],
        out_specs=pl.BlockSpec((tm, D), lambda i, e: (i, 0)),
        out_shape=jax.ShapeDtypeStruct((T, D), F32),
        scratch_shapes=[pltpu.VMEM((tm, D), F32)],
        compiler_params=_cparams(("parallel", "arbitrary")),
        name="moe_ln2",
    )(x1b, x1, gates, wg, wu, wd, g2, b2)


def _layer(x, mem, cos2, sin2, l, depth, w_in, b_in, conv_w, conv_b, conv_ln_g, conv_ln_b,
           w_mem_kv, w_attn_o, w_conv_o, w_mem_o, w_out, ln1_g, ln1_b,
           w_group_router, b_group_router, w_expert_router, b_expert_router,
           w_exp_gate, w_exp_up, w_exp_down, ln2_g, ln2_b):
    B, S, D = x.shape
    alpha = (2.0 * depth) ** 0.25
    w_in_b = w_in[l].astype(BF16)
    b_in_l = b_in[l].reshape(1, -1)

    o_list, lse_list = [], []
    for g, (_, dilation) in enumerate(DIL_PATTERNS):
        cols = [slice(p * ATTN_QK + g * ATTN_OUT, p * ATTN_QK + (g + 1) * ATTN_OUT) for p in range(3)]
        w = jnp.concatenate([w_in_b[:, c] for c in cols], axis=1)
        b = jnp.concatenate([b_in_l[:, c] for c in cols], axis=1)
        o_g, lse_g = _attn_group(x, cos2, sin2, w, b, dilation)
        o_list.append(o_g)
        lse_list.append(lse_g)

    c0 = 3 * ATTN_QK
    c1 = c0 + 2 * CONV_CH
    c2 = c1 + MEM_W
    row = lambda a: a[l].reshape(1, -1)
    c = _conv_branch(x, w_in_b[:, c0:c1], b_in_l[:, c0:c1], conv_w[l], row(conv_b),
                     row(conv_ln_g), row(conv_ln_b))
    om = _mem_branch(x, mem, w_in_b[:, c1:c2], b_in_l[:, c1:c2], w_mem_kv[l].astype(BF16))

    wr = jnp.concatenate([w_group_router[l],
                          w_expert_router[l].transpose(1, 0, 2).reshape(D, N_EXPERTS)], axis=1)
    wr = jnp.pad(wr, ((0, 0), (0, LANES - wr.shape[1])))
    br = jnp.concatenate([b_group_router[l], b_expert_router[l].reshape(-1)])
    br = jnp.pad(br, (0, LANES - br.shape[0])).reshape(1, LANES)

    x1, x1b, gates = _merge(
        x.reshape(B * S, D), o_list, lse_list, c, om,
        w_in_b[:, c2:], b_in_l[:, c2:], w_attn_o[l].astype(BF16), w_conv_o[l].astype(BF16),
        w_mem_o[l].astype(BF16), w_out[l].astype(BF16), row(ln1_g), row(ln1_b), wr, br, alpha)

    out = _moe(x1b, x1, gates, w_exp_gate[l].astype(BF16), w_exp_up[l].astype(BF16),
               w_exp_down[l].astype(BF16), row(ln2_g), row(ln2_b), alpha)
    return out.reshape(B, S, D)


def kernel(x, mem, positions, w_in, b_in, conv_w, conv_b, conv_ln_g, conv_ln_b, w_mem_kv, w_attn_o,
           w_conv_o, w_mem_o, w_out, ln1_g, ln1_b, w_group_router, b_group_router, w_expert_router,
           b_expert_router, w_exp_gate, w_exp_up, w_exp_down, ln2_g, ln2_b):
    depth = w_in.shape[0]
    cos2, sin2 = _rope_tables(positions)
    for l in range(depth):
        x = _layer(x, mem, cos2, sin2, l, depth, w_in, b_in, conv_w, conv_b, conv_ln_g, conv_ln_b,
                   w_mem_kv, w_attn_o, w_conv_o, w_mem_o, w_out, ln1_g, ln1_b,
                   w_group_router, b_group_router, w_expert_router, b_expert_router,
                   w_exp_gate, w_exp_up, w_exp_down, ln2_g, ln2_b)
    return x
```

```python
import functools
import math

import jax
import jax.numpy as jnp
from jax import lax
from jax.experimental import pallas as pl
from jax.experimental.pallas import tpu as pltpu

D_MODEL = 1024
HEAD_DIM = 128
ATTN_SLOTS = 4
DIL_PATTERNS = ((128, 1), (512, 4), (2048, 16))
N_DIL = len(DIL_PATTERNS)
ATTN_QK = N_DIL * ATTN_SLOTS * HEAD_DIM
ATTN_OUT = ATTN_SLOTS * HEAD_DIM
BAND_BLOCK = 128
ROPE_THETA = 10000.0
CONV_CH = 512
CONV_WIDTH = 31
MEM_HEADS = 4
MEM_HEAD_DIM = 128
MEM_W = MEM_HEADS * MEM_HEAD_DIM
N_BRANCH = 3
N_GROUPS = 4
EXPERTS_PER_GROUP = 4
N_EXPERTS = N_GROUPS * EXPERTS_PER_GROUP
EXPERT_FF = 512
LN_EPS = 1e-5

LANES = 128
SUBLANES = 8
BF16_ROWS = 16
MERGE_TM = 256
SORT_TM = 512
SORT_ROWS = 2 * SORT_TM + N_EXPERTS * BF16_ROWS
EXPERT_TM = 512
CONV_PAD = 32
VMEM_LIMIT = 56 * 1024 * 1024

F32 = jnp.float32
BF16 = jnp.bfloat16
NEG = -1e30


def _cparams(sem):
    return pltpu.CompilerParams(dimension_semantics=sem, vmem_limit_bytes=VMEM_LIMIT)


def _full(shape):
    n = len(shape)
    return pl.BlockSpec(shape, lambda *_: (0,) * n)


def _rope_kernel(pos_ref, inv_ref, cos_ref, sin_ref):
    ang = pos_ref[...].astype(F32) * inv_ref[...]
    lane = lax.broadcasted_iota(jnp.int32, ang.shape, 1)
    s = jnp.sin(ang)
    cos_ref[...] = jnp.cos(ang)
    sin_ref[...] = jnp.where(lane < HEAD_DIM // 2, -s, s)


def _rope_tables(positions):
    B, S = positions.shape
    half = HEAD_DIM // 2
    inv = ROPE_THETA ** (-jnp.arange(half, dtype=F32) / half)
    inv = jnp.concatenate([inv, inv]).reshape(1, HEAD_DIM)
    pos3 = positions.reshape(B, S, 1)
    return pl.pallas_call(
        _rope_kernel,
        grid=(B,),
        in_specs=[pl.BlockSpec((None, S, 1), lambda b: (b, 0, 0)), _full((1, HEAD_DIM))],
        out_specs=[pl.BlockSpec((None, S, HEAD_DIM), lambda b: (b, 0, 0))] * 2,
        out_shape=[jax.ShapeDtypeStruct((B, S, HEAD_DIM), F32)] * 2,
        compiler_params=_cparams(("parallel",)),
        name="rope_tables",
    )(pos3, inv)


def _attn_kernel(x_ref, cos_ref, sin_ref, w_ref, b_ref, o_ref, lse_ref, q_s, k_s, v_s, *, L, rc):
    k_s[0:BAND_BLOCK, :] = jnp.zeros((BAND_BLOCK, ATTN_OUT), BF16)
    v_s[0:BAND_BLOCK, :] = jnp.zeros((BAND_BLOCK, ATTN_OUT), BF16)
    scale = HEAD_DIM ** -0.5

    def proj(i, carry):
        r0 = pl.multiple_of(i * rc, rc)
        xs = x_ref[pl.ds(r0, rc), :].astype(BF16)
        qkv = jnp.dot(xs, w_ref[...], preferred_element_type=F32) + b_ref[...]
        c = cos_ref[pl.ds(r0, rc), :]
        s = sin_ref[pl.ds(r0, rc), :]
        for h in range(ATTN_SLOTS):
            lo = h * HEAD_DIM
            qh = qkv[:, lo:lo + HEAD_DIM]
            qh = (qh * c + pltpu.roll(qh, HEAD_DIM // 2, 1) * s) * scale
            q_s[pl.ds(r0, rc), lo:lo + HEAD_DIM] = qh.astype(BF16)
            kh = qkv[:, ATTN_OUT + lo:ATTN_OUT + lo + HEAD_DIM]
            kh = kh * c + pltpu.roll(kh, HEAD_DIM // 2, 1) * s
            k_s[pl.ds(BAND_BLOCK + r0, rc), lo:lo + HEAD_DIM] = kh.astype(BF16)
        v_s[pl.ds(BAND_BLOCK + r0, rc), :] = qkv[:, 2 * ATTN_OUT:3 * ATTN_OUT].astype(BF16)
        return carry

    lax.fori_loop(0, L // rc, proj, 0)

    qi = lax.broadcasted_iota(jnp.int32, (BAND_BLOCK, 2 * BAND_BLOCK), 0)
    kj = lax.broadcasted_iota(jnp.int32, (BAND_BLOCK, 2 * BAND_BLOCK), 1)
    dist = BAND_BLOCK + qi - kj
    band = (dist >= 0) & (dist <= BAND_BLOCK)
    lane = lax.broadcasted_iota(jnp.int32, (BAND_BLOCK, LANES), 1)

    def blk(n, carry):
        r0 = pl.multiple_of(n * BAND_BLOCK, BAND_BLOCK)
        valid = band & ((kj >= BAND_BLOCK) | (n > 0))
        lse_tile = jnp.zeros((BAND_BLOCK, LANES), F32)
        for h in range(ATTN_SLOTS):
            lo = h * HEAD_DIM
            q = q_s[pl.ds(r0, BAND_BLOCK), lo:lo + HEAD_DIM]
            kk = k_s[pl.ds(r0, 2 * BAND_BLOCK), lo:lo + HEAD_DIM]
            vv = v_s[pl.ds(r0, 2 * BAND_BLOCK), lo:lo + HEAD_DIM]
            s = lax.dot_general(q, kk, (((1,), (1,)), ((), ())), preferred_element_type=F32)
            s = jnp.where(valid, s, NEG)
            m = jnp.max(s, axis=-1, keepdims=True)
            p = jnp.exp(s - m)
            l = jnp.sum(p, axis=-1, keepdims=True)
            o = jnp.dot(p.astype(BF16), vv, preferred_element_type=F32) / l
            o_ref[pl.ds(r0, BAND_BLOCK), lo:lo + HEAD_DIM] = o.astype(o_ref.dtype)
            lse_h = m + jnp.log(l)
            seg = LANES // ATTN_SLOTS
            lse_tile = jnp.where((lane >= h * seg) & (lane < (h + 1) * seg), lse_h, lse_tile)
        lse_ref[pl.ds(r0, BAND_BLOCK), :] = lse_tile
        return carry

    lax.fori_loop(0, L // BAND_BLOCK, blk, 0)


def _attn_group(x, cos2, sin2, w, b, dilation):
    B, S, D = x.shape
    r = dilation
    L = S // r
    rc = min(L, 256)
    xv = x.reshape(B, L, r * D)
    cv = cos2.reshape(B, L, r * HEAD_DIM)
    sv = sin2.reshape(B, L, r * HEAD_DIM)
    kern = functools.partial(_attn_kernel, L=L, rc=rc)
    o, lse = pl.pallas_call(
        kern,
        grid=(B, r),
        in_specs=[
            pl.BlockSpec((None, L, D), lambda bb, j: (bb, 0, j)),
            pl.BlockSpec((None, L, HEAD_DIM), lambda bb, j: (bb, 0, j)),
            pl.BlockSpec((None, L, HEAD_DIM), lambda bb, j: (bb, 0, j)),
            _full((D, 3 * ATTN_OUT)),
            _full((1, 3 * ATTN_OUT)),
        ],
        out_specs=[
            pl.BlockSpec((None, L, ATTN_OUT), lambda bb, j: (bb, 0, j)),
            pl.BlockSpec((None, L, LANES), lambda bb, j: (bb, 0, j)),
        ],
        out_shape=[
            jax.ShapeDtypeStruct((B, L, r * ATTN_OUT), BF16),
            jax.ShapeDtypeStruct((B, L, r * LANES), F32),
        ],
        scratch_shapes=[
            pltpu.VMEM((L, ATTN_OUT), BF16),
            pltpu.VMEM((BAND_BLOCK + L, ATTN_OUT), BF16),
            pltpu.VMEM((BAND_BLOCK + L, ATTN_OUT), BF16),
        ],
        compiler_params=_cparams(("parallel", "parallel")),
        name=f"dilated_attn_r{r}",
    )(xv, cv, sv, w, b)
    return o.reshape(B * S, ATTN_OUT), lse.reshape(B * S, LANES)


def _conv_kernel(x_ref, w_ref, b_ref, cw_ref, cb_ref, g_ref, beta_ref, out_ref, cpad, *, S, rc):
    cpad[0:CONV_PAD, :] = jnp.zeros((CONV_PAD, CONV_CH), F32)

    def glu(i, carry):
        r0 = pl.multiple_of(i * rc, rc)
        xs = x_ref[pl.ds(r0, rc), :].astype(BF16)
        u = jnp.dot(xs, w_ref[...], preferred_element_type=F32) + b_ref[...]
        cpad[pl.ds(CONV_PAD + r0, rc), :] = u[:, :CONV_CH] * jax.nn.sigmoid(u[:, CONV_CH:])
        return carry

    lax.fori_loop(0, S // rc, glu, 0)

    off = CONV_PAD - (CONV_WIDTH - 1)

    def conv(i, carry):
        r0 = pl.multiple_of(i * rc, rc)
        win = cpad[pl.ds(r0, rc + CONV_PAD), :]
        acc = jnp.zeros((rc, CONV_CH), F32) + cb_ref[...]
        for j in range(CONV_WIDTH):
            acc = acc + win[j + off:j + off + rc, :] * cw_ref[j:j + 1, :]
        mu = jnp.mean(acc, axis=-1, keepdims=True)
        d = acc - mu
        var = jnp.mean(d * d, axis=-1, keepdims=True)
        y = d * lax.rsqrt(var + LN_EPS) * g_ref[...] + beta_ref[...]
        out_ref[pl.ds(r0, rc), :] = (y * jax.nn.sigmoid(y)).astype(out_ref.dtype)
        return carry

    lax.fori_loop(0, S // rc, conv, 0)


def _conv_branch(x, w_glu, b_glu, conv_w, conv_b, ln_g, ln_b):
    B, S, D = x.shape
    rc = 256
    cw = jnp.pad(conv_w, ((0, CONV_PAD - CONV_WIDTH), (0, 0)))
    kern = functools.partial(_conv_kernel, S=S, rc=rc)
    out = pl.pallas_call(
        kern,
        grid=(B,),
        in_specs=[
            pl.BlockSpec((None, S, D), lambda b: (b, 0, 0)),
            _full((D, 2 * CONV_CH)), _full((1, 2 * CONV_CH)),
            _full((CONV_PAD, CONV_CH)), _full((1, CONV_CH)), _full((1, CONV_CH)), _full((1, CONV_CH)),
        ],
        out_specs=pl.BlockSpec((None, S, CONV_CH), lambda b: (b, 0, 0)),
        out_shape=jax.ShapeDtypeStruct((B, S, CONV_CH), BF16),
        scratch_shapes=[pltpu.VMEM((CONV_PAD + S, CONV_CH), F32)],
        compiler_params=_cparams(("parallel",)),
        name="conformer_conv",
    )(x, w_glu, b_glu, cw, conv_b, ln_g, ln_b)
    return out.reshape(B * S, CONV_CH)


def _mem_kernel(x_ref, mem_ref, wq_ref, bq_ref, wkv_ref, out_ref, *, S, rc):
    kv = jnp.dot(mem_ref[...].astype(BF16), wkv_ref[...], preferred_element_type=F32)
    km = kv[:, :MEM_W].astype(BF16)
    vm = kv[:, MEM_W:].astype(BF16)
    scale = MEM_HEAD_DIM ** -0.5

    def body(i, carry):
        r0 = pl.multiple_of(i * rc, rc)
        xs = x_ref[pl.ds(r0, rc), :].astype(BF16)
        q = (jnp.dot(xs, wq_ref[...], preferred_element_type=F32) + bq_ref[...]) * scale
        q = q.astype(BF16)
        for h in range(MEM_HEADS):
            lo = h * MEM_HEAD_DIM
            s = lax.dot_general(q[:, lo:lo + MEM_HEAD_DIM], km[:, lo:lo + MEM_HEAD_DIM],
                                (((1,), (1,)), ((), ())), preferred_element_type=F32)
            m = jnp.max(s, axis=-1, keepdims=True)
            p = jnp.exp(s - m)
            l = jnp.sum(p, axis=-1, keepdims=True)
            o = jnp.dot(p.astype(BF16), vm[:, lo:lo + MEM_HEAD_DIM], preferred_element_type=F32) / l
            out_ref[pl.ds(r0, rc), lo:lo + MEM_HEAD_DIM] = o.astype(out_ref.dtype)
        return carry

    lax.fori_loop(0, S // rc, body, 0)


def _mem_branch(x, mem, w_q, b_q, w_kv):
    B, S, D = x.shape
    M = mem.shape[1]
    rc = 256
    kern = functools.partial(_mem_kernel, S=S, rc=rc)
    out = pl.pallas_call(
        kern,
        grid=(B,),
        in_specs=[
            pl.BlockSpec((None, S, D), lambda b: (b, 0, 0)),
            pl.BlockSpec((None, M, D), lambda b: (b, 0, 0)),
            _full((D, MEM_W)), _full((1, MEM_W)), _full((D, 2 * MEM_W)),
        ],
        out_specs=pl.BlockSpec((None, S, MEM_W), lambda b: (b, 0, 0)),
        out_shape=jax.ShapeDtypeStruct((B, S, MEM_W), BF16),
        compiler_params=_cparams(("parallel",)),
        name="memory_attn",
    )(x, mem, w_q, b_q, w_kv)
    return out.reshape(B * S, MEM_W)


def _layer_norm(z, g, b):
    mu = jnp.mean(z, axis=-1, keepdims=True)
    d = z - mu
    var = jnp.mean(d * d, axis=-1, keepdims=True)
    return d * lax.rsqrt(var + LN_EPS) * g + b


def _merge_kernel(x_ref, o0_ref, o1_ref, o2_ref, l0_ref, l1_ref, l2_ref, c_ref, m_ref,
                  wg_ref, bg_ref, wa_ref, wc_ref, wm_ref, wo_ref, g1_ref, b1_ref, wr_ref, br_ref,
                  x1_ref, x1b_ref, rinfo_ref, cnt_ref, *, alpha):
    x = x_ref[...]
    xb = x.astype(BF16)
    l0, l1, l2 = l0_ref[...], l1_ref[...], l2_ref[...]
    mx = jnp.maximum(jnp.maximum(l0, l1), l2)
    e0, e1, e2 = jnp.exp(l0 - mx), jnp.exp(l1 - mx), jnp.exp(l2 - mx)
    inv = 1.0 / (e0 + e1 + e2)
    seg = LANES // ATTN_SLOTS
    parts = []
    for h in range(ATTN_SLOTS):
        lo = h * HEAD_DIM
        w0 = (e0 * inv)[:, h * seg:h * seg + 1]
        w1 = (e1 * inv)[:, h * seg:h * seg + 1]
        w2 = (e2 * inv)[:, h * seg:h * seg + 1]
        parts.append(w0 * o0_ref[:, lo:lo + HEAD_DIM].astype(F32)
                     + w1 * o1_ref[:, lo:lo + HEAD_DIM].astype(F32)
                     + w2 * o2_ref[:, lo:lo + HEAD_DIM].astype(F32))
    o_attn = jnp.concatenate(parts, axis=-1).astype(BF16)

    def gate(k):
        lo = k * D_MODEL
        return jax.nn.sigmoid(jnp.dot(xb, wg_ref[:, lo:lo + D_MODEL], preferred_element_type=F32)
                              + bg_ref[:, lo:lo + D_MODEL])

    merged = gate(0) * jnp.dot(o_attn, wa_ref[...], preferred_element_type=F32)
    merged = merged + gate(1) * jnp.dot(c_ref[...], wc_ref[...], preferred_element_type=F32)
    merged = merged + gate(2) * jnp.dot(m_ref[...], wm_ref[...], preferred_element_type=F32)
    mix = jnp.dot(merged.astype(BF16), wo_ref[...], preferred_element_type=F32)
    x1 = _layer_norm(alpha * x + mix, g1_ref[...], b1_ref[...])
    x1_ref[...] = x1
    x1b_ref[...] = x1.astype(BF16)

    logits = jnp.dot(x1, wr_ref[...], preferred_element_type=F32,
                     precision=lax.Precision.HIGHEST) + br_ref[...]
    lane = lax.broadcasted_iota(jnp.int32, logits.shape, 1)
    is_g = lane < N_GROUPS
    gl = jnp.where(is_g, logits, NEG)
    gmax = jnp.max(gl, axis=-1, keepdims=True)
    g_sel = jnp.min(jnp.where(is_g & (gl == gmax), lane, LANES), axis=-1, keepdims=True)
    g_prob = 1.0 / jnp.sum(jnp.where(is_g, jnp.exp(gl - gmax), 0.0), axis=-1, keepdims=True)
    e_lo = N_GROUPS + g_sel * EXPERTS_PER_GROUP
    in_grp = (lane >= e_lo) & (lane < e_lo + EXPERTS_PER_GROUP)
    el = jnp.where(in_grp, logits, NEG)
    v1 = jnp.max(el, axis=-1, keepdims=True)
    i1 = jnp.min(jnp.where(in_grp & (el == v1), lane, LANES), axis=-1, keepdims=True)
    rest = in_grp & (lane != i1)
    el2 = jnp.where(rest, logits, NEG)
    v2 = jnp.max(el2, axis=-1, keepdims=True)
    i2 = jnp.min(jnp.where(rest & (el2 == v2), lane, LANES), axis=-1, keepdims=True)
    t = jnp.exp(v2 - v1)
    w1 = g_prob / (1.0 + t)
    w2 = g_prob * t / (1.0 + t)
    e1 = (i1 - N_GROUPS).astype(F32)
    e2 = (i2 - N_GROUPS).astype(F32)
    rinfo_ref[...] = jnp.where(lane == 0, e1, jnp.where(lane == 1, e2, jnp.where(
        lane == 2, w1, jnp.where(lane == 3, w2, 0.0))))
    sel = lane + N_GROUPS
    hot = jnp.where((sel == i1) | (sel == i2), 1.0, 0.0)
    cnt_ref[...] = jnp.broadcast_to(jnp.sum(hot, axis=0, keepdims=True), cnt_ref.shape)


def _merge(x2, o_list, lse_list, c, om, wg, bg, wa, wc, wm, wo, g1, b1, wr, br, alpha, tm=MERGE_TM):
    T, D = x2.shape
    row = lambda w: pl.BlockSpec((tm, w), lambda i: (i, 0))
    kern = functools.partial(_merge_kernel, alpha=alpha)
    return pl.pallas_call(
        kern,
        grid=(T // tm,),
        in_specs=[row(D)] + [row(ATTN_OUT)] * 3 + [row(LANES)] * 3 + [row(CONV_CH), row(MEM_W),
                  _full((D, N_BRANCH * D)), _full((1, N_BRANCH * D)),
                  _full((ATTN_OUT, D)), _full((CONV_CH, D)), _full((MEM_W, D)), _full((D, D)),
                  _full((1, D)), _full((1, D)), _full((D, LANES)), _full((1, LANES))],
        out_specs=[row(D), row(D), row(LANES), pl.BlockSpec((None, SUBLANES, LANES), lambda i: (i, 0, 0))],
        out_shape=[jax.ShapeDtypeStruct((T, D), F32), jax.ShapeDtypeStruct((T, D), BF16),
                   jax.ShapeDtypeStruct((T, LANES), F32),
                   jax.ShapeDtypeStruct((T // tm, SUBLANES, LANES), F32)],
        compiler_params=_cparams(("parallel",)),
        name="merge_ln1_router",
    )(x2, *o_list, *lse_list, c, om, wg, bg, wa, wc, wm, wo, g1, b1, wr, br)


SEG_BITS = tuple(range((SORT_TM // BF16_ROWS).bit_length() - 1, -1, -1))
GAP_BITS = tuple(range((EXPERT_TM // BF16_ROWS - 1).bit_length() - 1, -1, -1))


def _segment_copies(units, bits, make_copy, act):
    for b in bits:
        v = 1 << b

        @pl.when((units & v) != 0)
        def _():
            start = pl.multiple_of((units & (-2 * v)) * BF16_ROWS, BF16_ROWS)
            act(make_copy(start, v * BF16_ROWS))


def _sort_positions(rinfo, lo_vec, tri_ref):
    lane = lax.broadcasted_iota(jnp.int32, rinfo.shape, 1)
    e1 = rinfo[:, 0:1].astype(jnp.int32)
    e2 = rinfo[:, 1:2].astype(jnp.int32)
    hot = jnp.where((lane == e1) | (lane == e2), 1.0, 0.0).astype(BF16)
    before = jnp.dot(tri_ref[...], hot, preferred_element_type=F32) + lo_vec
    lp1 = jnp.sum(jnp.where(lane == e1, before, 0.0), axis=-1, keepdims=True)
    lp2 = jnp.sum(jnp.where(lane == e2, before, 0.0), axis=-1, keepdims=True)
    return lp1, lp2


def _dispatch_kernel(np_ref, lo_ref, g_ref, gap_ref, gapdst_ref,
                     xb_ref, rinfo_ref, lov_ref, tri_ref, xs_ref, pos_ref, sorted_s, zero_s, sem):
    i = pl.program_id(0)
    rinfo = rinfo_ref[...]
    lp1, lp2 = _sort_positions(rinfo, lov_ref[...], tri_ref)
    lane = lax.broadcasted_iota(jnp.int32, rinfo.shape, 1)
    pos = jnp.where(lane == 0, lp1, jnp.where(lane == 1, lp2, 0.0))
    pos_ref[...] = pos
    pos_t = pos.T
    p_iota = lax.broadcasted_iota(jnp.int32, (SORT_ROWS, SORT_TM), 0)
    perm = (p_iota == pos_t[0:1, :].astype(jnp.int32)) | (p_iota == pos_t[1:2, :].astype(jnp.int32))
    perm = jnp.where(perm, 1.0, 0.0).astype(BF16)
    sorted_s[...] = jnp.dot(perm, xb_ref[...], preferred_element_type=F32).astype(BF16)

    def seg(e):
        src0 = pl.multiple_of(lo_ref[i * N_EXPERTS + e], BF16_ROWS)
        dst0 = pl.multiple_of(g_ref[i * N_EXPERTS + e], BF16_ROWS)
        return lambda start, rows: pltpu.make_async_copy(
            sorted_s.at[pl.ds(src0 + start, rows), :], xs_ref.at[pl.ds(dst0 + start, rows), :], sem)

    def gap(e):
        dst0 = pl.multiple_of(gapdst_ref[e], BF16_ROWS)
        return lambda start, rows: pltpu.make_async_copy(
            zero_s.at[pl.ds(0, rows), :], xs_ref.at[pl.ds(dst0 + start, rows), :], sem)

    zrows = zero_s.shape[0]

    def tail(r):
        return pltpu.make_async_copy(zero_s, xs_ref.at[pl.ds(pl.multiple_of(r * zrows, zrows), zrows), :], sem)

    @pl.when(i == 0)
    def _():
        zero_s[...] = jnp.zeros_like(zero_s)
        for e in range(N_EXPERTS):
            _segment_copies(gap_ref[e], GAP_BITS, gap(e), lambda cp: cp.start())
        tail_lo = gap_ref[N_EXPERTS] * (EXPERT_TM // zrows)
        tail_hi = xs_ref.shape[0] // zrows
        lax.fori_loop(tail_lo, tail_hi, lambda r, c: (tail(r).start(), c)[1], 0)
        for e in range(N_EXPERTS):
            _segment_copies(gap_ref[e], GAP_BITS, gap(e), lambda cp: cp.wait())
        lax.fori_loop(tail_lo, tail_hi, lambda r, c: (tail(r).wait(), c)[1], 0)

    for e in range(N_EXPERTS):
        _segment_copies(np_ref[i * N_EXPERTS + e], SEG_BITS, seg(e), lambda cp: cp.start())
    for e in range(N_EXPERTS):
        _segment_copies(np_ref[i * N_EXPERTS + e], SEG_BITS, seg(e), lambda cp: cp.wait())


def _dispatch(x1b, rinfo, np_t, lo_t, g_t, gap_t, gapdst_t, lo_vec, n_rows):
    T, D = x1b.shape
    n_tiles = T // SORT_TM
    tri = (jnp.arange(SORT_TM)[:, None] > jnp.arange(SORT_TM)[None, :]).astype(BF16)
    gap_rows = BF16_ROWS << GAP_BITS[0]
    return pl.pallas_call(
        _dispatch_kernel,
        grid_spec=pltpu.PrefetchScalarGridSpec(
            num_scalar_prefetch=5,
            grid=(n_tiles,),
            in_specs=[
                pl.BlockSpec((SORT_TM, D), lambda i, *_: (i, 0)),
                pl.BlockSpec((SORT_TM, LANES), lambda i, *_: (i, 0)),
                pl.BlockSpec((None, 1, LANES), lambda i, *_: (i, 0, 0)),
                pl.BlockSpec((SORT_TM, SORT_TM), lambda i, *_: (0, 0)),
            ],
            out_specs=[pl.BlockSpec(memory_space=pl.ANY),
                       pl.BlockSpec((SORT_TM, LANES), lambda i, *_: (i, 0))],
            scratch_shapes=[pltpu.VMEM((SORT_ROWS, D), BF16), pltpu.VMEM((gap_rows, D), BF16),
                            pltpu.SemaphoreType.DMA(())],
        ),
        out_shape=[jax.ShapeDtypeStruct((n_rows, D), BF16), jax.ShapeDtypeStruct((T, LANES), F32)],
        compiler_params=_cparams(("arbitrary",)),
        name="moe_dispatch",
    )(np_t, lo_t, g_t, gap_t, gapdst_t, x1b, rinfo, lo_vec, tri)


def _expert_kernel(blk_e_ref, used_ref, xs_ref, wg_ref, wu_ref, wd_ref, ys_ref):
    j = pl.program_id(0)

    @pl.when(j < used_ref[0])
    def _():
        xb = xs_ref[...]
        hg = jnp.dot(xb, wg_ref[...], preferred_element_type=F32)
        hu = jnp.dot(xb, wu_ref[...], preferred_element_type=F32)
        hid = hg * jax.nn.sigmoid(hg) * hu
        ys_ref[...] = jnp.dot(hid.astype(BF16), wd_ref[...], preferred_element_type=F32).astype(ys_ref.dtype)

    @pl.when(j >= used_ref[0])
    def _():
        ys_ref[...] = jnp.zeros_like(ys_ref)


def _experts(xs, blk_e, used, wg, wu, wd):
    n_rows, D = xs.shape
    return pl.pallas_call(
        _expert_kernel,
        grid_spec=pltpu.PrefetchScalarGridSpec(
            num_scalar_prefetch=2,
            grid=(n_rows // EXPERT_TM,),
            in_specs=[
                pl.BlockSpec((EXPERT_TM, D), lambda j, be, u: (j, 0)),
                pl.BlockSpec((None, D, EXPERT_FF), lambda j, be, u: (be[j], 0, 0)),
                pl.BlockSpec((None, D, EXPERT_FF), lambda j, be, u: (be[j], 0, 0)),
                pl.BlockSpec((None, EXPERT_FF, D), lambda j, be, u: (be[j], 0, 0)),
            ],
            out_specs=pl.BlockSpec((EXPERT_TM, D), lambda j, be, u: (j, 0)),
        ),
        out_shape=jax.ShapeDtypeStruct((n_rows, D), BF16),
        compiler_params=_cparams(("arbitrary",)),
        name="moe_experts",
    )(blk_e, used, xs, wg, wu, wd)


def _combine_kernel(np_ref, lo_ref, g_ref, ys_ref, rinfo_ref, pos_ref, x1_ref, g2_ref, b2_ref,
                    out_ref, sorted_s, sem, *, alpha):
    i = pl.program_id(0)

    @pl.when(i == 0)
    def _():
        sorted_s[...] = jnp.zeros_like(sorted_s)

    def seg(e):
        dst0 = pl.multiple_of(lo_ref[i * N_EXPERTS + e], BF16_ROWS)
        src0 = pl.multiple_of(g_ref[i * N_EXPERTS + e], BF16_ROWS)
        return lambda start, rows: pltpu.make_async_copy(
            ys_ref.at[pl.ds(src0 + start, rows), :], sorted_s.at[pl.ds(dst0 + start, rows), :], sem)

    for e in range(N_EXPERTS):
        _segment_copies(np_ref[i * N_EXPERTS + e], SEG_BITS, seg(e), lambda cp: cp.start())
    rinfo = rinfo_ref[...]
    pos = pos_ref[...]
    p_iota = lax.broadcasted_iota(jnp.int32, (SORT_TM, SORT_ROWS), 1)
    sel1 = jnp.where(p_iota == pos[:, 0:1].astype(jnp.int32), 1.0, 0.0).astype(BF16)
    sel2 = jnp.where(p_iota == pos[:, 1:2].astype(jnp.int32), 1.0, 0.0).astype(BF16)
    for e in range(N_EXPERTS):
        _segment_copies(np_ref[i * N_EXPERTS + e], SEG_BITS, seg(e), lambda cp: cp.wait())
    ys = sorted_s[...]
    y = (rinfo[:, 2:3] * jnp.dot(sel1, ys, preferred_element_type=F32)
         + rinfo[:, 3:4] * jnp.dot(sel2, ys, preferred_element_type=F32))
    out_ref[...] = _layer_norm(alpha * x1_ref[...] + y, g2_ref[...], b2_ref[...])


def _combine(ys, rinfo, pos, x1, np_t, lo_t, g_t, g2, b2, alpha):
    T, D = x1.shape
    kern = functools.partial(_combine_kernel, alpha=alpha)
    return pl.pallas_call(
        kern,
        grid_spec=pltpu.PrefetchScalarGridSpec(
            num_scalar_prefetch=3,
            grid=(T // SORT_TM,),
            in_specs=[
                pl.BlockSpec(memory_space=pl.ANY),
                pl.BlockSpec((SORT_TM, LANES), lambda i, *_: (i, 0)),
                pl.BlockSpec((SORT_TM, LANES), lambda i, *_: (i, 0)),
                pl.BlockSpec((SORT_TM, D), lambda i, *_: (i, 0)),
                pl.BlockSpec((1, D), lambda i, *_: (0, 0)),
                pl.BlockSpec((1, D), lambda i, *_: (0, 0)),
            ],
            out_specs=pl.BlockSpec((SORT_TM, D), lambda i, *_: (i, 0)),
            scratch_shapes=[pltpu.VMEM((SORT_ROWS, D), BF16), pltpu.SemaphoreType.DMA(())],
        ),
        out_shape=jax.ShapeDtypeStruct((T, D), F32),
        compiler_params=_cparams(("arbitrary",)),
        name="moe_combine_ln2",
    )(np_t, lo_t, g_t, ys, rinfo, pos, x1, g2, b2)


def _ceil_to(a, m):
    return (a + m - 1) // m * m


def _moe(x1b, x1, rinfo, cnt, wg, wu, wd, g2, b2, alpha):
    T, D = x1.shape
    n_tiles = T // SORT_TM
    i32 = jnp.int32
    n = cnt[:, 0, :N_EXPERTS].astype(i32).reshape(n_tiles, SORT_TM // MERGE_TM, N_EXPERTS).sum(axis=1)
    npad = _ceil_to(n, BF16_ROWS)
    lo = jnp.cumsum(npad, axis=1) - npad
    tot = npad.sum(axis=0)
    reg = _ceil_to(tot, EXPERT_TM)
    off = jnp.cumsum(reg) - reg
    g = off[None, :] + jnp.cumsum(npad, axis=0) - npad
    n_blocks = T * 2 // EXPERT_TM + n_tiles * N_EXPERTS * (BF16_ROWS - 1) // EXPERT_TM + N_EXPERTS
    n_rows = n_blocks * EXPERT_TM
    blk_end = jnp.cumsum(reg) // EXPERT_TM
    j = jnp.arange(n_blocks, dtype=i32)
    used = blk_end[-1]
    jj = jnp.minimum(j, used - 1)
    blk_e = jnp.sum(jj[:, None] >= blk_end[None, :], axis=1).astype(i32)
    lo_vec = jnp.pad(lo.astype(F32), ((0, 0), (0, LANES - N_EXPERTS))).reshape(n_tiles, 1, LANES)
    flat = lambda a: a.astype(i32).reshape(-1)
    gap_t = jnp.concatenate([flat((reg - tot) // BF16_ROWS), flat(used)])

    xs, pos = _dispatch(x1b, rinfo, flat(npad // BF16_ROWS), flat(lo), flat(g),
                        gap_t, flat(off + tot), lo_vec, n_rows)
    ys = _experts(xs, blk_e, flat(used), wg, wu, wd)
    return _combine(ys, rinfo, pos, x1, flat(npad // BF16_ROWS), flat(lo), flat(g), g2, b2, alpha)


def _layer(x, mem, cos2, sin2, l, depth, w_in, b_in, conv_w, conv_b, conv_ln_g, conv_ln_b,
           w_mem_kv, w_attn_o, w_conv_o, w_mem_o, w_out, ln1_g, ln1_b,
           w_group_router, b_group_router, w_expert_router, b_expert_router,
           w_exp_gate, w_exp_up, w_exp_down, ln2_g, ln2_b):
    B, S, D = x.shape
    alpha = (2.0 * depth) ** 0.25
    w_in_b = w_in[l].astype(BF16)
    b_in_l = b_in[l].reshape(1, -1)

    o_list, lse_list = [], []
    for g, (_, dilation) in enumerate(DIL_PATTERNS):
        cols = [slice(p * ATTN_QK + g * ATTN_OUT, p * ATTN_QK + (g + 1) * ATTN_OUT) for p in range(3)]
        w = jnp.concatenate([w_in_b[:, c] for c in cols], axis=1)
        b = jnp.concatenate([b_in_l[:, c] for c in cols], axis=1)
        o_g, lse_g = _attn_group(x, cos2, sin2, w, b, dilation)
        o_list.append(o_g)
        lse_list.append(lse_g)

    c0 = 3 * ATTN_QK
    c1 = c0 + 2 * CONV_CH
    c2 = c1 + MEM_W
    row = lambda a: a[l].reshape(1, -1)
    c = _conv_branch(x, w_in_b[:, c0:c1], b_in_l[:, c0:c1], conv_w[l], row(conv_b),
                     row(conv_ln_g), row(conv_ln_b))
    om = _mem_branch(x, mem, w_in_b[:, c1:c2], b_in_l[:, c1:c2], w_mem_kv[l].astype(BF16))

    wr = jnp.concatenate([w_group_router[l],
                          w_expert_router[l].transpose(1, 0, 2).reshape(D, N_EXPERTS)], axis=1)
    wr = jnp.pad(wr, ((0, 0), (0, LANES - wr.shape[1])))
    br = jnp.concatenate([b_group_router[l], b_expert_router[l].reshape(-1)])
    br = jnp.pad(br, (0, LANES - br.shape[0])).reshape(1, LANES)

    x1, x1b, rinfo, cnt = _merge(
        x.reshape(B * S, D), o_list, lse_list, c, om,
        w_in_b[:, c2:], b_in_l[:, c2:], w_attn_o[l].astype(BF16), w_conv_o[l].astype(BF16),
        w_mem_o[l].astype(BF16), w_out[l].astype(BF16), row(ln1_g), row(ln1_b), wr, br, alpha)

    out = _moe(x1b, x1, rinfo, cnt, w_exp_gate[l].astype(BF16), w_exp_up[l].astype(BF16),
               w_exp_down[l].astype(BF16), row(ln2_g), row(ln2_b), alpha)
    return out.reshape(B, S, D)


def kernel(x, mem, positions, w_in, b_in, conv_w, conv_b, conv_ln_g, conv_ln_b, w_mem_kv, w_attn_o,
           w_conv_o, w_mem_o, w_out, ln1_g, ln1_b, w_group_router, b_group_router, w_expert_router,
           b_expert_router, w_exp_gate, w_exp_up, w_exp_down, ln2_g, ln2_b):
    depth = w_in.shape[0]
    cos2, sin2 = _rope_tables(positions)
    for l in range(depth):
        x = _layer(x, mem, cos2, sin2, l, depth, w_in, b_in, conv_w, conv_b, conv_ln_g, conv_ln_b,
                   w_mem_kv, w_attn_o, w_conv_o, w_mem_o, w_out, ln1_g, ln1_b,
                   w_group_router, b_group_router, w_expert_router, b_expert_router,
                   w_exp_gate, w_exp_up, w_exp_down, ln2_g, ln2_b)
    return x
```

```python
import functools
import math

import jax
import jax.numpy as jnp
from jax import lax
from jax.experimental import pallas as pl
from jax.experimental.pallas import tpu as pltpu

D_MODEL = 1024
HEAD_DIM = 128
ATTN_SLOTS = 4
DIL_PATTERNS = ((128, 1), (512, 4), (2048, 16))
N_DIL = len(DIL_PATTERNS)
ATTN_QK = N_DIL * ATTN_SLOTS * HEAD_DIM
ATTN_OUT = ATTN_SLOTS * HEAD_DIM
BAND_BLOCK = 128
ROPE_THETA = 10000.0
CONV_CH = 512
CONV_WIDTH = 31
MEM_HEADS = 4
MEM_HEAD_DIM = 128
MEM_W = MEM_HEADS * MEM_HEAD_DIM
N_BRANCH = 3
N_GROUPS = 4
EXPERTS_PER_GROUP = 4
N_EXPERTS = N_GROUPS * EXPERTS_PER_GROUP
EXPERT_FF = 512
LN_EPS = 1e-5

LANES = 128
SUBLANES = 8
BF16_ROWS = 16
MERGE_TM = 512
SORT_TM = 512
SORT_ROWS = 2 * SORT_TM + N_EXPERTS * BF16_ROWS
EXPERT_TM = 512
CONV_PAD = 32
VMEM_LIMIT = 56 * 1024 * 1024

F32 = jnp.float32
BF16 = jnp.bfloat16
NEG = -1e30


def _cparams(sem):
    return pltpu.CompilerParams(dimension_semantics=sem, vmem_limit_bytes=VMEM_LIMIT)


def _full(shape):
    n = len(shape)
    return pl.BlockSpec(shape, lambda *_: (0,) * n)


def _rope_kernel(pos_ref, inv_ref, cos_ref, sin_ref):
    ang = pos_ref[...].astype(F32) * inv_ref[...]
    lane = lax.broadcasted_iota(jnp.int32, ang.shape, 1)
    s = jnp.sin(ang)
    cos_ref[...] = jnp.cos(ang)
    sin_ref[...] = jnp.where(lane < HEAD_DIM // 2, -s, s)


def _rope_tables(positions):
    B, S = positions.shape
    half = HEAD_DIM // 2
    inv = ROPE_THETA ** (-jnp.arange(half, dtype=F32) / half)
    inv = jnp.concatenate([inv, inv]).reshape(1, HEAD_DIM)
    pos3 = positions.reshape(B, S, 1)
    return pl.pallas_call(
        _rope_kernel,
        grid=(B,),
        in_specs=[pl.BlockSpec((None, S, 1), lambda b: (b, 0, 0)), _full((1, HEAD_DIM))],
        out_specs=[pl.BlockSpec((None, S, HEAD_DIM), lambda b: (b, 0, 0))] * 2,
        out_shape=[jax.ShapeDtypeStruct((B, S, HEAD_DIM), F32)] * 2,
        compiler_params=_cparams(("parallel",)),
        name="rope_tables",
    )(pos3, inv)


def _attn_kernel(x_ref, cos_ref, sin_ref, w_ref, b_ref, o_ref, q_s, k_s, v_s, acc_s, m_s, l_s, *, S, rc):
    g = pl.program_id(1)
    scale = HEAD_DIM ** -0.5
    seg = LANES // ATTN_SLOTS

    @pl.when(g == 0)
    def _():
        m_s[...] = jnp.full(m_s.shape, NEG, F32)
        l_s[...] = jnp.zeros(l_s.shape, F32)
        acc_s[...] = jnp.zeros(acc_s.shape, F32)

    def proj(i, carry):
        r0 = pl.multiple_of(i * rc, rc)
        xs = x_ref[pl.ds(r0, rc), :].astype(BF16)
        qkv = jnp.dot(xs, w_ref[...], preferred_element_type=F32) + b_ref[...]
        c = cos_ref[pl.ds(r0, rc), :]
        s = sin_ref[pl.ds(r0, rc), :]
        for h in range(ATTN_SLOTS):
            lo = h * HEAD_DIM
            qh = qkv[:, lo:lo + HEAD_DIM]
            q_s[h, pl.ds(r0, rc), :] = (qh * c + pltpu.roll(qh, HEAD_DIM // 2, 1) * s) * scale
            kh = qkv[:, ATTN_OUT + lo:ATTN_OUT + lo + HEAD_DIM]
            k_s[h, pl.ds(r0, rc), :] = kh * c + pltpu.roll(kh, HEAD_DIM // 2, 1) * s
            v_s[h, pl.ds(r0, rc), :] = qkv[:, 2 * ATTN_OUT + lo:2 * ATTN_OUT + lo + HEAD_DIM]
        return carry

    lax.fori_loop(0, S // rc, proj, 0)

    qi = lax.broadcasted_iota(jnp.int32, (BAND_BLOCK, 2 * BAND_BLOCK), 0)
    kj = lax.broadcasted_iota(jnp.int32, (BAND_BLOCK, 2 * BAND_BLOCK), 1)
    dist = BAND_BLOCK + qi - kj
    band = (dist >= 0) & (dist <= BAND_BLOCK)
    lane = lax.broadcasted_iota(jnp.int32, (BAND_BLOCK, LANES), 1)

    def attend(r):
        span = BAND_BLOCK * r

        def blk(c, carry):
            res = c % r
            n = c // r
            start = n * span + res
            prev = jnp.maximum(n - 1, 0) * span + res
            valid = band & ((kj >= BAND_BLOCK) | (n > 0))
            cur_rows = pl.ds(start, BAND_BLOCK, stride=r)
            prev_rows = pl.ds(prev, BAND_BLOCK, stride=r)
            m_tile = m_s[cur_rows, :]
            l_tile = l_s[cur_rows, :]
            for h in range(ATTN_SLOTS):
                q = q_s[h, cur_rows, :].astype(BF16)
                kk = jnp.concatenate([k_s[h, prev_rows, :], k_s[h, cur_rows, :]], axis=0).astype(BF16)
                vv = jnp.concatenate([v_s[h, prev_rows, :], v_s[h, cur_rows, :]], axis=0).astype(BF16)
                s = lax.dot_general(q, kk, (((1,), (1,)), ((), ())), preferred_element_type=F32)
                s = jnp.where(valid, s, NEG)
                m_old = m_tile[:, h * seg:h * seg + 1]
                l_old = l_tile[:, h * seg:h * seg + 1]
                m_new = jnp.maximum(m_old, jnp.max(s, axis=-1, keepdims=True))
                a = jnp.exp(m_old - m_new)
                p = jnp.exp(s - m_new)
                l_new = a * l_old + jnp.sum(p, axis=-1, keepdims=True)
                pv = jnp.dot(p.astype(BF16), vv, preferred_element_type=F32)
                acc_s[h, cur_rows, :] = a * acc_s[h, cur_rows, :] + pv
                in_h = (lane >= h * seg) & (lane < (h + 1) * seg)
                m_tile = jnp.where(in_h, m_new, m_tile)
                l_tile = jnp.where(in_h, l_new, l_tile)
            m_s[cur_rows, :] = m_tile
            l_s[cur_rows, :] = l_tile
            return carry

        lax.fori_loop(0, S // BAND_BLOCK, blk, 0)

    for gi, (_, dilation) in enumerate(DIL_PATTERNS):
        pl.when(g == gi)(functools.partial(attend, dilation))

    @pl.when(g == N_DIL - 1)
    def _():
        def fin(i, carry):
            r0 = pl.multiple_of(i * rc, rc)
            lt = l_s[pl.ds(r0, rc), :]
            for h in range(ATTN_SLOTS):
                o = acc_s[h, pl.ds(r0, rc), :] / lt[:, h * seg:h * seg + 1]
                o_ref[pl.ds(r0, rc), h * HEAD_DIM:(h + 1) * HEAD_DIM] = o.astype(o_ref.dtype)
            return carry

        lax.fori_loop(0, S // rc, fin, 0)


def _attention(x, cos2, sin2, w, b):
    B, S, D = x.shape
    rc = 512
    gw = 3 * ATTN_OUT
    slab = pltpu.VMEM((ATTN_SLOTS, S, HEAD_DIM), F32)
    kern = functools.partial(_attn_kernel, S=S, rc=rc)
    o = pl.pallas_call(
        kern,
        grid=(B, N_DIL),
        in_specs=[
            pl.BlockSpec((None, S, D), lambda bb, g: (bb, 0, 0), pipeline_mode=pl.Buffered(1)),
            pl.BlockSpec((None, S, HEAD_DIM), lambda bb, g: (bb, 0, 0)),
            pl.BlockSpec((None, S, HEAD_DIM), lambda bb, g: (bb, 0, 0)),
            pl.BlockSpec((D, gw), lambda bb, g: (0, g)),
            pl.BlockSpec((1, gw), lambda bb, g: (0, g)),
        ],
        out_specs=pl.BlockSpec((None, S, ATTN_OUT), lambda bb, g: (bb, 0, 0)),
        out_shape=jax.ShapeDtypeStruct((B, S, ATTN_OUT), BF16),
        scratch_shapes=[slab, slab, slab, slab, pltpu.VMEM((S, LANES), F32), pltpu.VMEM((S, LANES), F32)],
        compiler_params=_cparams(("parallel", "arbitrary")),
        name="dilated_attn",
    )(x, cos2, sin2, w, b)
    return o.reshape(B * S, ATTN_OUT)


def _conv_kernel(x_ref, w_ref, b_ref, cw_ref, cb_ref, g_ref, beta_ref, out_ref, cpad, shift_s, *, S, rp, rc):
    cpad[0:CONV_PAD, :] = jnp.zeros((CONV_PAD, CONV_CH), F32)

    def glu(i, carry):
        r0 = pl.multiple_of(i * rp, rp)
        xs = x_ref[pl.ds(r0, rp), :].astype(BF16)
        u = jnp.dot(xs, w_ref[...], preferred_element_type=F32) + b_ref[...]
        cpad[pl.ds(CONV_PAD + r0, rp), :] = u[:, :CONV_CH] * jax.nn.sigmoid(u[:, CONV_CH:])
        return carry

    lax.fori_loop(0, S // rp, glu, 0)

    off = CONV_PAD - (CONV_WIDTH - 1)
    srows = shift_s.shape[1]

    def conv(i, carry):
        r0 = pl.multiple_of(i * rc, rc)
        win = cpad[pl.ds(r0, rc + CONV_PAD), :]
        for ph in range(1, SUBLANES):
            shift_s[ph - 1, :, :] = win[ph:ph + srows, :]
        acc = jnp.zeros((rc, CONV_CH), F32) + cb_ref[...]
        for j in range(CONV_WIDTH):
            a, ph = divmod(j + off, SUBLANES)
            if ph == 0:
                tap = win[a * SUBLANES:a * SUBLANES + rc, :]
            else:
                tap = shift_s[ph - 1, a * SUBLANES:a * SUBLANES + rc, :]
            acc = acc + tap * cw_ref[j:j + 1, :]
        mu = jnp.mean(acc, axis=-1, keepdims=True)
        d = acc - mu
        var = jnp.mean(d * d, axis=-1, keepdims=True)
        y = d * lax.rsqrt(var + LN_EPS) * g_ref[...] + beta_ref[...]
        out_ref[pl.ds(r0, rc), :] = (y * jax.nn.sigmoid(y)).astype(out_ref.dtype)
        return carry

    lax.fori_loop(0, S // rc, conv, 0)


def _conv_branch(x, w_glu, b_glu, conv_w, conv_b, ln_g, ln_b):
    B, S, D = x.shape
    rp, rc = 512, 256
    cw = jnp.pad(conv_w, ((0, CONV_PAD - CONV_WIDTH), (0, 0)))
    kern = functools.partial(_conv_kernel, S=S, rp=rp, rc=rc)
    out = pl.pallas_call(
        kern,
        grid=(B,),
        in_specs=[
            pl.BlockSpec((None, S, D), lambda b: (b, 0, 0)),
            _full((D, 2 * CONV_CH)), _full((1, 2 * CONV_CH)),
            _full((CONV_PAD, CONV_CH)), _full((1, CONV_CH)), _full((1, CONV_CH)), _full((1, CONV_CH)),
        ],
        out_specs=pl.BlockSpec((None, S, CONV_CH), lambda b: (b, 0, 0)),
        out_shape=jax.ShapeDtypeStruct((B, S, CONV_CH), BF16),
        scratch_shapes=[pltpu.VMEM((CONV_PAD + S, CONV_CH), F32),
                        pltpu.VMEM((SUBLANES - 1, rc + CONV_PAD - SUBLANES, CONV_CH), F32)],
        compiler_params=_cparams(("parallel",)),
        name="conformer_conv",
    )(x, w_glu, b_glu, cw, conv_b, ln_g, ln_b)
    return out.reshape(B * S, CONV_CH)


def _mem_kernel(x_ref, mem_ref, wq_ref, bq_ref, wkv_ref, out_ref, *, S, rc):
    kv = jnp.dot(mem_ref[...].astype(BF16), wkv_ref[...], preferred_element_type=F32)
    km = kv[:, :MEM_W].astype(BF16)
    vm = kv[:, MEM_W:].astype(BF16)
    scale = MEM_HEAD_DIM ** -0.5

    def body(i, carry):
        r0 = pl.multiple_of(i * rc, rc)
        xs = x_ref[pl.ds(r0, rc), :].astype(BF16)
        q = (jnp.dot(xs, wq_ref[...], preferred_element_type=F32) + bq_ref[...]) * scale
        q = q.astype(BF16)
        for h in range(MEM_HEADS):
            lo = h * MEM_HEAD_DIM
            s = lax.dot_general(q[:, lo:lo + MEM_HEAD_DIM], km[:, lo:lo + MEM_HEAD_DIM],
                                (((1,), (1,)), ((), ())), preferred_element_type=F32)
            m = jnp.max(s, axis=-1, keepdims=True)
            p = jnp.exp(s - m)
            l = jnp.sum(p, axis=-1, keepdims=True)
            o = jnp.dot(p.astype(BF16), vm[:, lo:lo + MEM_HEAD_DIM], preferred_element_type=F32) / l
            out_ref[pl.ds(r0, rc), lo:lo + MEM_HEAD_DIM] = o.astype(out_ref.dtype)
        return carry

    lax.fori_loop(0, S // rc, body, 0)


def _mem_branch(x, mem, w_q, b_q, w_kv):
    B, S, D = x.shape
    M = mem.shape[1]
    rc = 512
    kern = functools.partial(_mem_kernel, S=S, rc=rc)
    out = pl.pallas_call(
        kern,
        grid=(B,),
        in_specs=[
            pl.BlockSpec((None, S, D), lambda b: (b, 0, 0)),
            pl.BlockSpec((None, M, D), lambda b: (b, 0, 0)),
            _full((D, MEM_W)), _full((1, MEM_W)), _full((D, 2 * MEM_W)),
        ],
        out_specs=pl.BlockSpec((None, S, MEM_W), lambda b: (b, 0, 0)),
        out_shape=jax.ShapeDtypeStruct((B, S, MEM_W), BF16),
        compiler_params=_cparams(("parallel",)),
        name="memory_attn",
    )(x, mem, w_q, b_q, w_kv)
    return out.reshape(B * S, MEM_W)


def _layer_norm(z, g, b):
    mu = jnp.mean(z, axis=-1, keepdims=True)
    d = z - mu
    var = jnp.mean(d * d, axis=-1, keepdims=True)
    return d * lax.rsqrt(var + LN_EPS) * g + b


def _merge_kernel(x_ref, a_ref, c_ref, m_ref,
                  wg_ref, bg_ref, wa_ref, wc_ref, wm_ref, wo_ref, g1_ref, b1_ref, wr_ref, br_ref,
                  x1_ref, x1b_ref, rinfo_ref, cnt_ref, *, alpha):
    x = x_ref[...]
    xb = x.astype(BF16)

    def gate(k):
        lo = k * D_MODEL
        return jax.nn.sigmoid(jnp.dot(xb, wg_ref[:, lo:lo + D_MODEL], preferred_element_type=F32)
                              + bg_ref[:, lo:lo + D_MODEL])

    merged = gate(0) * jnp.dot(a_ref[...], wa_ref[...], preferred_element_type=F32)
    merged = merged + gate(1) * jnp.dot(c_ref[...], wc_ref[...], preferred_element_type=F32)
    merged = merged + gate(2) * jnp.dot(m_ref[...], wm_ref[...], preferred_element_type=F32)
    mix = jnp.dot(merged.astype(BF16), wo_ref[...], preferred_element_type=F32)
    x1 = _layer_norm(alpha * x + mix, g1_ref[...], b1_ref[...])
    x1_ref[...] = x1
    x1b_ref[...] = x1.astype(BF16)

    x1_hi = x1.astype(BF16)
    x1_lo = (x1 - x1_hi.astype(F32)).astype(BF16)
    t = jnp.dot(x1_hi, wr_ref[...], preferred_element_type=F32)
    logits = (t[:, :LANES] + t[:, LANES:]
              + jnp.dot(x1_lo, wr_ref[:, :LANES], preferred_element_type=F32) + br_ref[...])
    lane = lax.broadcasted_iota(jnp.int32, logits.shape, 1)
    is_g = lane < N_GROUPS
    gl = jnp.where(is_g, logits, NEG)
    gmax = jnp.max(gl, axis=-1, keepdims=True)
    g_sel = jnp.min(jnp.where(is_g & (gl == gmax), lane, LANES), axis=-1, keepdims=True)
    g_prob = 1.0 / jnp.sum(jnp.where(is_g, jnp.exp(gl - gmax), 0.0), axis=-1, keepdims=True)
    e_lo = N_GROUPS + g_sel * EXPERTS_PER_GROUP
    in_grp = (lane >= e_lo) & (lane < e_lo + EXPERTS_PER_GROUP)
    el = jnp.where(in_grp, logits, NEG)
    v1 = jnp.max(el, axis=-1, keepdims=True)
    i1 = jnp.min(jnp.where(in_grp & (el == v1), lane, LANES), axis=-1, keepdims=True)
    rest = in_grp & (lane != i1)
    el2 = jnp.where(rest, logits, NEG)
    v2 = jnp.max(el2, axis=-1, keepdims=True)
    i2 = jnp.min(jnp.where(rest & (el2 == v2), lane, LANES), axis=-1, keepdims=True)
    t = jnp.exp(v2 - v1)
    w1 = g_prob / (1.0 + t)
    w2 = g_prob * t / (1.0 + t)
    e1 = (i1 - N_GROUPS).astype(F32)
    e2 = (i2 - N_GROUPS).astype(F32)
    rinfo_ref[...] = jnp.where(lane == 0, e1, jnp.where(lane == 1, e2, jnp.where(
        lane == 2, w1, jnp.where(lane == 3, w2, 0.0))))
    sel = lane + N_GROUPS
    hot = jnp.where((sel == i1) | (sel == i2), 1.0, 0.0)
    cnt_ref[...] = jnp.broadcast_to(jnp.sum(hot, axis=0, keepdims=True), cnt_ref.shape)


def _merge(x2, a, c, om, wg, bg, wa, wc, wm, wo, g1, b1, wr, br, alpha, tm=MERGE_TM):
    T, D = x2.shape
    row = lambda w: pl.BlockSpec((tm, w), lambda i: (i, 0))
    kern = functools.partial(_merge_kernel, alpha=alpha)
    return pl.pallas_call(
        kern,
        grid=(T // tm,),
        in_specs=[row(D), row(ATTN_OUT), row(CONV_CH), row(MEM_W),
                  _full((D, N_BRANCH * D)), _full((1, N_BRANCH * D)),
                  _full((ATTN_OUT, D)), _full((CONV_CH, D)), _full((MEM_W, D)), _full((D, D)),
                  _full((1, D)), _full((1, D)), _full((D, 2 * LANES)), _full((1, LANES))],
        out_specs=[row(D), row(D), row(LANES), pl.BlockSpec((None, SUBLANES, LANES), lambda i: (i, 0, 0))],
        out_shape=[jax.ShapeDtypeStruct((T, D), F32), jax.ShapeDtypeStruct((T, D), BF16),
                   jax.ShapeDtypeStruct((T, LANES), F32),
                   jax.ShapeDtypeStruct((T // tm, SUBLANES, LANES), F32)],
        compiler_params=_cparams(("parallel",)),
        name="merge_ln1_router",
    )(x2, a, c, om, wg, bg, wa, wc, wm, wo, g1, b1, wr, br)


SEG_BITS = tuple(range((SORT_TM // BF16_ROWS).bit_length() - 1, -1, -1))
GAP_BITS = tuple(range((EXPERT_TM // BF16_ROWS - 1).bit_length() - 1, -1, -1))


def _segment_copies(units, bits, make_copy, act):
    for b in bits:
        v = 1 << b

        @pl.when((units & v) != 0)
        def _():
            start = pl.multiple_of((units & (-2 * v)) * BF16_ROWS, BF16_ROWS)
            act(make_copy(start, v * BF16_ROWS))


def _sort_positions(rinfo, lo_vec, tri_ref):
    lane = lax.broadcasted_iota(jnp.int32, rinfo.shape, 1)
    e1 = rinfo[:, 0:1].astype(jnp.int32)
    e2 = rinfo[:, 1:2].astype(jnp.int32)
    hot = jnp.where((lane == e1) | (lane == e2), 1.0, 0.0).astype(BF16)
    before = jnp.dot(tri_ref[...], hot, preferred_element_type=F32) + lo_vec
    lp1 = jnp.sum(jnp.where(lane == e1, before, 0.0), axis=-1, keepdims=True)
    lp2 = jnp.sum(jnp.where(lane == e2, before, 0.0), axis=-1, keepdims=True)
    return lp1, lp2


def _dispatch_kernel(np_ref, lo_ref, g_ref, gap_ref, gapdst_ref,
                     xb_ref, rinfo_ref, lov_ref, tri_ref, xs_ref, pos_ref, sorted_s, zero_s, sem):
    i = pl.program_id(0)
    rinfo = rinfo_ref[...]
    lp1, lp2 = _sort_positions(rinfo, lov_ref[...], tri_ref)
    lane = lax.broadcasted_iota(jnp.int32, rinfo.shape, 1)
    pos = jnp.where(lane == 0, lp1, jnp.where(lane == 1, lp2, 0.0))
    pos_ref[...] = pos
    pos_t = pos.T
    p_iota = lax.broadcasted_iota(jnp.int32, (SORT_ROWS, SORT_TM), 0)
    perm = (p_iota == pos_t[0:1, :].astype(jnp.int32)) | (p_iota == pos_t[1:2, :].astype(jnp.int32))
    perm = jnp.where(perm, 1.0, 0.0).astype(BF16)
    sorted_s[...] = jnp.dot(perm, xb_ref[...], preferred_element_type=F32).astype(BF16)

    def seg(e):
        src0 = pl.multiple_of(lo_ref[i * N_EXPERTS + e], BF16_ROWS)
        dst0 = pl.multiple_of(g_ref[i * N_EXPERTS + e], BF16_ROWS)
        return lambda start, rows: pltpu.make_async_copy(
            sorted_s.at[pl.ds(src0 + start, rows), :], xs_ref.at[pl.ds(dst0 + start, rows), :], sem)

    def gap(e):
        dst0 = pl.multiple_of(gapdst_ref[e], BF16_ROWS)
        return lambda start, rows: pltpu.make_async_copy(
            zero_s.at[pl.ds(0, rows), :], xs_ref.at[pl.ds(dst0 + start, rows), :], sem)

    zrows = zero_s.shape[0]

    def tail(r):
        return pltpu.make_async_copy(zero_s, xs_ref.at[pl.ds(pl.multiple_of(r * zrows, zrows), zrows), :], sem)

    @pl.when(i == 0)
    def _():
        zero_s[...] = jnp.zeros_like(zero_s)
        for e in range(N_EXPERTS):
            _segment_copies(gap_ref[e], GAP_BITS, gap(e), lambda cp: cp.start())
        tail_lo = gap_ref[N_EXPERTS] * (EXPERT_TM // zrows)
        tail_hi = xs_ref.shape[0] // zrows
        lax.fori_loop(tail_lo, tail_hi, lambda r, c: (tail(r).start(), c)[1], 0)
        for e in range(N_EXPERTS):
            _segment_copies(gap_ref[e], GAP_BITS, gap(e), lambda cp: cp.wait())
        lax.fori_loop(tail_lo, tail_hi, lambda r, c: (tail(r).wait(), c)[1], 0)

    for e in range(N_EXPERTS):
        _segment_copies(np_ref[i * N_EXPERTS + e], SEG_BITS, seg(e), lambda cp: cp.start())
    for e in range(N_EXPERTS):
        _segment_copies(np_ref[i * N_EXPERTS + e], SEG_BITS, seg(e), lambda cp: cp.wait())


def _dispatch(x1b, rinfo, np_t, lo_t, g_t, gap_t, gapdst_t, lo_vec, n_rows):
    T, D = x1b.shape
    n_tiles = T // SORT_TM
    tri = (jnp.arange(SORT_TM)[:, None] > jnp.arange(SORT_TM)[None, :]).astype(BF16)
    gap_rows = BF16_ROWS << GAP_BITS[0]
    return pl.pallas_call(
        _dispatch_kernel,
        grid_spec=pltpu.PrefetchScalarGridSpec(
            num_scalar_prefetch=5,
            grid=(n_tiles,),
            in_specs=[
                pl.BlockSpec((SORT_TM, D), lambda i, *_: (i, 0)),
                pl.BlockSpec((SORT_TM, LANES), lambda i, *_: (i, 0)),
                pl.BlockSpec((None, 1, LANES), lambda i, *_: (i, 0, 0)),
                pl.BlockSpec((SORT_TM, SORT_TM), lambda i, *_: (0, 0)),
            ],
            out_specs=[pl.BlockSpec(memory_space=pl.ANY),
                       pl.BlockSpec((SORT_TM, LANES), lambda i, *_: (i, 0))],
            scratch_shapes=[pltpu.VMEM((SORT_ROWS, D), BF16), pltpu.VMEM((gap_rows, D), BF16),
                            pltpu.SemaphoreType.DMA(())],
        ),
        out_shape=[jax.ShapeDtypeStruct((n_rows, D), BF16), jax.ShapeDtypeStruct((T, LANES), F32)],
        compiler_params=_cparams(("arbitrary",)),
        name="moe_dispatch",
    )(np_t, lo_t, g_t, gap_t, gapdst_t, x1b, rinfo, lo_vec, tri)


def _expert_kernel(blk_e_ref, used_ref, xs_ref, wg_ref, wu_ref, wd_ref, ys_ref):
    j = pl.program_id(0)

    @pl.when(j < used_ref[0])
    def _():
        xb = xs_ref[...]
        hg = jnp.dot(xb, wg_ref[...], preferred_element_type=F32)
        hu = jnp.dot(xb, wu_ref[...], preferred_element_type=F32)
        hid = hg * jax.nn.sigmoid(hg) * hu
        ys_ref[...] = jnp.dot(hid.astype(BF16), wd_ref[...], preferred_element_type=F32).astype(ys_ref.dtype)

    @pl.when(j >= used_ref[0])
    def _():
        ys_ref[...] = jnp.zeros_like(ys_ref)


def _experts(xs, blk_e, used, wg, wu, wd):
    n_rows, D = xs.shape
    return pl.pallas_call(
        _expert_kernel,
        grid_spec=pltpu.PrefetchScalarGridSpec(
            num_scalar_prefetch=2,
            grid=(n_rows // EXPERT_TM,),
            in_specs=[
                pl.BlockSpec((EXPERT_TM, D), lambda j, be, u: (j, 0)),
                pl.BlockSpec((None, D, EXPERT_FF), lambda j, be, u: (be[j], 0, 0)),
                pl.BlockSpec((None, D, EXPERT_FF), lambda j, be, u: (be[j], 0, 0)),
                pl.BlockSpec((None, EXPERT_FF, D), lambda j, be, u: (be[j], 0, 0)),
            ],
            out_specs=pl.BlockSpec((EXPERT_TM, D), lambda j, be, u: (j, 0)),
        ),
        out_shape=jax.ShapeDtypeStruct((n_rows, D), BF16),
        compiler_params=_cparams(("arbitrary",)),
        name="moe_experts",
    )(blk_e, used, xs, wg, wu, wd)


def _combine_kernel(np_ref, lo_ref, g_ref, ys_ref, rinfo_ref, pos_ref, x1_ref, g2_ref, b2_ref,
                    out_ref, sorted_s, sem, *, alpha):
    i = pl.program_id(0)

    @pl.when(i == 0)
    def _():
        sorted_s[...] = jnp.zeros_like(sorted_s)

    def seg(e):
        dst0 = pl.multiple_of(lo_ref[i * N_EXPERTS + e], BF16_ROWS)
        src0 = pl.multiple_of(g_ref[i * N_EXPERTS + e], BF16_ROWS)
        return lambda start, rows: pltpu.make_async_copy(
            ys_ref.at[pl.ds(src0 + start, rows), :], sorted_s.at[pl.ds(dst0 + start, rows), :], sem)

    for e in range(N_EXPERTS):
        _segment_copies(np_ref[i * N_EXPERTS + e], SEG_BITS, seg(e), lambda cp: cp.start())
    rinfo = rinfo_ref[...]
    pos = pos_ref[...]
    p_iota = lax.broadcasted_iota(jnp.int32, (SORT_TM, SORT_ROWS), 1)
    sel1 = jnp.where(p_iota == pos[:, 0:1].astype(jnp.int32), 1.0, 0.0).astype(BF16)
    sel2 = jnp.where(p_iota == pos[:, 1:2].astype(jnp.int32), 1.0, 0.0).astype(BF16)
    for e in range(N_EXPERTS):
        _segment_copies(np_ref[i * N_EXPERTS + e], SEG_BITS, seg(e), lambda cp: cp.wait())
    ys = sorted_s[...]
    y = (rinfo[:, 2:3] * jnp.dot(sel1, ys, preferred_element_type=F32)
         + rinfo[:, 3:4] * jnp.dot(sel2, ys, preferred_element_type=F32))
    out_ref[...] = _layer_norm(alpha * x1_ref[...] + y, g2_ref[...], b2_ref[...])


def _combine(ys, rinfo, pos, x1, np_t, lo_t, g_t, g2, b2, alpha):
    T, D = x1.shape
    kern = functools.partial(_combine_kernel, alpha=alpha)
    return pl.pallas_call(
        kern,
        grid_spec=pltpu.PrefetchScalarGridSpec(
            num_scalar_prefetch=3,
            grid=(T // SORT_TM,),
            in_specs=[
                pl.BlockSpec(memory_space=pl.ANY),
                pl.BlockSpec((SORT_TM, LANES), lambda i, *_: (i, 0)),
                pl.BlockSpec((SORT_TM, LANES), lambda i, *_: (i, 0)),
                pl.BlockSpec((SORT_TM, D), lambda i, *_: (i, 0)),
                pl.BlockSpec((1, D), lambda i, *_: (0, 0)),
                pl.BlockSpec((1, D), lambda i, *_: (0, 0)),
            ],
            out_specs=pl.BlockSpec((SORT_TM, D), lambda i, *_: (i, 0)),
            scratch_shapes=[pltpu.VMEM((SORT_ROWS, D), BF16), pltpu.SemaphoreType.DMA(())],
        ),
        out_shape=jax.ShapeDtypeStruct((T, D), F32),
        compiler_params=_cparams(("arbitrary",)),
        name="moe_combine_ln2",
    )(np_t, lo_t, g_t, ys, rinfo, pos, x1, g2, b2)


def _ceil_to(a, m):
    return (a + m - 1) // m * m


def _moe(x1b, x1, rinfo, cnt, wg, wu, wd, g2, b2, alpha):
    T, D = x1.shape
    n_tiles = T // SORT_TM
    i32 = jnp.int32
    n = cnt[:, 0, :N_EXPERTS].astype(i32).reshape(n_tiles, SORT_TM // MERGE_TM, N_EXPERTS).sum(axis=1)
    npad = _ceil_to(n, BF16_ROWS)
    lo = jnp.cumsum(npad, axis=1) - npad
    tot = npad.sum(axis=0)
    reg = _ceil_to(tot, EXPERT_TM)
    off = jnp.cumsum(reg) - reg
    g = off[None, :] + jnp.cumsum(npad, axis=0) - npad
    n_blocks = T * 2 // EXPERT_TM + n_tiles * N_EXPERTS * (BF16_ROWS - 1) // EXPERT_TM + N_EXPERTS
    n_rows = n_blocks * EXPERT_TM
    blk_end = jnp.cumsum(reg) // EXPERT_TM
    j = jnp.arange(n_blocks, dtype=i32)
    used = blk_end[-1]
    jj = jnp.minimum(j, used - 1)
    blk_e = jnp.sum(jj[:, None] >= blk_end[None, :], axis=1).astype(i32)
    lo_vec = jnp.pad(lo.astype(F32), ((0, 0), (0, LANES - N_EXPERTS))).reshape(n_tiles, 1, LANES)
    flat = lambda a: a.astype(i32).reshape(-1)
    gap_t = jnp.concatenate([flat((reg - tot) // BF16_ROWS), flat(used)])

    xs, pos = _dispatch(x1b, rinfo, flat(npad // BF16_ROWS), flat(lo), flat(g),
                        gap_t, flat(off + tot), lo_vec, n_rows)
    ys = _experts(xs, blk_e, flat(used), wg, wu, wd)
    return _combine(ys, rinfo, pos, x1, flat(npad // BF16_ROWS), flat(lo), flat(g), g2, b2, alpha)


def _layer(x, mem, cos2, sin2, l, depth, w_in, b_in, conv_w, conv_b, conv_ln_g, conv_ln_b,
           w_mem_kv, w_attn_o, w_conv_o, w_mem_o, w_out, ln1_g, ln1_b,
           w_group_router, b_group_router, w_expert_router, b_expert_router,
           w_exp_gate, w_exp_up, w_exp_down, ln2_g, ln2_b):
    B, S, D = x.shape
    alpha = (2.0 * depth) ** 0.25
    w_in_b = w_in[l].astype(BF16)
    b_in_l = b_in[l].reshape(1, -1)

    cols = [slice(p * ATTN_QK + g * ATTN_OUT, p * ATTN_QK + (g + 1) * ATTN_OUT)
            for g in range(N_DIL) for p in range(3)]
    o_attn = _attention(x, cos2, sin2, jnp.concatenate([w_in_b[:, c] for c in cols], axis=1),
                        jnp.concatenate([b_in_l[:, c] for c in cols], axis=1))

    c0 = 3 * ATTN_QK
    c1 = c0 + 2 * CONV_CH
    c2 = c1 + MEM_W
    row = lambda a: a[l].reshape(1, -1)
    c = _conv_branch(x, w_in_b[:, c0:c1], b_in_l[:, c0:c1], conv_w[l], row(conv_b),
                     row(conv_ln_g), row(conv_ln_b))
    om = _mem_branch(x, mem, w_in_b[:, c1:c2], b_in_l[:, c1:c2], w_mem_kv[l].astype(BF16))

    wr = jnp.concatenate([w_group_router[l],
                          w_expert_router[l].transpose(1, 0, 2).reshape(D, N_EXPERTS)], axis=1)
    wr = jnp.pad(wr, ((0, 0), (0, LANES - wr.shape[1])))
    wr_hi = wr.astype(BF16)
    wr = jnp.concatenate([wr_hi, (wr - wr_hi.astype(F32)).astype(BF16)], axis=1)
    br = jnp.concatenate([b_group_router[l], b_expert_router[l].reshape(-1)])
    br = jnp.pad(br, (0, LANES - br.shape[0])).reshape(1, LANES)

    x1, x1b, rinfo, cnt = _merge(
        x.reshape(B * S, D), o_attn, c, om,
        w_in_b[:, c2:], b_in_l[:, c2:], w_attn_o[l].astype(BF16), w_conv_o[l].astype(BF16),
        w_mem_o[l].astype(BF16), w_out[l].astype(BF16), row(ln1_g), row(ln1_b), wr, br, alpha)

    out = _moe(x1b, x1, rinfo, cnt, w_exp_gate[l].astype(BF16), w_exp_up[l].astype(BF16),
               w_exp_down[l].astype(BF16), row(ln2_g), row(ln2_b), alpha)
    return out.reshape(B, S, D)


def kernel(x, mem, positions, w_in, b_in, conv_w, conv_b, conv_ln_g, conv_ln_b, w_mem_kv, w_attn_o,
           w_conv_o, w_mem_o, w_out, ln1_g, ln1_b, w_group_router, b_group_router, w_expert_router,
           b_expert_router, w_exp_gate, w_exp_up, w_exp_down, ln2_g, ln2_b):
    depth = w_in.shape[0]
    cos2, sin2 = _rope_tables(positions)
    for l in range(depth):
        x = _layer(x, mem, cos2, sin2, l, depth, w_in, b_in, conv_w, conv_b, conv_ln_g, conv_ln_b,
                   w_mem_kv, w_attn_o, w_conv_o, w_mem_o, w_out, ln1_g, ln1_b,
                   w_group_router, b_group_router, w_expert_router, b_expert_router,
                   w_exp_gate, w_exp_up, w_exp_down, ln2_g, ln2_b)
    return x
```

```python
import functools
import math

import jax
import jax.numpy as jnp
from jax import lax
from jax.experimental import pallas as pl
from jax.experimental.pallas import tpu as pltpu

D_MODEL = 1024
HEAD_DIM = 128
ATTN_SLOTS = 4
DIL_PATTERNS = ((128, 1), (512, 4), (2048, 16))
N_DIL = len(DIL_PATTERNS)
ATTN_QK = N_DIL * ATTN_SLOTS * HEAD_DIM
ATTN_OUT = ATTN_SLOTS * HEAD_DIM
BAND_BLOCK = 128
ROPE_THETA = 10000.0
CONV_CH = 512
CONV_WIDTH = 31
MEM_HEADS = 4
MEM_HEAD_DIM = 128
MEM_W = MEM_HEADS * MEM_HEAD_DIM
N_BRANCH = 3
N_GROUPS = 4
EXPERTS_PER_GROUP = 4
N_EXPERTS = N_GROUPS * EXPERTS_PER_GROUP
EXPERT_FF = 512
LN_EPS = 1e-5

LANES = 128
SUBLANES = 8
BF16_ROWS = 16
BLOCKS_PER_ITER = 2
ATTN_GROUP_IDX = (2, 0, 1)
ATTN_GROUP_ORDER = tuple(DIL_PATTERNS[i][1] for i in ATTN_GROUP_IDX)
MERGE_TM = 512
SORT_TM = 512
SORT_ROWS = 2 * SORT_TM + N_EXPERTS * BF16_ROWS
EXPERT_TM = 512
CONV_PAD = 32
VMEM_LIMIT = 56 * 1024 * 1024

F32 = jnp.float32
BF16 = jnp.bfloat16
NEG = -1e30


def _cparams(sem):
    return pltpu.CompilerParams(dimension_semantics=sem, vmem_limit_bytes=VMEM_LIMIT)


def _full(shape):
    n = len(shape)
    return pl.BlockSpec(shape, lambda *_: (0,) * n)


def _rope_kernel(pos_ref, inv_ref, cos_ref, sin_ref):
    ang = pos_ref[...].astype(F32) * inv_ref[...]
    lane = lax.broadcasted_iota(jnp.int32, ang.shape, 1)
    s = jnp.sin(ang)
    cos_ref[...] = jnp.cos(ang)
    sin_ref[...] = jnp.where(lane < HEAD_DIM // 2, -s, s)


def _rope_tables(positions):
    B, S = positions.shape
    half = HEAD_DIM // 2
    inv = ROPE_THETA ** (-jnp.arange(half, dtype=F32) / half)
    inv = jnp.concatenate([inv, inv]).reshape(1, HEAD_DIM)
    pos3 = positions.reshape(B, S, 1)
    return pl.pallas_call(
        _rope_kernel,
        grid=(B,),
        in_specs=[pl.BlockSpec((None, S, 1), lambda b: (b, 0, 0)), _full((1, HEAD_DIM))],
        out_specs=[pl.BlockSpec((None, S, HEAD_DIM), lambda b: (b, 0, 0))] * 2,
        out_shape=[jax.ShapeDtypeStruct((B, S, HEAD_DIM), F32)] * 2,
        compiler_params=_cparams(("parallel",)),
        name="rope_tables",
    )(pos3, inv)


def _attn_kernel(x_ref, cos_ref, sin_ref, w_ref, b_ref, o_ref, q_s, k_s, v_s, acc_s, m_s, l_s, *, S, rc):
    g = pl.program_id(1)
    scale = HEAD_DIM ** -0.5
    slabs = (q_s, k_s, v_s)

    def project(i):
        r0 = pl.multiple_of(i * rc, rc)
        xs = x_ref[pl.ds(r0, rc), :].astype(BF16)
        qkv = jnp.dot(xs, w_ref[...], preferred_element_type=F32) + b_ref[...]
        c = cos_ref[pl.ds(r0, rc), :]
        s = sin_ref[pl.ds(r0, rc), :]
        out = []
        for h in range(ATTN_SLOTS):
            qh = qkv[:, h * HEAD_DIM:(h + 1) * HEAD_DIM]
            out.append((qh * c + pltpu.roll(qh, HEAD_DIM // 2, 1) * s) * scale)
        for h in range(ATTN_SLOTS):
            kh = qkv[:, ATTN_OUT + h * HEAD_DIM:ATTN_OUT + (h + 1) * HEAD_DIM]
            out.append(kh * c + pltpu.roll(kh, HEAD_DIM // 2, 1) * s)
        for h in range(ATTN_SLOTS):
            out.append(qkv[:, 2 * ATTN_OUT + h * HEAD_DIM:2 * ATTN_OUT + (h + 1) * HEAD_DIM])
        return r0, out

    def proj_token_order(i, carry):
        r0, parts = project(i)
        for k, part in enumerate(parts):
            slabs[k // ATTN_SLOTS][k % ATTN_SLOTS, pl.ds(r0, rc), :] = part
        return carry

    def proj_residue_order(r):
        pc = r * BF16_ROWS
        run = pc // r
        t_i = lax.broadcasted_iota(jnp.int32, (pc, pc), 1)
        p_i = lax.broadcasted_iota(jnp.int32, (pc, pc), 0)
        sort = jnp.where(p_i == (t_i % r) * run + t_i // r, 1.0, 0.0).astype(BF16)

        def body(i, carry):
            _, parts = project(i)
            rows = jnp.concatenate(parts, axis=1).astype(BF16)
            for sub in range(rc // pc):
                chunk = i * (rc // pc) + sub
                srt = jnp.dot(sort, rows[sub * pc:(sub + 1) * pc, :], preferred_element_type=F32)
                for res in range(r):
                    dst = pl.ds(pl.multiple_of(res * BAND_BLOCK + chunk * run, run), run)
                    for k in range(3 * ATTN_SLOTS):
                        slabs[k // ATTN_SLOTS][k % ATTN_SLOTS, dst, :] = (
                            srt[res * run:(res + 1) * run, k * HEAD_DIM:(k + 1) * HEAD_DIM])
            return carry

        return body

    qi = lax.broadcasted_iota(jnp.int32, (BAND_BLOCK, 2 * BAND_BLOCK), 0)
    kj = lax.broadcasted_iota(jnp.int32, (BAND_BLOCK, 2 * BAND_BLOCK), 1)
    dist = BAND_BLOCK + qi - kj
    band = (dist >= 0) & (dist <= BAND_BLOCK)
    ones = jnp.ones((2 * BAND_BLOCK, HEAD_DIM), BF16)

    def attend(r):
        span = BAND_BLOCK * r

        def blk(i, carry):
            loaded = []
            for u in range(BLOCKS_PER_ITER):
                c = i * BLOCKS_PER_ITER + u
                res = c % r
                n = c // r
                start = n * span + res
                prev = jnp.maximum(n - 1, 0) * span + res
                valid = band & ((kj >= BAND_BLOCK) | (n > 0))
                cur_rows = pl.ds(start, BAND_BLOCK, stride=r)
                prev_rows = pl.ds(prev, BAND_BLOCK, stride=r)
                for h in range(ATTN_SLOTS):
                    q = q_s[h, cur_rows, :].astype(BF16)
                    kk = jnp.concatenate([k_s[h, prev_rows, :], k_s[h, cur_rows, :]], axis=0).astype(BF16)
                    vv = jnp.concatenate([v_s[h, prev_rows, :], v_s[h, cur_rows, :]], axis=0).astype(BF16)
                    loaded.append((h, cur_rows, valid, q, kk, vv,
                                   acc_s[h, cur_rows, :], m_s[h, cur_rows, :], l_s[h, cur_rows, :]))
            scores = [lax.dot_general(item[3], item[4], (((1,), (1,)), ((), ())), preferred_element_type=F32)
                      for item in loaded]
            probs = []
            for (_, _, valid, _, _, _, _, m_old, _), s in zip(loaded, scores):
                s = jnp.where(valid, s, NEG)
                m_blk = jnp.max(jnp.maximum(s[:, :BAND_BLOCK], s[:, BAND_BLOCK:]), axis=-1, keepdims=True)
                m_new = jnp.maximum(m_old, m_blk)
                probs.append((m_new, jnp.exp(m_old - m_new),
                              jnp.exp((s - jnp.concatenate([m_new, m_new], axis=1)).astype(BF16))))
            updated = []
            for (h, cur_rows, _, _, _, vv, acc_old, _, l_old), (m_new, a, p) in zip(loaded, probs):
                pv = jnp.dot(p, jnp.concatenate([vv, ones], axis=1), preferred_element_type=F32)
                updated.append((h, cur_rows, a * acc_old + pv[:, :HEAD_DIM], m_new,
                                a * l_old + pv[:, HEAD_DIM:]))
            for h, cur_rows, acc_new, m_new, l_new in updated:
                acc_s[h, cur_rows, :] = acc_new
                m_s[h, cur_rows, :] = m_new
                l_s[h, cur_rows, :] = l_new
            return carry

        lax.fori_loop(0, S // rc, proj_token_order, 0)
        lax.fori_loop(0, S // (BAND_BLOCK * BLOCKS_PER_ITER), blk, 0)

    def attend_first(r):
        causal = (lax.broadcasted_iota(jnp.int32, (BAND_BLOCK, BAND_BLOCK), 0)
                  >= lax.broadcasted_iota(jnp.int32, (BAND_BLOCK, BAND_BLOCK), 1))
        ones_blk = jnp.ones((BAND_BLOCK, HEAD_DIM), BF16)

        def blk(i, carry):
            loaded = []
            for u in range(BLOCKS_PER_ITER):
                res = i * BLOCKS_PER_ITER + u
                rows = pl.ds(pl.multiple_of(res * BAND_BLOCK, BAND_BLOCK), BAND_BLOCK)
                tokens = pl.ds(res, BAND_BLOCK, stride=r)
                for h in range(ATTN_SLOTS):
                    loaded.append((h, tokens, q_s[h, rows, :].astype(BF16), k_s[h, rows, :].astype(BF16),
                                   v_s[h, rows, :].astype(BF16)))
            scores = [lax.dot_general(q, kk, (((1,), (1,)), ((), ())), preferred_element_type=F32)
                      for _, _, q, kk, _ in loaded]
            probs = []
            for s in scores:
                s = jnp.where(causal, s, NEG)
                m_new = jnp.broadcast_to(jnp.max(s, axis=-1, keepdims=True), s.shape)
                probs.append((m_new, jnp.exp((s - m_new).astype(BF16))))
            updated = []
            for (h, tokens, _, _, vv), (m_new, p) in zip(loaded, probs):
                pv = jnp.dot(p, jnp.concatenate([vv, ones_blk], axis=1), preferred_element_type=F32)
                updated.append((h, tokens, pv[:, :HEAD_DIM], m_new, pv[:, HEAD_DIM:]))
            for h, tokens, acc_new, m_new, l_new in updated:
                acc_s[h, tokens, :] = acc_new
                m_s[h, tokens, :] = m_new
                l_s[h, tokens, :] = l_new
            return carry

        lax.fori_loop(0, S // rc, proj_residue_order(r), 0)
        lax.fori_loop(0, r // BLOCKS_PER_ITER, blk, 0)

    assert S // ATTN_GROUP_ORDER[0] == BAND_BLOCK
    for gi, dilation in enumerate(ATTN_GROUP_ORDER):
        pl.when(g == gi)(functools.partial(attend_first if gi == 0 else attend, dilation))

    @pl.when(g == N_DIL - 1)
    def _():
        def fin(i, carry):
            r0 = pl.multiple_of(i * rc, rc)
            for h in range(ATTN_SLOTS):
                o = acc_s[h, pl.ds(r0, rc), :] / l_s[h, pl.ds(r0, rc), :]
                o_ref[pl.ds(r0, rc), h * HEAD_DIM:(h + 1) * HEAD_DIM] = o.astype(o_ref.dtype)
            return carry

        lax.fori_loop(0, S // rc, fin, 0)


def _attention(x, cos2, sin2, w, b):
    B, S, D = x.shape
    rc = 512
    gw = 3 * ATTN_OUT
    slab = pltpu.VMEM((ATTN_SLOTS, S, HEAD_DIM), F32)
    kern = functools.partial(_attn_kernel, S=S, rc=rc)
    o = pl.pallas_call(
        kern,
        grid=(B, N_DIL),
        in_specs=[
            pl.BlockSpec((None, S, D), lambda bb, g: (bb, 0, 0), pipeline_mode=pl.Buffered(1)),
            pl.BlockSpec((None, S, HEAD_DIM), lambda bb, g: (bb, 0, 0), pipeline_mode=pl.Buffered(1)),
            pl.BlockSpec((None, S, HEAD_DIM), lambda bb, g: (bb, 0, 0), pipeline_mode=pl.Buffered(1)),
            pl.BlockSpec((D, gw), lambda bb, g: (0, g)),
            pl.BlockSpec((1, gw), lambda bb, g: (0, g)),
        ],
        out_specs=pl.BlockSpec((None, S, ATTN_OUT), lambda bb, g: (bb, 0, 0)),
        out_shape=jax.ShapeDtypeStruct((B, S, ATTN_OUT), BF16),
        scratch_shapes=[slab] * 6,
        compiler_params=_cparams(("parallel", "arbitrary")),
        name="dilated_attn",
    )(x, cos2, sin2, w, b)
    return o.reshape(B * S, ATTN_OUT)


def _conv_kernel(x_ref, w_ref, b_ref, cw_ref, cb_ref, g_ref, beta_ref, out_ref, cpad, shift_s, *, S, rp, rc):
    cpad[0:CONV_PAD, :] = jnp.zeros((CONV_PAD, CONV_CH), F32)

    def glu(i, carry):
        r0 = pl.multiple_of(i * rp, rp)
        xs = x_ref[pl.ds(r0, rp), :].astype(BF16)
        u = jnp.dot(xs, w_ref[...], preferred_element_type=F32) + b_ref[...]
        cpad[pl.ds(CONV_PAD + r0, rp), :] = u[:, :CONV_CH] * jax.nn.sigmoid(u[:, CONV_CH:])
        return carry

    lax.fori_loop(0, S // rp, glu, 0)

    off = CONV_PAD - (CONV_WIDTH - 1)
    srows = shift_s.shape[1]

    def conv(i, carry):
        r0 = pl.multiple_of(i * rc, rc)
        win = cpad[pl.ds(r0, rc + CONV_PAD), :]
        for ph in range(1, SUBLANES):
            shift_s[ph - 1, :, :] = win[ph:ph + srows, :]
        acc = jnp.zeros((rc, CONV_CH), F32) + cb_ref[...]
        for j in range(CONV_WIDTH):
            a, ph = divmod(j + off, SUBLANES)
            if ph == 0:
                tap = win[a * SUBLANES:a * SUBLANES + rc, :]
            else:
                tap = shift_s[ph - 1, a * SUBLANES:a * SUBLANES + rc, :]
            acc = acc + tap * cw_ref[j:j + 1, :]
        mu = jnp.mean(acc, axis=-1, keepdims=True)
        d = acc - mu
        var = jnp.mean(d * d, axis=-1, keepdims=True)
        y = d * lax.rsqrt(var + LN_EPS) * g_ref[...] + beta_ref[...]
        out_ref[pl.ds(r0, rc), :] = (y * jax.nn.sigmoid(y)).astype(out_ref.dtype)
        return carry

    lax.fori_loop(0, S // rc, conv, 0)


def _conv_branch(x, w_glu, b_glu, conv_w, conv_b, ln_g, ln_b):
    B, S, D = x.shape
    rp, rc = 512, 256
    cw = jnp.pad(conv_w, ((0, CONV_PAD - CONV_WIDTH), (0, 0)))
    kern = functools.partial(_conv_kernel, S=S, rp=rp, rc=rc)
    out = pl.pallas_call(
        kern,
        grid=(B,),
        in_specs=[
            pl.BlockSpec((None, S, D), lambda b: (b, 0, 0)),
            _full((D, 2 * CONV_CH)), _full((1, 2 * CONV_CH)),
            _full((CONV_PAD, CONV_CH)), _full((1, CONV_CH)), _full((1, CONV_CH)), _full((1, CONV_CH)),
        ],
        out_specs=pl.BlockSpec((None, S, CONV_CH), lambda b: (b, 0, 0)),
        out_shape=jax.ShapeDtypeStruct((B, S, CONV_CH), BF16),
        scratch_shapes=[pltpu.VMEM((CONV_PAD + S, CONV_CH), F32),
                        pltpu.VMEM((SUBLANES - 1, rc + CONV_PAD - SUBLANES, CONV_CH), F32)],
        compiler_params=_cparams(("parallel",)),
        name="conformer_conv",
    )(x, w_glu, b_glu, cw, conv_b, ln_g, ln_b)
    return out.reshape(B * S, CONV_CH)


def _mem_kernel(x_ref, mem_ref, wq_ref, bq_ref, wkv_ref, out_ref, *, S, rc):
    kv = jnp.dot(mem_ref[...].astype(BF16), wkv_ref[...], preferred_element_type=F32)
    km = kv[:, :MEM_W].astype(BF16)
    vm = kv[:, MEM_W:].astype(BF16)
    scale = MEM_HEAD_DIM ** -0.5

    def body(i, carry):
        r0 = pl.multiple_of(i * rc, rc)
        xs = x_ref[pl.ds(r0, rc), :].astype(BF16)
        q = (jnp.dot(xs, wq_ref[...], preferred_element_type=F32) + bq_ref[...]) * scale
        q = q.astype(BF16)
        for h in range(MEM_HEADS):
            lo = h * MEM_HEAD_DIM
            s = lax.dot_general(q[:, lo:lo + MEM_HEAD_DIM], km[:, lo:lo + MEM_HEAD_DIM],
                                (((1,), (1,)), ((), ())), preferred_element_type=F32)
            m = jnp.max(s, axis=-1, keepdims=True)
            p = jnp.exp(s - m)
            l = jnp.sum(p, axis=-1, keepdims=True)
            o = jnp.dot(p.astype(BF16), vm[:, lo:lo + MEM_HEAD_DIM], preferred_element_type=F32) / l
            out_ref[pl.ds(r0, rc), lo:lo + MEM_HEAD_DIM] = o.astype(out_ref.dtype)
        return carry

    lax.fori_loop(0, S // rc, body, 0)


def _mem_branch(x, mem, w_q, b_q, w_kv):
    B, S, D = x.shape
    M = mem.shape[1]
    rc = 512
    kern = functools.partial(_mem_kernel, S=S, rc=rc)
    out = pl.pallas_call(
        kern,
        grid=(B,),
        in_specs=[
            pl.BlockSpec((None, S, D), lambda b: (b, 0, 0)),
            pl.BlockSpec((None, M, D), lambda b: (b, 0, 0)),
            _full((D, MEM_W)), _full((1, MEM_W)), _full((D, 2 * MEM_W)),
        ],
        out_specs=pl.BlockSpec((None, S, MEM_W), lambda b: (b, 0, 0)),
        out_shape=jax.ShapeDtypeStruct((B, S, MEM_W), BF16),
        compiler_params=_cparams(("parallel",)),
        name="memory_attn",
    )(x, mem, w_q, b_q, w_kv)
    return out.reshape(B * S, MEM_W)


def _layer_norm(z, g, b):
    mu = jnp.mean(z, axis=-1, keepdims=True)
    d = z - mu
    var = jnp.mean(d * d, axis=-1, keepdims=True)
    return d * lax.rsqrt(var + LN_EPS) * g + b


def _merge_kernel(x_ref, a_ref, c_ref, m_ref,
                  wg_ref, bg_ref, wa_ref, wc_ref, wm_ref, wo_ref, g1_ref, b1_ref, wr_ref, br_ref,
                  x1_ref, x1b_ref, rinfo_ref, cnt_ref, *, alpha):
    x = x_ref[...]
    xb = x.astype(BF16)

    def gate(k):
        lo = k * D_MODEL
        return jax.nn.sigmoid(jnp.dot(xb, wg_ref[:, lo:lo + D_MODEL], preferred_element_type=F32)
                              + bg_ref[:, lo:lo + D_MODEL])

    merged = gate(0) * jnp.dot(a_ref[...], wa_ref[...], preferred_element_type=F32)
    merged = merged + gate(1) * jnp.dot(c_ref[...], wc_ref[...], preferred_element_type=F32)
    merged = merged + gate(2) * jnp.dot(m_ref[...], wm_ref[...], preferred_element_type=F32)
    mix = jnp.dot(merged.astype(BF16), wo_ref[...], preferred_element_type=F32)
    x1 = _layer_norm(alpha * x + mix, g1_ref[...], b1_ref[...])
    x1_ref[...] = x1
    x1b_ref[...] = x1.astype(BF16)

    x1_hi = x1.astype(BF16)
    x1_lo = (x1 - x1_hi.astype(F32)).astype(BF16)
    t = jnp.dot(x1_hi, wr_ref[...], preferred_element_type=F32)
    logits = (t[:, :LANES] + t[:, LANES:]
              + jnp.dot(x1_lo, wr_ref[:, :LANES], preferred_element_type=F32) + br_ref[...])
    lane = lax.broadcasted_iota(jnp.int32, logits.shape, 1)
    is_g = lane < N_GROUPS
    gl = jnp.where(is_g, logits, NEG)
    gmax = jnp.max(gl, axis=-1, keepdims=True)
    g_sel = jnp.min(jnp.where(is_g & (gl == gmax), lane, LANES), axis=-1, keepdims=True)
    g_prob = 1.0 / jnp.sum(jnp.where(is_g, jnp.exp(gl - gmax), 0.0), axis=-1, keepdims=True)
    e_lo = N_GROUPS + g_sel * EXPERTS_PER_GROUP
    in_grp = (lane >= e_lo) & (lane < e_lo + EXPERTS_PER_GROUP)
    el = jnp.where(in_grp, logits, NEG)
    v1 = jnp.max(el, axis=-1, keepdims=True)
    i1 = jnp.min(jnp.where(in_grp & (el == v1), lane, LANES), axis=-1, keepdims=True)
    rest = in_grp & (lane != i1)
    el2 = jnp.where(rest, logits, NEG)
    v2 = jnp.max(el2, axis=-1, keepdims=True)
    i2 = jnp.min(jnp.where(rest & (el2 == v2), lane, LANES), axis=-1, keepdims=True)
    t = jnp.exp(v2 - v1)
    w1 = g_prob / (1.0 + t)
    w2 = g_prob * t / (1.0 + t)
    e1 = (i1 - N_GROUPS).astype(F32)
    e2 = (i2 - N_GROUPS).astype(F32)
    rinfo_ref[...] = jnp.where(lane == 0, e1, jnp.where(lane == 1, e2, jnp.where(
        lane == 2, w1, jnp.where(lane == 3, w2, 0.0))))
    sel = lane + N_GROUPS
    hot = jnp.where((sel == i1) | (sel == i2), 1.0, 0.0)
    cnt_ref[...] = jnp.broadcast_to(jnp.sum(hot, axis=0, keepdims=True), cnt_ref.shape)


def _merge(x2, a, c, om, wg, bg, wa, wc, wm, wo, g1, b1, wr, br, alpha, tm=MERGE_TM):
    T, D = x2.shape
    row = lambda w: pl.BlockSpec((tm, w), lambda i: (i, 0))
    kern = functools.partial(_merge_kernel, alpha=alpha)
    return pl.pallas_call(
        kern,
        grid=(T // tm,),
        in_specs=[row(D), row(ATTN_OUT), row(CONV_CH), row(MEM_W),
                  _full((D, N_BRANCH * D)), _full((1, N_BRANCH * D)),
                  _full((ATTN_OUT, D)), _full((CONV_CH, D)), _full((MEM_W, D)), _full((D, D)),
                  _full((1, D)), _full((1, D)), _full((D, 2 * LANES)), _full((1, LANES))],
        out_specs=[row(D), row(D), row(LANES), pl.BlockSpec((None, SUBLANES, LANES), lambda i: (i, 0, 0))],
        out_shape=[jax.ShapeDtypeStruct((T, D), F32), jax.ShapeDtypeStruct((T, D), BF16),
                   jax.ShapeDtypeStruct((T, LANES), F32),
                   jax.ShapeDtypeStruct((T // tm, SUBLANES, LANES), F32)],
        compiler_params=_cparams(("parallel",)),
        name="merge_ln1_router",
    )(x2, a, c, om, wg, bg, wa, wc, wm, wo, g1, b1, wr, br)


SEG_BITS = tuple(range((SORT_TM // BF16_ROWS).bit_length() - 1, -1, -1))
GAP_BITS = tuple(range((EXPERT_TM // BF16_ROWS - 1).bit_length() - 1, -1, -1))


def _segment_copies(units, bits, make_copy, act):
    for b in bits:
        v = 1 << b

        @pl.when((units & v) != 0)
        def _():
            start = pl.multiple_of((units & (-2 * v)) * BF16_ROWS, BF16_ROWS)
            act(make_copy(start, v * BF16_ROWS))


def _sort_positions(rinfo, lo_vec, tri_ref):
    lane = lax.broadcasted_iota(jnp.int32, rinfo.shape, 1)
    e1 = rinfo[:, 0:1].astype(jnp.int32)
    e2 = rinfo[:, 1:2].astype(jnp.int32)
    hot = jnp.where((lane == e1) | (lane == e2), 1.0, 0.0).astype(BF16)
    before = jnp.dot(tri_ref[...], hot, preferred_element_type=F32) + lo_vec
    lp1 = jnp.sum(jnp.where(lane == e1, before, 0.0), axis=-1, keepdims=True)
    lp2 = jnp.sum(jnp.where(lane == e2, before, 0.0), axis=-1, keepdims=True)
    return lp1, lp2


def _dispatch_kernel(np_ref, lo_ref, g_ref, gap_ref, gapdst_ref,
                     xb_ref, rinfo_ref, lov_ref, tri_ref, xs_ref, pos_ref, sorted_s, zero_s, sem):
    i = pl.program_id(0)
    rinfo = rinfo_ref[...]
    lp1, lp2 = _sort_positions(rinfo, lov_ref[...], tri_ref)
    lane = lax.broadcasted_iota(jnp.int32, rinfo.shape, 1)
    pos = jnp.where(lane == 0, lp1, jnp.where(lane == 1, lp2, 0.0))
    pos_ref[...] = pos
    pos_t = pos.T
    p_iota = lax.broadcasted_iota(jnp.int32, (SORT_ROWS, SORT_TM), 0)
    perm = (p_iota == pos_t[0:1, :].astype(jnp.int32)) | (p_iota == pos_t[1:2, :].astype(jnp.int32))
    perm = jnp.where(perm, 1.0, 0.0).astype(BF16)
    sorted_s[...] = jnp.dot(perm, xb_ref[...], preferred_element_type=F32).astype(BF16)

    def seg(e):
        src0 = pl.multiple_of(lo_ref[i * N_EXPERTS + e], BF16_ROWS)
        dst0 = pl.multiple_of(g_ref[i * N_EXPERTS + e], BF16_ROWS)
        return lambda start, rows: pltpu.make_async_copy(
            sorted_s.at[pl.ds(src0 + start, rows), :], xs_ref.at[pl.ds(dst0 + start, rows), :], sem)

    def gap(e):
        dst0 = pl.multiple_of(gapdst_ref[e], BF16_ROWS)
        return lambda start, rows: pltpu.make_async_copy(
            zero_s.at[pl.ds(0, rows), :], xs_ref.at[pl.ds(dst0 + start, rows), :], sem)

    zrows = zero_s.shape[0]

    def tail(r):
        return pltpu.make_async_copy(zero_s, xs_ref.at[pl.ds(pl.multiple_of(r * zrows, zrows), zrows), :], sem)

    @pl.when(i == 0)
    def _():
        zero_s[...] = jnp.zeros_like(zero_s)
        for e in range(N_EXPERTS):
            _segment_copies(gap_ref[e], GAP_BITS, gap(e), lambda cp: cp.start())
        tail_lo = gap_ref[N_EXPERTS] * (EXPERT_TM // zrows)
        tail_hi = xs_ref.shape[0] // zrows
        lax.fori_loop(tail_lo, tail_hi, lambda r, c: (tail(r).start(), c)[1], 0)
        for e in range(N_EXPERTS):
            _segment_copies(gap_ref[e], GAP_BITS, gap(e), lambda cp: cp.wait())
        lax.fori_loop(tail_lo, tail_hi, lambda r, c: (tail(r).wait(), c)[1], 0)

    for e in range(N_EXPERTS):
        _segment_copies(np_ref[i * N_EXPERTS + e], SEG_BITS, seg(e), lambda cp: cp.start())
    for e in range(N_EXPERTS):
        _segment_copies(np_ref[i * N_EXPERTS + e], SEG_BITS, seg(e), lambda cp: cp.wait())


def _dispatch(x1b, rinfo, np_t, lo_t, g_t, gap_t, gapdst_t, lo_vec, n_rows):
    T, D = x1b.shape
    n_tiles = T // SORT_TM
    tri = (jnp.arange(SORT_TM)[:, None] > jnp.arange(SORT_TM)[None, :]).astype(BF16)
    gap_rows = BF16_ROWS << GAP_BITS[0]
    return pl.pallas_call(
        _dispatch_kernel,
        grid_spec=pltpu.PrefetchScalarGridSpec(
            num_scalar_prefetch=5,
            grid=(n_tiles,),
            in_specs=[
                pl.BlockSpec((SORT_TM, D), lambda i, *_: (i, 0)),
                pl.BlockSpec((SORT_TM, LANES), lambda i, *_: (i, 0)),
                pl.BlockSpec((None, 1, LANES), lambda i, *_: (i, 0, 0)),
                pl.BlockSpec((SORT_TM, SORT_TM), lambda i, *_: (0, 0)),
            ],
            out_specs=[pl.BlockSpec(memory_space=pl.ANY),
                       pl.BlockSpec((SORT_TM, LANES), lambda i, *_: (i, 0))],
            scratch_shapes=[pltpu.VMEM((SORT_ROWS, D), BF16), pltpu.VMEM((gap_rows, D), BF16),
                            pltpu.SemaphoreType.DMA(())],
        ),
        out_shape=[jax.ShapeDtypeStruct((n_rows, D), BF16), jax.ShapeDtypeStruct((T, LANES), F32)],
        compiler_params=_cparams(("arbitrary",)),
        name="moe_dispatch",
    )(np_t, lo_t, g_t, gap_t, gapdst_t, x1b, rinfo, lo_vec, tri)


def _expert_kernel(blk_e_ref, used_ref, xs_ref, wg_ref, wu_ref, wd_ref, ys_ref):
    j = pl.program_id(0)

    @pl.when(j < used_ref[0])
    def _():
        xb = xs_ref[...]
        hg = jnp.dot(xb, wg_ref[...], preferred_element_type=F32)
        hu = jnp.dot(xb, wu_ref[...], preferred_element_type=F32)
        hid = hg * jax.nn.sigmoid(hg) * hu
        ys_ref[...] = jnp.dot(hid.astype(BF16), wd_ref[...], preferred_element_type=F32).astype(ys_ref.dtype)

    @pl.when(j >= used_ref[0])
    def _():
        ys_ref[...] = jnp.zeros_like(ys_ref)


def _experts(xs, blk_e, used, wg, wu, wd):
    n_rows, D = xs.shape
    return pl.pallas_call(
        _expert_kernel,
        grid_spec=pltpu.PrefetchScalarGridSpec(
            num_scalar_prefetch=2,
            grid=(n_rows // EXPERT_TM,),
            in_specs=[
                pl.BlockSpec((EXPERT_TM, D), lambda j, be, u: (j, 0)),
                pl.BlockSpec((None, D, EXPERT_FF), lambda j, be, u: (be[j], 0, 0)),
                pl.BlockSpec((None, D, EXPERT_FF), lambda j, be, u: (be[j], 0, 0)),
                pl.BlockSpec((None, EXPERT_FF, D), lambda j, be, u: (be[j], 0, 0)),
            ],
            out_specs=pl.BlockSpec((EXPERT_TM, D), lambda j, be, u: (j, 0)),
        ),
        out_shape=jax.ShapeDtypeStruct((n_rows, D), BF16),
        compiler_params=_cparams(("arbitrary",)),
        name="moe_experts",
    )(blk_e, used, xs, wg, wu, wd)


def _combine_kernel(np_ref, lo_ref, g_ref, ys_ref, rinfo_ref, pos_ref, x1_ref, g2_ref, b2_ref,
                    out_ref, sorted_s, sem, *, alpha):
    i = pl.program_id(0)

    @pl.when(i == 0)
    def _():
        sorted_s[...] = jnp.zeros_like(sorted_s)

    def seg(e):
        dst0 = pl.multiple_of(lo_ref[i * N_EXPERTS + e], BF16_ROWS)
        src0 = pl.multiple_of(g_ref[i * N_EXPERTS + e], BF16_ROWS)
        return lambda start, rows: pltpu.make_async_copy(
            ys_ref.at[pl.ds(src0 + start, rows), :], sorted_s.at[pl.ds(dst0 + start, rows), :], sem)

    for e in range(N_EXPERTS):
        _segment_copies(np_ref[i * N_EXPERTS + e], SEG_BITS, seg(e), lambda cp: cp.start())
    rinfo = rinfo_ref[...]
    pos = pos_ref[...]
    p_iota = lax.broadcasted_iota(jnp.int32, (SORT_TM, SORT_ROWS), 1)
    sel1 = jnp.where(p_iota == pos[:, 0:1].astype(jnp.int32), 1.0, 0.0).astype(BF16)
    sel2 = jnp.where(p_iota == pos[:, 1:2].astype(jnp.int32), 1.0, 0.0).astype(BF16)
    for e in range(N_EXPERTS):
        _segment_copies(np_ref[i * N_EXPERTS + e], SEG_BITS, seg(e), lambda cp: cp.wait())
    ys = sorted_s[...]
    y = (rinfo[:, 2:3] * jnp.dot(sel1, ys, preferred_element_type=F32)
         + rinfo[:, 3:4] * jnp.dot(sel2, ys, preferred_element_type=F32))
    out_ref[...] = _layer_norm(alpha * x1_ref[...] + y, g2_ref[...], b2_ref[...])


def _combine(ys, rinfo, pos, x1, np_t, lo_t, g_t, g2, b2, alpha):
    T, D = x1.shape
    kern = functools.partial(_combine_kernel, alpha=alpha)
    return pl.pallas_call(
        kern,
        grid_spec=pltpu.PrefetchScalarGridSpec(
            num_scalar_prefetch=3,
            grid=(T // SORT_TM,),
            in_specs=[
                pl.BlockSpec(memory_space=pl.ANY),
                pl.BlockSpec((SORT_TM, LANES), lambda i, *_: (i, 0)),
                pl.BlockSpec((SORT_TM, LANES), lambda i, *_: (i, 0)),
                pl.BlockSpec((SORT_TM, D), lambda i, *_: (i, 0)),
                pl.BlockSpec((1, D), lambda i, *_: (0, 0)),
                pl.BlockSpec((1, D), lambda i, *_: (0, 0)),
            ],
            out_specs=pl.BlockSpec((SORT_TM, D), lambda i, *_: (i, 0)),
            scratch_shapes=[pltpu.VMEM((SORT_ROWS, D), BF16), pltpu.SemaphoreType.DMA(())],
        ),
        out_shape=jax.ShapeDtypeStruct((T, D), F32),
        compiler_params=_cparams(("arbitrary",)),
        name="moe_combine_ln2",
    )(np_t, lo_t, g_t, ys, rinfo, pos, x1, g2, b2)


def _ceil_to(a, m):
    return (a + m - 1) // m * m


def _moe(x1b, x1, rinfo, cnt, wg, wu, wd, g2, b2, alpha):
    T, D = x1.shape
    n_tiles = T // SORT_TM
    i32 = jnp.int32
    n = cnt[:, 0, :N_EXPERTS].astype(i32).reshape(n_tiles, SORT_TM // MERGE_TM, N_EXPERTS).sum(axis=1)
    npad = _ceil_to(n, BF16_ROWS)
    lo = jnp.cumsum(npad, axis=1) - npad
    tot = npad.sum(axis=0)
    reg = _ceil_to(tot, EXPERT_TM)
    off = jnp.cumsum(reg) - reg
    g = off[None, :] + jnp.cumsum(npad, axis=0) - npad
    n_blocks = T * 2 // EXPERT_TM + n_tiles * N_EXPERTS * (BF16_ROWS - 1) // EXPERT_TM + N_EXPERTS
    n_rows = n_blocks * EXPERT_TM
    blk_end = jnp.cumsum(reg) // EXPERT_TM
    j = jnp.arange(n_blocks, dtype=i32)
    used = blk_end[-1]
    jj = jnp.minimum(j, used - 1)
    blk_e = jnp.sum(jj[:, None] >= blk_end[None, :], axis=1).astype(i32)
    lo_vec = jnp.pad(lo.astype(F32), ((0, 0), (0, LANES - N_EXPERTS))).reshape(n_tiles, 1, LANES)
    flat = lambda a: a.astype(i32).reshape(-1)
    gap_t = jnp.concatenate([flat((reg - tot) // BF16_ROWS), flat(used)])

    xs, pos = _dispatch(x1b, rinfo, flat(npad // BF16_ROWS), flat(lo), flat(g),
                        gap_t, flat(off + tot), lo_vec, n_rows)
    ys = _experts(xs, blk_e, flat(used), wg, wu, wd)
    return _combine(ys, rinfo, pos, x1, flat(npad // BF16_ROWS), flat(lo), flat(g), g2, b2, alpha)


def _layer(x, mem, cos2, sin2, l, depth, w_in, b_in, conv_w, conv_b, conv_ln_g, conv_ln_b,
           w_mem_kv, w_attn_o, w_conv_o, w_mem_o, w_out, ln1_g, ln1_b,
           w_group_router, b_group_router, w_expert_router, b_expert_router,
           w_exp_gate, w_exp_up, w_exp_down, ln2_g, ln2_b):
    B, S, D = x.shape
    alpha = (2.0 * depth) ** 0.25
    w_in_b = w_in[l].astype(BF16)
    b_in_l = b_in[l].reshape(1, -1)

    cols = [slice(p * ATTN_QK + g * ATTN_OUT, p * ATTN_QK + (g + 1) * ATTN_OUT)
            for g in ATTN_GROUP_IDX for p in range(3)]
    o_attn = _attention(x, cos2, sin2, jnp.concatenate([w_in_b[:, c] for c in cols], axis=1),
                        jnp.concatenate([b_in_l[:, c] for c in cols], axis=1))

    c0 = 3 * ATTN_QK
    c1 = c0 + 2 * CONV_CH
    c2 = c1 + MEM_W
    row = lambda a: a[l].reshape(1, -1)
    c = _conv_branch(x, w_in_b[:, c0:c1], b_in_l[:, c0:c1], conv_w[l], row(conv_b),
                     row(conv_ln_g), row(conv_ln_b))
    om = _mem_branch(x, mem, w_in_b[:, c1:c2], b_in_l[:, c1:c2], w_mem_kv[l].astype(BF16))

    wr = jnp.concatenate([w_group_router[l],
                          w_expert_router[l].transpose(1, 0, 2).reshape(D, N_EXPERTS)], axis=1)
    wr = jnp.pad(wr, ((0, 0), (0, LANES - wr.shape[1])))
    wr_hi = lax.reduce_precision(wr, exponent_bits=8, mantissa_bits=7)
    wr = jnp.concatenate([wr_hi.astype(BF16), (wr - wr_hi).astype(BF16)], axis=1)
    br = jnp.concatenate([b_group_router[l], b_expert_router[l].reshape(-1)])
    br = jnp.pad(br, (0, LANES - br.shape[0])).reshape(1, LANES)

    x1, x1b, rinfo, cnt = _merge(
        x.reshape(B * S, D), o_attn, c, om,
        w_in_b[:, c2:], b_in_l[:, c2:], w_attn_o[l].astype(BF16), w_conv_o[l].astype(BF16),
        w_mem_o[l].astype(BF16), w_out[l].astype(BF16), row(ln1_g), row(ln1_b), wr, br, alpha)

    out = _moe(x1b, x1, rinfo, cnt, w_exp_gate[l].astype(BF16), w_exp_up[l].astype(BF16),
               w_exp_down[l].astype(BF16), row(ln2_g), row(ln2_b), alpha)
    return out.reshape(B, S, D)


def kernel(x, mem, positions, w_in, b_in, conv_w, conv_b, conv_ln_g, conv_ln_b, w_mem_kv, w_attn_o,
           w_conv_o, w_mem_o, w_out, ln1_g, ln1_b, w_group_router, b_group_router, w_expert_router,
           b_expert_router, w_exp_gate, w_exp_up, w_exp_down, ln2_g, ln2_b):
    depth = w_in.shape[0]
    cos2, sin2 = _rope_tables(positions)
    for l in range(depth):
        x = _layer(x, mem, cos2, sin2, l, depth, w_in, b_in, conv_w, conv_b, conv_ln_g, conv_ln_b,
                   w_mem_kv, w_attn_o, w_conv_o, w_mem_o, w_out, ln1_g, ln1_b,
                   w_group_router, b_group_router, w_expert_router, b_expert_router,
                   w_exp_gate, w_exp_up, w_exp_down, ln2_g, ln2_b)
    return x
```

```python
import functools
import math

import jax
import jax.numpy as jnp
from jax import lax
from jax.experimental import pallas as pl
from jax.experimental.pallas import tpu as pltpu

D_MODEL = 1024
HEAD_DIM = 128
ATTN_SLOTS = 4
DIL_PATTERNS = ((128, 1), (512, 4), (2048, 16))
N_DIL = len(DIL_PATTERNS)
ATTN_QK = N_DIL * ATTN_SLOTS * HEAD_DIM
ATTN_OUT = ATTN_SLOTS * HEAD_DIM
BAND_BLOCK = 128
ROPE_THETA = 10000.0
CONV_CH = 512
CONV_WIDTH = 31
MEM_HEADS = 4
MEM_HEAD_DIM = 128
MEM_W = MEM_HEADS * MEM_HEAD_DIM
N_BRANCH = 3
N_GROUPS = 4
EXPERTS_PER_GROUP = 4
N_EXPERTS = N_GROUPS * EXPERTS_PER_GROUP
EXPERT_FF = 512
LN_EPS = 1e-5

LANES = 128
SUBLANES = 8
BF16_ROWS = 16
BLOCKS_PER_ITER = 2
ATTN_GROUP_IDX = (2, 0, 1)
ATTN_GROUP_ORDER = tuple(DIL_PATTERNS[i][1] for i in ATTN_GROUP_IDX)
MERGE_TM = 512
SORT_TM = 512
SORT_ROWS = 2 * SORT_TM + N_EXPERTS * BF16_ROWS
EXPERT_TM = 512
CONV_PAD = 32
VMEM_LIMIT = 56 * 1024 * 1024

F32 = jnp.float32
BF16 = jnp.bfloat16
NEG = -1e30


def _cparams(sem):
    return pltpu.CompilerParams(dimension_semantics=sem, vmem_limit_bytes=VMEM_LIMIT)


def _full(shape):
    n = len(shape)
    return pl.BlockSpec(shape, lambda *_: (0,) * n)


def _rope_kernel(pos_ref, inv_ref, cos_ref, sin_ref):
    ang = pos_ref[...].astype(F32) * inv_ref[...]
    lane = lax.broadcasted_iota(jnp.int32, ang.shape, 1)
    s = jnp.sin(ang)
    cos_ref[...] = jnp.cos(ang)
    sin_ref[...] = jnp.where(lane < HEAD_DIM // 2, -s, s)


def _rope_tables(positions):
    B, S = positions.shape
    half = HEAD_DIM // 2
    inv = ROPE_THETA ** (-jnp.arange(half, dtype=F32) / half)
    inv = jnp.concatenate([inv, inv]).reshape(1, HEAD_DIM)
    pos3 = positions.reshape(B, S, 1)
    return pl.pallas_call(
        _rope_kernel,
        grid=(B,),
        in_specs=[pl.BlockSpec((None, S, 1), lambda b: (b, 0, 0)), _full((1, HEAD_DIM))],
        out_specs=[pl.BlockSpec((None, S, HEAD_DIM), lambda b: (b, 0, 0))] * 2,
        out_shape=[jax.ShapeDtypeStruct((B, S, HEAD_DIM), F32)] * 2,
        compiler_params=_cparams(("parallel",)),
        name="rope_tables",
    )(pos3, inv)


def _attn_kernel(x_ref, cos_ref, sin_ref, w_ref, b_ref, o_ref, q_s, k_s, v_s, acc_s, m_s, l_s, *, S, rc):
    g = pl.program_id(1)
    scale = HEAD_DIM ** -0.5
    slabs = (q_s, k_s, v_s)

    def project(i):
        r0 = pl.multiple_of(i * rc, rc)
        xs = x_ref[pl.ds(r0, rc), :].astype(BF16)
        qkv = jnp.dot(xs, w_ref[...], preferred_element_type=F32) + b_ref[...]
        c = cos_ref[pl.ds(r0, rc), :]
        s = sin_ref[pl.ds(r0, rc), :]
        out = []
        for h in range(ATTN_SLOTS):
            qh = qkv[:, h * HEAD_DIM:(h + 1) * HEAD_DIM]
            out.append((qh * c + pltpu.roll(qh, HEAD_DIM // 2, 1) * s) * scale)
        for h in range(ATTN_SLOTS):
            kh = qkv[:, ATTN_OUT + h * HEAD_DIM:ATTN_OUT + (h + 1) * HEAD_DIM]
            out.append(kh * c + pltpu.roll(kh, HEAD_DIM // 2, 1) * s)
        for h in range(ATTN_SLOTS):
            out.append(qkv[:, 2 * ATTN_OUT + h * HEAD_DIM:2 * ATTN_OUT + (h + 1) * HEAD_DIM])
        return r0, out

    def proj_token_order(i, carry):
        r0, parts = project(i)
        for k, part in enumerate(parts):
            slabs[k // ATTN_SLOTS][k % ATTN_SLOTS, pl.ds(r0, rc), :] = part
        return carry

    def proj_residue_order(r):
        pc = r * BF16_ROWS
        run = pc // r
        t_i = lax.broadcasted_iota(jnp.int32, (pc, pc), 1)
        p_i = lax.broadcasted_iota(jnp.int32, (pc, pc), 0)
        sort = jnp.where(p_i == (t_i % r) * run + t_i // r, 1.0, 0.0).astype(BF16)

        def body(i, carry):
            _, parts = project(i)
            rows = jnp.concatenate(parts, axis=1).astype(BF16)
            for sub in range(rc // pc):
                chunk = i * (rc // pc) + sub
                srt = jnp.dot(sort, rows[sub * pc:(sub + 1) * pc, :], preferred_element_type=F32)
                for res in range(r):
                    dst = pl.ds(pl.multiple_of(res * BAND_BLOCK + chunk * run, run), run)
                    for k in range(3 * ATTN_SLOTS):
                        slabs[k // ATTN_SLOTS][k % ATTN_SLOTS, dst, :] = (
                            srt[res * run:(res + 1) * run, k * HEAD_DIM:(k + 1) * HEAD_DIM])
            return carry

        return body

    qi = lax.broadcasted_iota(jnp.int32, (BAND_BLOCK, 2 * BAND_BLOCK), 0)
    kj = lax.broadcasted_iota(jnp.int32, (BAND_BLOCK, 2 * BAND_BLOCK), 1)
    dist = BAND_BLOCK + qi - kj
    band = (dist >= 0) & (dist <= BAND_BLOCK)
    ones = jnp.ones((2 * BAND_BLOCK, HEAD_DIM), BF16)

    def attend(r):
        span = BAND_BLOCK * r

        def blk(i, carry):
            loaded = []
            for u in range(BLOCKS_PER_ITER):
                c = i * BLOCKS_PER_ITER + u
                res = c % r
                n = c // r
                start = n * span + res
                prev = jnp.maximum(n - 1, 0) * span + res
                valid = band & ((kj >= BAND_BLOCK) | (n > 0))
                cur_rows = pl.ds(start, BAND_BLOCK, stride=r)
                prev_rows = pl.ds(prev, BAND_BLOCK, stride=r)
                for h in range(ATTN_SLOTS):
                    q = q_s[h, cur_rows, :].astype(BF16)
                    kk = jnp.concatenate([k_s[h, prev_rows, :], k_s[h, cur_rows, :]], axis=0).astype(BF16)
                    vv = jnp.concatenate([v_s[h, prev_rows, :], v_s[h, cur_rows, :]], axis=0).astype(BF16)
                    loaded.append((h, cur_rows, valid, q, kk, vv,
                                   acc_s[h, cur_rows, :], m_s[h, cur_rows, :], l_s[h, cur_rows, :]))
            scores = [lax.dot_general(item[3], item[4], (((1,), (1,)), ((), ())), preferred_element_type=F32)
                      for item in loaded]
            probs = []
            for (_, _, valid, _, _, _, _, m_old, _), s in zip(loaded, scores):
                s = jnp.where(valid, s, NEG)
                m_blk = jnp.max(jnp.maximum(s[:, :BAND_BLOCK], s[:, BAND_BLOCK:]), axis=-1, keepdims=True)
                m_new = jnp.maximum(m_old, m_blk)
                probs.append((m_new, jnp.exp(m_old - m_new),
                              jnp.exp((s - jnp.concatenate([m_new, m_new], axis=1)).astype(BF16))))
            updated = []
            for (h, cur_rows, _, _, _, vv, acc_old, _, l_old), (m_new, a, p) in zip(loaded, probs):
                pv = jnp.dot(p, jnp.concatenate([vv, ones], axis=1), preferred_element_type=F32)
                updated.append((h, cur_rows, a * acc_old + pv[:, :HEAD_DIM], m_new,
                                a * l_old + pv[:, HEAD_DIM:]))
            for h, cur_rows, acc_new, m_new, l_new in updated:
                acc_s[h, cur_rows, :] = acc_new
                m_s[h, cur_rows, :] = m_new
                l_s[h, cur_rows, :] = l_new
            return carry

        lax.fori_loop(0, S // rc, proj_token_order, 0)
        lax.fori_loop(0, S // (BAND_BLOCK * BLOCKS_PER_ITER), blk, 0)

    def attend_first(r):
        causal = (lax.broadcasted_iota(jnp.int32, (BAND_BLOCK, BAND_BLOCK), 0)
                  >= lax.broadcasted_iota(jnp.int32, (BAND_BLOCK, BAND_BLOCK), 1))
        ones_blk = jnp.ones((BAND_BLOCK, HEAD_DIM), BF16)

        def blk(i, carry):
            loaded = []
            for u in range(BLOCKS_PER_ITER):
                res = i * BLOCKS_PER_ITER + u
                rows = pl.ds(pl.multiple_of(res * BAND_BLOCK, BAND_BLOCK), BAND_BLOCK)
                tokens = pl.ds(res, BAND_BLOCK, stride=r)
                for h in range(ATTN_SLOTS):
                    loaded.append((h, tokens, q_s[h, rows, :].astype(BF16), k_s[h, rows, :].astype(BF16),
                                   v_s[h, rows, :].astype(BF16)))
            scores = [lax.dot_general(q, kk, (((1,), (1,)), ((), ())), preferred_element_type=F32)
                      for _, _, q, kk, _ in loaded]
            probs = []
            for s in scores:
                s = jnp.where(causal, s, NEG)
                m_new = jnp.broadcast_to(jnp.max(s, axis=-1, keepdims=True), s.shape)
                probs.append((m_new, jnp.exp((s - m_new).astype(BF16))))
            updated = []
            for (h, tokens, _, _, vv), (m_new, p) in zip(loaded, probs):
                pv = jnp.dot(p, jnp.concatenate([vv, ones_blk], axis=1), preferred_element_type=F32)
                updated.append((h, tokens, pv[:, :HEAD_DIM], m_new, pv[:, HEAD_DIM:]))
            for h, tokens, acc_new, m_new, l_new in updated:
                acc_s[h, tokens, :] = acc_new
                m_s[h, tokens, :] = m_new
                l_s[h, tokens, :] = l_new
            return carry

        lax.fori_loop(0, S // rc, proj_residue_order(r), 0)
        lax.fori_loop(0, r // BLOCKS_PER_ITER, blk, 0)

    assert S // ATTN_GROUP_ORDER[0] == BAND_BLOCK
    for gi, dilation in enumerate(ATTN_GROUP_ORDER):
        pl.when(g == gi)(functools.partial(attend_first if gi == 0 else attend, dilation))

    @pl.when(g == N_DIL - 1)
    def _():
        def fin(i, carry):
            r0 = pl.multiple_of(i * rc, rc)
            for h in range(ATTN_SLOTS):
                o = acc_s[h, pl.ds(r0, rc), :] / l_s[h, pl.ds(r0, rc), :]
                o_ref[pl.ds(r0, rc), h * HEAD_DIM:(h + 1) * HEAD_DIM] = o.astype(o_ref.dtype)
            return carry

        lax.fori_loop(0, S // rc, fin, 0)


def _attention(x, cos2, sin2, w, b):
    B, S, D = x.shape
    rc = 512
    gw = 3 * ATTN_OUT
    slab = pltpu.VMEM((ATTN_SLOTS, S, HEAD_DIM), F32)
    kern = functools.partial(_attn_kernel, S=S, rc=rc)
    o = pl.pallas_call(
        kern,
        grid=(B, N_DIL),
        in_specs=[
            pl.BlockSpec((None, S, D), lambda bb, g: (bb, 0, 0), pipeline_mode=pl.Buffered(1)),
            pl.BlockSpec((None, S, HEAD_DIM), lambda bb, g: (bb, 0, 0), pipeline_mode=pl.Buffered(1)),
            pl.BlockSpec((None, S, HEAD_DIM), lambda bb, g: (bb, 0, 0), pipeline_mode=pl.Buffered(1)),
            pl.BlockSpec((D, gw), lambda bb, g: (0, g)),
            pl.BlockSpec((1, gw), lambda bb, g: (0, g)),
        ],
        out_specs=pl.BlockSpec((None, S, ATTN_OUT), lambda bb, g: (bb, 0, 0)),
        out_shape=jax.ShapeDtypeStruct((B, S, ATTN_OUT), BF16),
        scratch_shapes=[slab] * 6,
        compiler_params=_cparams(("parallel", "arbitrary")),
        name="dilated_attn",
    )(x, cos2, sin2, w, b)
    return o.reshape(B * S, ATTN_OUT)


def _conv_kernel(x_ref, w_ref, b_ref, cw_ref, cb_ref, g_ref, beta_ref, out_ref, cpad, shift_s, *, S, rp, rc):
    cpad[0:CONV_PAD, :] = jnp.zeros((CONV_PAD, CONV_CH), F32)

    def glu(i, carry):
        r0 = pl.multiple_of(i * rp, rp)
        xs = x_ref[pl.ds(r0, rp), :].astype(BF16)
        u = jnp.dot(xs, w_ref[...], preferred_element_type=F32) + b_ref[...]
        cpad[pl.ds(CONV_PAD + r0, rp), :] = u[:, :CONV_CH] * jax.nn.sigmoid(u[:, CONV_CH:])
        return carry

    lax.fori_loop(0, S // rp, glu, 0)

    off = CONV_PAD - (CONV_WIDTH - 1)
    srows = shift_s.shape[1]

    def conv(i, carry):
        r0 = pl.multiple_of(i * rc, rc)
        win = cpad[pl.ds(r0, rc + CONV_PAD), :]
        for ph in range(1, SUBLANES):
            shift_s[ph - 1, :, :] = win[ph:ph + srows, :]
        acc = jnp.zeros((rc, CONV_CH), F32) + cb_ref[...]
        for j in range(CONV_WIDTH):
            a, ph = divmod(j + off, SUBLANES)
            if ph == 0:
                tap = win[a * SUBLANES:a * SUBLANES + rc, :]
            else:
                tap = shift_s[ph - 1, a * SUBLANES:a * SUBLANES + rc, :]
            acc = acc + tap * cw_ref[j:j + 1, :]
        mu = jnp.mean(acc, axis=-1, keepdims=True)
        d = acc - mu
        var = jnp.mean(d * d, axis=-1, keepdims=True)
        y = d * lax.rsqrt(var + LN_EPS) * g_ref[...] + beta_ref[...]
        out_ref[pl.ds(r0, rc), :] = (y * jax.nn.sigmoid(y)).astype(out_ref.dtype)
        return carry

    lax.fori_loop(0, S // rc, conv, 0)


def _conv_branch(x, w_glu, b_glu, conv_w, conv_b, ln_g, ln_b):
    B, S, D = x.shape
    rp, rc = 512, 256
    cw = jnp.pad(conv_w, ((0, CONV_PAD - CONV_WIDTH), (0, 0)))
    kern = functools.partial(_conv_kernel, S=S, rp=rp, rc=rc)
    out = pl.pallas_call(
        kern,
        grid=(B,),
        in_specs=[
            pl.BlockSpec((None, S, D), lambda b: (b, 0, 0)),
            _full((D, 2 * CONV_CH)), _full((1, 2 * CONV_CH)),
            _full((CONV_PAD, CONV_CH)), _full((1, CONV_CH)), _full((1, CONV_CH)), _full((1, CONV_CH)),
        ],
        out_specs=pl.BlockSpec((None, S, CONV_CH), lambda b: (b, 0, 0)),
        out_shape=jax.ShapeDtypeStruct((B, S, CONV_CH), BF16),
        scratch_shapes=[pltpu.VMEM((CONV_PAD + S, CONV_CH), F32),
                        pltpu.VMEM((SUBLANES - 1, rc + CONV_PAD - SUBLANES, CONV_CH), F32)],
        compiler_params=_cparams(("parallel",)),
        name="conformer_conv",
    )(x, w_glu, b_glu, cw, conv_b, ln_g, ln_b)
    return out.reshape(B * S, CONV_CH)


def _mem_kernel(x_ref, mem_ref, wq_ref, bq_ref, wkv_ref, out_ref, *, S, rc):
    kv = jnp.dot(mem_ref[...].astype(BF16), wkv_ref[...], preferred_element_type=F32)
    km = kv[:, :MEM_W].astype(BF16)
    vm = kv[:, MEM_W:].astype(BF16)
    scale = MEM_HEAD_DIM ** -0.5

    def body(i, carry):
        r0 = pl.multiple_of(i * rc, rc)
        xs = x_ref[pl.ds(r0, rc), :].astype(BF16)
        q = (jnp.dot(xs, wq_ref[...], preferred_element_type=F32) + bq_ref[...]) * scale
        q = q.astype(BF16)
        for h in range(MEM_HEADS):
            lo = h * MEM_HEAD_DIM
            s = lax.dot_general(q[:, lo:lo + MEM_HEAD_DIM], km[:, lo:lo + MEM_HEAD_DIM],
                                (((1,), (1,)), ((), ())), preferred_element_type=F32)
            m = jnp.max(s, axis=-1, keepdims=True)
            p = jnp.exp(s - m)
            l = jnp.sum(p, axis=-1, keepdims=True)
            o = jnp.dot(p.astype(BF16), vm[:, lo:lo + MEM_HEAD_DIM], preferred_element_type=F32) / l
            out_ref[pl.ds(r0, rc), lo:lo + MEM_HEAD_DIM] = o.astype(out_ref.dtype)
        return carry

    lax.fori_loop(0, S // rc, body, 0)


def _mem_branch(x, mem, w_q, b_q, w_kv):
    B, S, D = x.shape
    M = mem.shape[1]
    rc = 512
    kern = functools.partial(_mem_kernel, S=S, rc=rc)
    out = pl.pallas_call(
        kern,
        grid=(B,),
        in_specs=[
            pl.BlockSpec((None, S, D), lambda b: (b, 0, 0)),
            pl.BlockSpec((None, M, D), lambda b: (b, 0, 0)),
            _full((D, MEM_W)), _full((1, MEM_W)), _full((D, 2 * MEM_W)),
        ],
        out_specs=pl.BlockSpec((None, S, MEM_W), lambda b: (b, 0, 0)),
        out_shape=jax.ShapeDtypeStruct((B, S, MEM_W), BF16),
        compiler_params=_cparams(("parallel",)),
        name="memory_attn",
    )(x, mem, w_q, b_q, w_kv)
    return out.reshape(B * S, MEM_W)


def _layer_norm(z, g, b):
    mu = jnp.mean(z, axis=-1, keepdims=True)
    d = z - mu
    var = jnp.mean(d * d, axis=-1, keepdims=True)
    return d * lax.rsqrt(var + LN_EPS) * g + b


def _merge_kernel(x_ref, a_ref, c_ref, m_ref,
                  wg_ref, bg_ref, wa_ref, wc_ref, wm_ref, wo_ref, g1_ref, b1_ref, wr_ref, br_ref,
                  x1_ref, x1b_ref, rinfo_ref, cnt_ref, *, alpha):
    x = x_ref[...]
    xb = x.astype(BF16)

    def gate(k):
        lo = k * D_MODEL
        return jax.nn.sigmoid(jnp.dot(xb, wg_ref[:, lo:lo + D_MODEL], preferred_element_type=F32)
                              + bg_ref[:, lo:lo + D_MODEL])

    merged = gate(0) * jnp.dot(a_ref[...], wa_ref[...], preferred_element_type=F32)
    merged = merged + gate(1) * jnp.dot(c_ref[...], wc_ref[...], preferred_element_type=F32)
    merged = merged + gate(2) * jnp.dot(m_ref[...], wm_ref[...], preferred_element_type=F32)
    mix = jnp.dot(merged.astype(BF16), wo_ref[...], preferred_element_type=F32)
    x1 = _layer_norm(alpha * x + mix, g1_ref[...], b1_ref[...])
    x1_ref[...] = x1
    x1b_ref[...] = x1.astype(BF16)

    x1_hi = x1.astype(BF16)
    x1_lo = (x1 - x1_hi.astype(F32)).astype(BF16)
    wr = wr_ref[...]
    wr_hi = wr.astype(BF16)
    wr_lo = (wr - wr_hi.astype(F32)).astype(BF16)
    t = jnp.dot(x1_hi, jnp.concatenate([wr_hi, wr_lo], axis=1), preferred_element_type=F32)
    logits = (t[:, :LANES] + t[:, LANES:]
              + jnp.dot(x1_lo, wr_hi, preferred_element_type=F32) + br_ref[...])
    lane = lax.broadcasted_iota(jnp.int32, logits.shape, 1)
    is_g = lane < N_GROUPS
    gl = jnp.where(is_g, logits, NEG)
    gmax = jnp.max(gl, axis=-1, keepdims=True)
    g_sel = jnp.min(jnp.where(is_g & (gl == gmax), lane, LANES), axis=-1, keepdims=True)
    g_prob = 1.0 / jnp.sum(jnp.where(is_g, jnp.exp(gl - gmax), 0.0), axis=-1, keepdims=True)
    e_lo = N_GROUPS + g_sel * EXPERTS_PER_GROUP
    in_grp = (lane >= e_lo) & (lane < e_lo + EXPERTS_PER_GROUP)
    el = jnp.where(in_grp, logits, NEG)
    v1 = jnp.max(el, axis=-1, keepdims=True)
    i1 = jnp.min(jnp.where(in_grp & (el == v1), lane, LANES), axis=-1, keepdims=True)
    rest = in_grp & (lane != i1)
    el2 = jnp.where(rest, logits, NEG)
    v2 = jnp.max(el2, axis=-1, keepdims=True)
    i2 = jnp.min(jnp.where(rest & (el2 == v2), lane, LANES), axis=-1, keepdims=True)
    t = jnp.exp(v2 - v1)
    w1 = g_prob / (1.0 + t)
    w2 = g_prob * t / (1.0 + t)
    e1 = (i1 - N_GROUPS).astype(F32)
    e2 = (i2 - N_GROUPS).astype(F32)
    rinfo_ref[...] = jnp.where(lane == 0, e1, jnp.where(lane == 1, e2, jnp.where(
        lane == 2, w1, jnp.where(lane == 3, w2, 0.0))))
    sel = lane + N_GROUPS
    hot = jnp.where((sel == i1) | (sel == i2), 1.0, 0.0)
    cnt_ref[...] = jnp.broadcast_to(jnp.sum(hot, axis=0, keepdims=True), cnt_ref.shape)


def _merge(x2, a, c, om, wg, bg, wa, wc, wm, wo, g1, b1, wr, br, alpha, tm=MERGE_TM):
    T, D = x2.shape
    row = lambda w: pl.BlockSpec((tm, w), lambda i: (i, 0))
    kern = functools.partial(_merge_kernel, alpha=alpha)
    return pl.pallas_call(
        kern,
        grid=(T // tm,),
        in_specs=[row(D), row(ATTN_OUT), row(CONV_CH), row(MEM_W),
                  _full((D, N_BRANCH * D)), _full((1, N_BRANCH * D)),
                  _full((ATTN_OUT, D)), _full((CONV_CH, D)), _full((MEM_W, D)), _full((D, D)),
                  _full((1, D)), _full((1, D)), _full((D, LANES)), _full((1, LANES))],
        out_specs=[row(D), row(D), row(LANES), pl.BlockSpec((None, SUBLANES, LANES), lambda i: (i, 0, 0))],
        out_shape=[jax.ShapeDtypeStruct((T, D), F32), jax.ShapeDtypeStruct((T, D), BF16),
                   jax.ShapeDtypeStruct((T, LANES), F32),
                   jax.ShapeDtypeStruct((T // tm, SUBLANES, LANES), F32)],
        compiler_params=_cparams(("parallel",)),
        name="merge_ln1_router",
    )(x2, a, c, om, wg, bg, wa, wc, wm, wo, g1, b1, wr, br)


SORT_UNITS = SORT_ROWS // BF16_ROWS
GAP_BITS = tuple(range((EXPERT_TM // BF16_ROWS - 1).bit_length() - 1, -1, -1))


def _segment_copies(units, bits, make_copy, act):
    for b in bits:
        v = 1 << b

        @pl.when((units & v) != 0)
        def _():
            start = pl.multiple_of((units & (-2 * v)) * BF16_ROWS, BF16_ROWS)
            act(make_copy(start, v * BF16_ROWS))


def _sort_positions(rinfo, lo_vec, tri_ref):
    lane = lax.broadcasted_iota(jnp.int32, rinfo.shape, 1)
    e1 = rinfo[:, 0:1].astype(jnp.int32)
    e2 = rinfo[:, 1:2].astype(jnp.int32)
    hot = jnp.where((lane == e1) | (lane == e2), 1.0, 0.0).astype(BF16)
    before = jnp.dot(tri_ref[...], hot, preferred_element_type=F32) + lo_vec
    lp1 = jnp.sum(jnp.where(lane == e1, before, 0.0), axis=-1, keepdims=True)
    lp2 = jnp.sum(jnp.where(lane == e2, before, 0.0), axis=-1, keepdims=True)
    return lp1, lp2


def _for_each_unit(n_units, fn):
    lax.fori_loop(0, n_units, lambda u, c: (fn(u), c)[1], 0)


def _dispatch_kernel(nu_ref, dst_ref, gap_ref, gapdst_ref,
                     xb_ref, rinfo_ref, lov_ref, tri_ref, xs_ref, pos_ref, sorted_s, zero_s, sem):
    i = pl.program_id(0)
    last = pl.num_programs(0) - 1
    slot = i % 2
    zsem = sem.at[2]

    def unit_copy(tile, buf):
        def make(u):
            dst = pl.multiple_of(dst_ref[tile * SORT_UNITS + u], BF16_ROWS)
            return pltpu.make_async_copy(
                sorted_s.at[buf, pl.ds(pl.multiple_of(u * BF16_ROWS, BF16_ROWS), BF16_ROWS), :],
                xs_ref.at[pl.ds(dst, BF16_ROWS), :], sem.at[buf])
        return make

    def drain(tile, buf):
        _for_each_unit(nu_ref[tile], lambda u: unit_copy(tile, buf)(u).wait())

    def gap(e):
        dst0 = pl.multiple_of(gapdst_ref[e], BF16_ROWS)
        return lambda start, rows: pltpu.make_async_copy(
            zero_s.at[pl.ds(0, rows), :], xs_ref.at[pl.ds(dst0 + start, rows), :], zsem)

    zrows = zero_s.shape[0]

    def tail(r):
        return pltpu.make_async_copy(zero_s, xs_ref.at[pl.ds(pl.multiple_of(r * zrows, zrows), zrows), :], zsem)

    @pl.when(i == 0)
    def _():
        zero_s[...] = jnp.zeros_like(zero_s)
        for e in range(N_EXPERTS):
            _segment_copies(gap_ref[e], GAP_BITS, gap(e), lambda cp: cp.start())
        tail_lo = gap_ref[N_EXPERTS] * (EXPERT_TM // zrows)
        tail_hi = xs_ref.shape[0] // zrows
        lax.fori_loop(tail_lo, tail_hi, lambda r, c: (tail(r).start(), c)[1], 0)
        for e in range(N_EXPERTS):
            _segment_copies(gap_ref[e], GAP_BITS, gap(e), lambda cp: cp.wait())
        lax.fori_loop(tail_lo, tail_hi, lambda r, c: (tail(r).wait(), c)[1], 0)

    rinfo = rinfo_ref[...]
    lp1, lp2 = _sort_positions(rinfo, lov_ref[...], tri_ref)
    lane = lax.broadcasted_iota(jnp.int32, rinfo.shape, 1)
    pos = jnp.where(lane == 0, lp1, jnp.where(lane == 1, lp2, 0.0))
    pos_ref[...] = pos
    pos_t = pos.T
    p_iota = lax.broadcasted_iota(jnp.int32, (SORT_ROWS, SORT_TM), 0)
    perm = (p_iota == pos_t[0:1, :].astype(jnp.int32)) | (p_iota == pos_t[1:2, :].astype(jnp.int32))
    perm = jnp.where(perm, 1.0, 0.0).astype(BF16)
    srt = jnp.dot(perm, xb_ref[...], preferred_element_type=F32).astype(BF16)

    @pl.when(i >= 2)
    def _():
        drain(i - 2, slot)

    sorted_s[slot] = srt
    _for_each_unit(nu_ref[i], lambda u: unit_copy(i, slot)(u).start())

    @pl.when(i == last)
    def _():
        @pl.when(i >= 1)
        def _():
            drain(i - 1, 1 - slot)
        drain(i, slot)


def _dispatch(x1b, rinfo, nu_t, dst_t, gap_t, gapdst_t, lo_vec, n_rows):
    T, D = x1b.shape
    n_tiles = T // SORT_TM
    tri = (jnp.arange(SORT_TM)[:, None] > jnp.arange(SORT_TM)[None, :]).astype(BF16)
    gap_rows = BF16_ROWS << GAP_BITS[0]
    return pl.pallas_call(
        _dispatch_kernel,
        grid_spec=pltpu.PrefetchScalarGridSpec(
            num_scalar_prefetch=4,
            grid=(n_tiles,),
            in_specs=[
                pl.BlockSpec((SORT_TM, D), lambda i, *_: (i, 0)),
                pl.BlockSpec((SORT_TM, LANES), lambda i, *_: (i, 0)),
                pl.BlockSpec((None, 1, LANES), lambda i, *_: (i, 0, 0)),
                pl.BlockSpec((SORT_TM, SORT_TM), lambda i, *_: (0, 0)),
            ],
            out_specs=[pl.BlockSpec(memory_space=pl.ANY),
                       pl.BlockSpec((SORT_TM, LANES), lambda i, *_: (i, 0))],
            scratch_shapes=[pltpu.VMEM((2, SORT_ROWS, D), BF16), pltpu.VMEM((gap_rows, D), BF16),
                            pltpu.SemaphoreType.DMA((3,))],
        ),
        out_shape=[jax.ShapeDtypeStruct((n_rows, D), BF16), jax.ShapeDtypeStruct((T, LANES), F32)],
        compiler_params=_cparams(("arbitrary",)),
        name="moe_dispatch",
    )(nu_t, dst_t, gap_t, gapdst_t, x1b, rinfo, lo_vec, tri)


def _expert_kernel(blk_e_ref, used_ref, xs_ref, wg_ref, wu_ref, wd_ref, ys_ref):
    j = pl.program_id(0)

    @pl.when(j < used_ref[0])
    def _():
        xb = xs_ref[...]
        hg = jnp.dot(xb, wg_ref[...], preferred_element_type=F32)
        hu = jnp.dot(xb, wu_ref[...], preferred_element_type=F32)
        hid = hg * jax.nn.sigmoid(hg) * hu
        ys_ref[...] = jnp.dot(hid.astype(BF16), wd_ref[...], preferred_element_type=F32).astype(ys_ref.dtype)

    @pl.when(j >= used_ref[0])
    def _():
        ys_ref[...] = jnp.zeros_like(ys_ref)


def _experts(xs, blk_e, used, wg, wu, wd):
    n_rows, D = xs.shape
    return pl.pallas_call(
        _expert_kernel,
        grid_spec=pltpu.PrefetchScalarGridSpec(
            num_scalar_prefetch=2,
            grid=(n_rows // EXPERT_TM,),
            in_specs=[
                pl.BlockSpec((EXPERT_TM, D), lambda j, be, u: (j, 0)),
                pl.BlockSpec((None, D, EXPERT_FF), lambda j, be, u: (be[j], 0, 0)),
                pl.BlockSpec((None, D, EXPERT_FF), lambda j, be, u: (be[j], 0, 0)),
                pl.BlockSpec((None, EXPERT_FF, D), lambda j, be, u: (be[j], 0, 0)),
            ],
            out_specs=pl.BlockSpec((EXPERT_TM, D), lambda j, be, u: (j, 0)),
        ),
        out_shape=jax.ShapeDtypeStruct((n_rows, D), BF16),
        compiler_params=_cparams(("arbitrary",)),
        name="moe_experts",
    )(blk_e, used, xs, wg, wu, wd)


def _combine_kernel(nu_ref, dst_ref, ys_ref, rinfo_ref, pos_ref, x1_ref, g2_ref, b2_ref,
                    out_ref, sorted_s, sem, *, alpha):
    i = pl.program_id(0)
    last = pl.num_programs(0) - 1
    slot = i % 2

    def unit_copy(tile, buf):
        def make(u):
            src = pl.multiple_of(dst_ref[tile * SORT_UNITS + u], BF16_ROWS)
            return pltpu.make_async_copy(
                ys_ref.at[pl.ds(src, BF16_ROWS), :],
                sorted_s.at[buf, pl.ds(pl.multiple_of(u * BF16_ROWS, BF16_ROWS), BF16_ROWS), :], sem.at[buf])
        return make

    def fetch(tile, buf):
        _for_each_unit(nu_ref[tile], lambda u: unit_copy(tile, buf)(u).start())

    @pl.when(i == 0)
    def _():
        sorted_s[...] = jnp.zeros_like(sorted_s)
        fetch(0, 0)

    @pl.when(i < last)
    def _():
        fetch(i + 1, 1 - slot)

    rinfo = rinfo_ref[...]
    pos = pos_ref[...]
    p_iota = lax.broadcasted_iota(jnp.int32, (SORT_TM, SORT_ROWS), 1)
    sel1 = jnp.where(p_iota == pos[:, 0:1].astype(jnp.int32), 1.0, 0.0).astype(BF16)
    sel2 = jnp.where(p_iota == pos[:, 1:2].astype(jnp.int32), 1.0, 0.0).astype(BF16)
    _for_each_unit(nu_ref[i], lambda u: unit_copy(i, slot)(u).wait())
    ys = sorted_s[slot]
    y = (rinfo[:, 2:3] * jnp.dot(sel1, ys, preferred_element_type=F32)
         + rinfo[:, 3:4] * jnp.dot(sel2, ys, preferred_element_type=F32))
    out_ref[...] = _layer_norm(alpha * x1_ref[...] + y, g2_ref[...], b2_ref[...])


def _combine(ys, rinfo, pos, x1, nu_t, dst_t, g2, b2, alpha):
    T, D = x1.shape
    kern = functools.partial(_combine_kernel, alpha=alpha)
    return pl.pallas_call(
        kern,
        grid_spec=pltpu.PrefetchScalarGridSpec(
            num_scalar_prefetch=2,
            grid=(T // SORT_TM,),
            in_specs=[
                pl.BlockSpec(memory_space=pl.ANY),
                pl.BlockSpec((SORT_TM, LANES), lambda i, *_: (i, 0)),
                pl.BlockSpec((SORT_TM, LANES), lambda i, *_: (i, 0)),
                pl.BlockSpec((SORT_TM, D), lambda i, *_: (i, 0)),
                pl.BlockSpec((1, D), lambda i, *_: (0, 0)),
                pl.BlockSpec((1, D), lambda i, *_: (0, 0)),
            ],
            out_specs=pl.BlockSpec((SORT_TM, D), lambda i, *_: (i, 0)),
            scratch_shapes=[pltpu.VMEM((2, SORT_ROWS, D), BF16), pltpu.SemaphoreType.DMA((2,))],
        ),
        out_shape=jax.ShapeDtypeStruct((T, D), F32),
        compiler_params=_cparams(("arbitrary",)),
        name="moe_combine_ln2",
    )(nu_t, dst_t, ys, rinfo, pos, x1, g2, b2)


def _ceil_to(a, m):
    return (a + m - 1) // m * m


def _moe(x1b, x1, rinfo, cnt, wg, wu, wd, g2, b2, alpha):
    T, D = x1.shape
    n_tiles = T // SORT_TM
    i32 = jnp.int32
    n = cnt[:, 0, :N_EXPERTS].astype(i32).reshape(n_tiles, SORT_TM // MERGE_TM, N_EXPERTS).sum(axis=1)
    npad = _ceil_to(n, BF16_ROWS)
    lo = jnp.cumsum(npad, axis=1) - npad
    tot = npad.sum(axis=0)
    reg = _ceil_to(tot, EXPERT_TM)
    off = jnp.cumsum(reg) - reg
    g = off[None, :] + jnp.cumsum(npad, axis=0) - npad
    n_blocks = T * 2 // EXPERT_TM + n_tiles * N_EXPERTS * (BF16_ROWS - 1) // EXPERT_TM + N_EXPERTS
    n_rows = n_blocks * EXPERT_TM
    blk_end = jnp.cumsum(reg) // EXPERT_TM
    j = jnp.arange(n_blocks, dtype=i32)
    used = blk_end[-1]
    jj = jnp.minimum(j, used - 1)
    blk_e = jnp.sum(jj[:, None] >= blk_end[None, :], axis=1).astype(i32)
    lo_vec = jnp.pad(lo.astype(F32), ((0, 0), (0, LANES - N_EXPERTS))).reshape(n_tiles, 1, LANES)
    flat = lambda a: a.astype(i32).reshape(-1)
    gap_t = jnp.concatenate([flat((reg - tot) // BF16_ROWS), flat(used)])
    u_row = jnp.arange(SORT_UNITS, dtype=i32) * BF16_ROWS
    u_e = jnp.sum(u_row[None, :, None] >= (lo + npad)[:, None, :], axis=-1)
    u_e = jnp.minimum(u_e, N_EXPERTS - 1)
    u_dst = jnp.take_along_axis(g, u_e, axis=1) + u_row[None, :] - jnp.take_along_axis(lo, u_e, axis=1)
    nu_t = flat(npad.sum(axis=1) // BF16_ROWS)

    xs, pos = _dispatch(x1b, rinfo, nu_t, flat(u_dst), gap_t, flat(off + tot), lo_vec, n_rows)
    ys = _experts(xs, blk_e, flat(used), wg, wu, wd)
    return _combine(ys, rinfo, pos, x1, nu_t, flat(u_dst), g2, b2, alpha)


def _layer(x, mem, cos2, sin2, l, depth, w_in, b_in, conv_w, conv_b, conv_ln_g, conv_ln_b,
           w_mem_kv, w_attn_o, w_conv_o, w_mem_o, w_out, ln1_g, ln1_b,
           w_group_router, b_group_router, w_expert_router, b_expert_router,
           w_exp_gate, w_exp_up, w_exp_down, ln2_g, ln2_b):
    B, S, D = x.shape
    alpha = (2.0 * depth) ** 0.25
    w_in_b = w_in[l].astype(BF16)
    b_in_l = b_in[l].reshape(1, -1)

    cols = [slice(p * ATTN_QK + g * ATTN_OUT, p * ATTN_QK + (g + 1) * ATTN_OUT)
            for g in ATTN_GROUP_IDX for p in range(3)]
    o_attn = _attention(x, cos2, sin2, jnp.concatenate([w_in_b[:, c] for c in cols], axis=1),
                        jnp.concatenate([b_in_l[:, c] for c in cols], axis=1))

    c0 = 3 * ATTN_QK
    c1 = c0 + 2 * CONV_CH
    c2 = c1 + MEM_W
    row = lambda a: a[l].reshape(1, -1)
    c = _conv_branch(x, w_in_b[:, c0:c1], b_in_l[:, c0:c1], conv_w[l], row(conv_b),
                     row(conv_ln_g), row(conv_ln_b))
    om = _mem_branch(x, mem, w_in_b[:, c1:c2], b_in_l[:, c1:c2], w_mem_kv[l].astype(BF16))

    wr = jnp.concatenate([w_group_router[l],
                          w_expert_router[l].transpose(1, 0, 2).reshape(D, N_EXPERTS)], axis=1)
    wr = jnp.pad(wr, ((0, 0), (0, LANES - wr.shape[1])))
    br = jnp.concatenate([b_group_router[l], b_expert_router[l].reshape(-1)])
    br = jnp.pad(br, (0, LANES - br.shape[0])).reshape(1, LANES)

    x1, x1b, rinfo, cnt = _merge(
        x.reshape(B * S, D), o_attn, c, om,
        w_in_b[:, c2:], b_in_l[:, c2:], w_attn_o[l].astype(BF16), w_conv_o[l].astype(BF16),
        w_mem_o[l].astype(BF16), w_out[l].astype(BF16), row(ln1_g), row(ln1_b), wr, br, alpha)

    out = _moe(x1b, x1, rinfo, cnt, w_exp_gate[l].astype(BF16), w_exp_up[l].astype(BF16),
               w_exp_down[l].astype(BF16), row(ln2_g), row(ln2_b), alpha)
    return out.reshape(B, S, D)


def kernel(x, mem, positions, w_in, b_in, conv_w, conv_b, conv_ln_g, conv_ln_b, w_mem_kv, w_attn_o,
           w_conv_o, w_mem_o, w_out, ln1_g, ln1_b, w_group_router, b_group_router, w_expert_router,
           b_expert_router, w_exp_gate, w_exp_up, w_exp_down, ln2_g, ln2_b):
    depth = w_in.shape[0]
    cos2, sin2 = _rope_tables(positions)
    for l in range(depth):
        x = _layer(x, mem, cos2, sin2, l, depth, w_in, b_in, conv_w, conv_b, conv_ln_g, conv_ln_b,
                   w_mem_kv, w_attn_o, w_conv_o, w_mem_o, w_out, ln1_g, ln1_b,
                   w_group_router, b_group_router, w_expert_router, b_expert_router,
                   w_exp_gate, w_exp_up, w_exp_down, ln2_g, ln2_b)
    return x
```

```python
import functools
import math

import jax
import jax.numpy as jnp
from jax import lax
from jax.experimental import pallas as pl
from jax.experimental.pallas import tpu as pltpu

D_MODEL = 1024
HEAD_DIM = 128
ATTN_SLOTS = 4
DIL_PATTERNS = ((128, 1), (512, 4), (2048, 16))
N_DIL = len(DIL_PATTERNS)
ATTN_QK = N_DIL * ATTN_SLOTS * HEAD_DIM
ATTN_OUT = ATTN_SLOTS * HEAD_DIM
BAND_BLOCK = 128
ROPE_THETA = 10000.0
CONV_CH = 512
CONV_WIDTH = 31
MEM_HEADS = 4
MEM_HEAD_DIM = 128
MEM_W = MEM_HEADS * MEM_HEAD_DIM
N_BRANCH = 3
N_GROUPS = 4
EXPERTS_PER_GROUP = 4
N_EXPERTS = N_GROUPS * EXPERTS_PER_GROUP
EXPERT_FF = 512
LN_EPS = 1e-5

LANES = 128
SUBLANES = 8
BF16_ROWS = 16
BLOCKS_PER_ITER = 2
ATTN_GROUP_IDX = (2, 0, 1)
ATTN_GROUP_ORDER = tuple(DIL_PATTERNS[i][1] for i in ATTN_GROUP_IDX)
MERGE_TM = 512
SORT_TM = 512
SORT_ROWS = 2 * SORT_TM + N_EXPERTS * BF16_ROWS
EXPERT_TM = 512
CONV_PAD = 32
VMEM_LIMIT = 56 * 1024 * 1024

F32 = jnp.float32
BF16 = jnp.bfloat16
NEG = -1e30


def _cparams(sem):
    return pltpu.CompilerParams(dimension_semantics=sem, vmem_limit_bytes=VMEM_LIMIT)


def _full(shape):
    n = len(shape)
    return pl.BlockSpec(shape, lambda *_: (0,) * n)


def _rope_kernel(pos_ref, inv_ref, cos_ref, sin_ref):
    ang = pos_ref[...].astype(F32) * inv_ref[...]
    lane = lax.broadcasted_iota(jnp.int32, ang.shape, 1)
    s = jnp.sin(ang)
    cos_ref[...] = jnp.cos(ang)
    sin_ref[...] = jnp.where(lane < HEAD_DIM // 2, -s, s)


def _rope_tables(positions):
    B, S = positions.shape
    half = HEAD_DIM // 2
    inv = ROPE_THETA ** (-jnp.arange(half, dtype=F32) / half)
    inv = jnp.concatenate([inv, inv]).reshape(1, HEAD_DIM)
    pos3 = positions.reshape(B, S, 1)
    return pl.pallas_call(
        _rope_kernel,
        grid=(B,),
        in_specs=[pl.BlockSpec((None, S, 1), lambda b: (b, 0, 0)), _full((1, HEAD_DIM))],
        out_specs=[pl.BlockSpec((None, S, HEAD_DIM), lambda b: (b, 0, 0))] * 2,
        out_shape=[jax.ShapeDtypeStruct((B, S, HEAD_DIM), F32)] * 2,
        compiler_params=_cparams(("parallel",)),
        name="rope_tables",
    )(pos3, inv)


def _attn_kernel(x_ref, cos_ref, sin_ref, w_ref, b_ref, o_ref, q_s, k_s, v_s, acc_s, m_s, l_s, *, S, rc):
    g = pl.program_id(1)
    scale = HEAD_DIM ** -0.5
    slabs = (q_s, k_s, v_s)

    def project(i):
        r0 = pl.multiple_of(i * rc, rc)
        xs = x_ref[pl.ds(r0, rc), :].astype(BF16)
        qkv = jnp.dot(xs, w_ref[...], preferred_element_type=F32) + b_ref[...]
        c = cos_ref[pl.ds(r0, rc), :]
        s = sin_ref[pl.ds(r0, rc), :]
        out = []
        for h in range(ATTN_SLOTS):
            qh = qkv[:, h * HEAD_DIM:(h + 1) * HEAD_DIM]
            out.append((qh * c + pltpu.roll(qh, HEAD_DIM // 2, 1) * s) * scale)
        for h in range(ATTN_SLOTS):
            kh = qkv[:, ATTN_OUT + h * HEAD_DIM:ATTN_OUT + (h + 1) * HEAD_DIM]
            out.append(kh * c + pltpu.roll(kh, HEAD_DIM // 2, 1) * s)
        for h in range(ATTN_SLOTS):
            out.append(qkv[:, 2 * ATTN_OUT + h * HEAD_DIM:2 * ATTN_OUT + (h + 1) * HEAD_DIM])
        return r0, out

    def proj_token_order(i, carry):
        r0, parts = project(i)
        for k, part in enumerate(parts):
            slabs[k // ATTN_SLOTS][k % ATTN_SLOTS, pl.ds(r0, rc), :] = part
        return carry

    def proj_residue_order(r):
        pc = r * BF16_ROWS
        run = pc // r
        t_i = lax.broadcasted_iota(jnp.int32, (pc, pc), 1)
        p_i = lax.broadcasted_iota(jnp.int32, (pc, pc), 0)
        sort = jnp.where(p_i == (t_i % r) * run + t_i // r, 1.0, 0.0).astype(BF16)

        def body(i, carry):
            _, parts = project(i)
            rows = jnp.concatenate(parts, axis=1).astype(BF16)
            for sub in range(rc // pc):
                chunk = i * (rc // pc) + sub
                srt = jnp.dot(sort, rows[sub * pc:(sub + 1) * pc, :], preferred_element_type=F32)
                for res in range(r):
                    dst = pl.ds(pl.multiple_of(res * BAND_BLOCK + chunk * run, run), run)
                    for k in range(3 * ATTN_SLOTS):
                        slabs[k // ATTN_SLOTS][k % ATTN_SLOTS, dst, :] = (
                            srt[res * run:(res + 1) * run, k * HEAD_DIM:(k + 1) * HEAD_DIM])
            return carry

        return body

    qi = lax.broadcasted_iota(jnp.int32, (BAND_BLOCK, 2 * BAND_BLOCK), 0)
    kj = lax.broadcasted_iota(jnp.int32, (BAND_BLOCK, 2 * BAND_BLOCK), 1)
    dist = BAND_BLOCK + qi - kj
    band = (dist >= 0) & (dist <= BAND_BLOCK)
    ones = jnp.ones((2 * BAND_BLOCK, HEAD_DIM), BF16)

    def attend(r):
        span = BAND_BLOCK * r

        def blk(i, carry):
            loaded = []
            for u in range(BLOCKS_PER_ITER):
                c = i * BLOCKS_PER_ITER + u
                res = c % r
                n = c // r
                start = n * span + res
                prev = jnp.maximum(n - 1, 0) * span + res
                valid = band & ((kj >= BAND_BLOCK) | (n > 0))
                cur_rows = pl.ds(start, BAND_BLOCK, stride=r)
                prev_rows = pl.ds(prev, BAND_BLOCK, stride=r)
                for h in range(ATTN_SLOTS):
                    q = q_s[h, cur_rows, :].astype(BF16)
                    kk = jnp.concatenate([k_s[h, prev_rows, :], k_s[h, cur_rows, :]], axis=0).astype(BF16)
                    vv = jnp.concatenate([v_s[h, prev_rows, :], v_s[h, cur_rows, :]], axis=0).astype(BF16)
                    loaded.append((h, cur_rows, valid, q, kk, vv,
                                   acc_s[h, cur_rows, :], m_s[h, cur_rows, :], l_s[h, cur_rows, :]))
            scores = [lax.dot_general(item[3], item[4], (((1,), (1,)), ((), ())), preferred_element_type=F32)
                      for item in loaded]
            probs = []
            for (_, _, valid, _, _, _, _, m_old, _), s in zip(loaded, scores):
                s = jnp.where(valid, s, NEG)
                m_blk = jnp.max(jnp.maximum(s[:, :BAND_BLOCK], s[:, BAND_BLOCK:]), axis=-1, keepdims=True)
                m_new = jnp.maximum(m_old, m_blk)
                probs.append((m_new, jnp.exp(m_old - m_new),
                              jnp.exp((s - jnp.concatenate([m_new, m_new], axis=1)).astype(BF16))))
            updated = []
            for (h, cur_rows, _, _, _, vv, acc_old, _, l_old), (m_new, a, p) in zip(loaded, probs):
                pv = jnp.dot(p, jnp.concatenate([vv, ones], axis=1), preferred_element_type=F32)
                updated.append((h, cur_rows, a * acc_old + pv[:, :HEAD_DIM], m_new,
                                a * l_old + pv[:, HEAD_DIM:]))
            for h, cur_rows, acc_new, m_new, l_new in updated:
                acc_s[h, cur_rows, :] = acc_new
                m_s[h, cur_rows, :] = m_new
                l_s[h, cur_rows, :] = l_new
            return carry

        lax.fori_loop(0, S // rc, proj_token_order, 0)
        lax.fori_loop(0, S // (BAND_BLOCK * BLOCKS_PER_ITER), blk, 0)

    def attend_first(r):
        causal = (lax.broadcasted_iota(jnp.int32, (BAND_BLOCK, BAND_BLOCK), 0)
                  >= lax.broadcasted_iota(jnp.int32, (BAND_BLOCK, BAND_BLOCK), 1))
        ones_blk = jnp.ones((BAND_BLOCK, HEAD_DIM), BF16)

        def blk(i, carry):
            loaded = []
            for u in range(BLOCKS_PER_ITER):
                res = i * BLOCKS_PER_ITER + u
                rows = pl.ds(pl.multiple_of(res * BAND_BLOCK, BAND_BLOCK), BAND_BLOCK)
                tokens = pl.ds(res, BAND_BLOCK, stride=r)
                for h in range(ATTN_SLOTS):
                    loaded.append((h, tokens, q_s[h, rows, :].astype(BF16), k_s[h, rows, :].astype(BF16),
                                   v_s[h, rows, :].astype(BF16)))
            scores = [lax.dot_general(q, kk, (((1,), (1,)), ((), ())), preferred_element_type=F32)
                      for _, _, q, kk, _ in loaded]
            probs = []
            for s in scores:
                s = jnp.where(causal, s, NEG)
                m_new = jnp.broadcast_to(jnp.max(s, axis=-1, keepdims=True), s.shape)
                probs.append((m_new, jnp.exp((s - m_new).astype(BF16))))
            updated = []
            for (h, tokens, _, _, vv), (m_new, p) in zip(loaded, probs):
                pv = jnp.dot(p, jnp.concatenate([vv, ones_blk], axis=1), preferred_element_type=F32)
                updated.append((h, tokens, pv[:, :HEAD_DIM], m_new, pv[:, HEAD_DIM:]))
            for h, tokens, acc_new, m_new, l_new in updated:
                acc_s[h, tokens, :] = acc_new
                m_s[h, tokens, :] = m_new
                l_s[h, tokens, :] = l_new
            return carry

        lax.fori_loop(0, S // rc, proj_residue_order(r), 0)
        lax.fori_loop(0, r // BLOCKS_PER_ITER, blk, 0)

    assert S // ATTN_GROUP_ORDER[0] == BAND_BLOCK
    for gi, dilation in enumerate(ATTN_GROUP_ORDER):
        pl.when(g == gi)(functools.partial(attend_first if gi == 0 else attend, dilation))

    @pl.when(g == N_DIL - 1)
    def _():
        def fin(i, carry):
            r0 = pl.multiple_of(i * rc, rc)
            for h in range(ATTN_SLOTS):
                o = acc_s[h, pl.ds(r0, rc), :] / l_s[h, pl.ds(r0, rc), :]
                o_ref[pl.ds(r0, rc), h * HEAD_DIM:(h + 1) * HEAD_DIM] = o.astype(o_ref.dtype)
            return carry

        lax.fori_loop(0, S // rc, fin, 0)


def _attention(x, cos2, sin2, w, b):
    B, S, D = x.shape
    rc = 512
    gw = 3 * ATTN_OUT
    slab = pltpu.VMEM((ATTN_SLOTS, S, HEAD_DIM), F32)
    kern = functools.partial(_attn_kernel, S=S, rc=rc)
    o = pl.pallas_call(
        kern,
        grid=(B, N_DIL),
        in_specs=[
            pl.BlockSpec((None, S, D), lambda bb, g: (bb, 0, 0), pipeline_mode=pl.Buffered(1)),
            pl.BlockSpec((None, S, HEAD_DIM), lambda bb, g: (bb, 0, 0), pipeline_mode=pl.Buffered(1)),
            pl.BlockSpec((None, S, HEAD_DIM), lambda bb, g: (bb, 0, 0), pipeline_mode=pl.Buffered(1)),
            pl.BlockSpec((D, gw), lambda bb, g: (0, g)),
            pl.BlockSpec((1, gw), lambda bb, g: (0, g)),
        ],
        out_specs=pl.BlockSpec((None, S, ATTN_OUT), lambda bb, g: (bb, 0, 0)),
        out_shape=jax.ShapeDtypeStruct((B, S, ATTN_OUT), BF16),
        scratch_shapes=[slab] * 6,
        compiler_params=_cparams(("parallel", "arbitrary")),
        name="dilated_attn",
    )(x, cos2, sin2, w, b)
    return o.reshape(B * S, ATTN_OUT)


def _conv_mem_kernel(x_ref, mem_ref, w_ref, b_ref, cw_ref, cb_ref, g_ref, beta_ref, wkv_ref,
                     out_ref, om_ref, cpad, shift_s, q_s, *, S, rp, rc):
    cpad[0:CONV_PAD, :] = jnp.zeros((CONV_PAD, CONV_CH), F32)
    kv = jnp.dot(mem_ref[...].astype(BF16), wkv_ref[...], preferred_element_type=F32)
    km = kv[:, :MEM_W].astype(BF16)
    vm = kv[:, MEM_W:].astype(BF16)
    scale = MEM_HEAD_DIM ** -0.5

    def glu(i, carry):
        r0 = pl.multiple_of(i * rp, rp)
        xs = x_ref[pl.ds(r0, rp), :].astype(BF16)
        u = jnp.dot(xs, w_ref[...], preferred_element_type=F32) + b_ref[...]
        cpad[pl.ds(CONV_PAD + r0, rp), :] = u[:, :CONV_CH] * jax.nn.sigmoid(u[:, CONV_CH:2 * CONV_CH])
        q_s[pl.ds(r0, rp), :] = (u[:, 2 * CONV_CH:] * scale).astype(BF16)
        return carry

    lax.fori_loop(0, S // rp, glu, 0)

    off = CONV_PAD - (CONV_WIDTH - 1)
    srows = shift_s.shape[1]

    def conv(i, carry):
        r0 = pl.multiple_of(i * rc, rc)
        win = cpad[pl.ds(r0, rc + CONV_PAD), :]
        for ph in range(1, SUBLANES):
            shift_s[ph - 1, :, :] = win[ph:ph + srows, :]
        acc = jnp.zeros((rc, CONV_CH), F32) + cb_ref[...]
        for j in range(CONV_WIDTH):
            a, ph = divmod(j + off, SUBLANES)
            if ph == 0:
                tap = win[a * SUBLANES:a * SUBLANES + rc, :]
            else:
                tap = shift_s[ph - 1, a * SUBLANES:a * SUBLANES + rc, :]
            acc = acc + tap * cw_ref[j:j + 1, :]
        mu = jnp.mean(acc, axis=-1, keepdims=True)
        d = acc - mu
        var = jnp.mean(d * d, axis=-1, keepdims=True)
        y = d * lax.rsqrt(var + LN_EPS) * g_ref[...] + beta_ref[...]
        out_ref[pl.ds(r0, rc), :] = (y * jax.nn.sigmoid(y)).astype(out_ref.dtype)
        q = q_s[pl.ds(r0, rc), :]
        for h in range(MEM_HEADS):
            lo = h * MEM_HEAD_DIM
            s = lax.dot_general(q[:, lo:lo + MEM_HEAD_DIM], km[:, lo:lo + MEM_HEAD_DIM],
                                (((1,), (1,)), ((), ())), preferred_element_type=F32)
            m = jnp.max(s, axis=-1, keepdims=True)
            p = jnp.exp(s - m)
            l = jnp.sum(p, axis=-1, keepdims=True)
            o = jnp.dot(p.astype(BF16), vm[:, lo:lo + MEM_HEAD_DIM], preferred_element_type=F32) / l
            om_ref[pl.ds(r0, rc), lo:lo + MEM_HEAD_DIM] = o.astype(om_ref.dtype)
        return carry

    lax.fori_loop(0, S // rc, conv, 0)


def _conv_mem_branches(x, mem, w_proj, b_proj, conv_w, conv_b, ln_g, ln_b, w_kv):
    B, S, D = x.shape
    M = mem.shape[1]
    rp, rc = 512, 256
    pw = 2 * CONV_CH + MEM_W
    cw = jnp.pad(conv_w, ((0, CONV_PAD - CONV_WIDTH), (0, 0)))
    kern = functools.partial(_conv_mem_kernel, S=S, rp=rp, rc=rc)
    c, om = pl.pallas_call(
        kern,
        grid=(B,),
        in_specs=[
            pl.BlockSpec((None, S, D), lambda b: (b, 0, 0)),
            pl.BlockSpec((None, M, D), lambda b: (b, 0, 0)),
            _full((D, pw)), _full((1, pw)),
            _full((CONV_PAD, CONV_CH)), _full((1, CONV_CH)), _full((1, CONV_CH)), _full((1, CONV_CH)),
            _full((D, 2 * MEM_W)),
        ],
        out_specs=[pl.BlockSpec((None, S, CONV_CH), lambda b: (b, 0, 0)),
                   pl.BlockSpec((None, S, MEM_W), lambda b: (b, 0, 0))],
        out_shape=[jax.ShapeDtypeStruct((B, S, CONV_CH), BF16), jax.ShapeDtypeStruct((B, S, MEM_W), BF16)],
        scratch_shapes=[pltpu.VMEM((CONV_PAD + S, CONV_CH), F32),
                        pltpu.VMEM((SUBLANES - 1, rc + CONV_PAD - SUBLANES, CONV_CH), F32),
                        pltpu.VMEM((S, MEM_W), BF16)],
        compiler_params=_cparams(("parallel",)),
        name="conv_and_memory_attn",
    )(x, mem, w_proj, b_proj, cw, conv_b, ln_g, ln_b, w_kv)
    return c.reshape(B * S, CONV_CH), om.reshape(B * S, MEM_W)


def _layer_norm(z, g, b):
    mu = jnp.mean(z, axis=-1, keepdims=True)
    d = z - mu
    var = jnp.mean(d * d, axis=-1, keepdims=True)
    return d * lax.rsqrt(var + LN_EPS) * g + b


def _merge_kernel(x_ref, a_ref, c_ref, m_ref,
                  wg_ref, bg_ref, wa_ref, wc_ref, wm_ref, wo_ref, g1_ref, b1_ref, wr_ref, br_ref,
                  x1_ref, x1b_ref, rinfo_ref, cnt_ref, *, alpha):
    x = x_ref[...]
    xb = x.astype(BF16)

    def gate(k):
        lo = k * D_MODEL
        return jax.nn.sigmoid(jnp.dot(xb, wg_ref[:, lo:lo + D_MODEL], preferred_element_type=F32)
                              + bg_ref[:, lo:lo + D_MODEL])

    merged = gate(0) * jnp.dot(a_ref[...], wa_ref[...], preferred_element_type=F32)
    merged = merged + gate(1) * jnp.dot(c_ref[...], wc_ref[...], preferred_element_type=F32)
    merged = merged + gate(2) * jnp.dot(m_ref[...], wm_ref[...], preferred_element_type=F32)
    mix = jnp.dot(merged.astype(BF16), wo_ref[...], preferred_element_type=F32)
    x1 = _layer_norm(alpha * x + mix, g1_ref[...], b1_ref[...])
    x1_ref[...] = x1
    x1b_ref[...] = x1.astype(BF16)

    x1_hi = x1.astype(BF16)
    x1_lo = (x1 - x1_hi.astype(F32)).astype(BF16)
    wr = wr_ref[...]
    wr_hi = wr.astype(BF16)
    wr_lo = (wr - wr_hi.astype(F32)).astype(BF16)
    t = jnp.dot(x1_hi, jnp.concatenate([wr_hi, wr_lo], axis=1), preferred_element_type=F32)
    logits = (t[:, :LANES] + t[:, LANES:]
              + jnp.dot(x1_lo, wr_hi, preferred_element_type=F32) + br_ref[...])
    lane = lax.broadcasted_iota(jnp.int32, logits.shape, 1)
    is_g = lane < N_GROUPS
    gl = jnp.where(is_g, logits, NEG)
    gmax = jnp.max(gl, axis=-1, keepdims=True)
    g_sel = jnp.min(jnp.where(is_g & (gl == gmax), lane, LANES), axis=-1, keepdims=True)
    g_prob = 1.0 / jnp.sum(jnp.where(is_g, jnp.exp(gl - gmax), 0.0), axis=-1, keepdims=True)
    e_lo = N_GROUPS + g_sel * EXPERTS_PER_GROUP
    in_grp = (lane >= e_lo) & (lane < e_lo + EXPERTS_PER_GROUP)
    el = jnp.where(in_grp, logits, NEG)
    v1 = jnp.max(el, axis=-1, keepdims=True)
    i1 = jnp.min(jnp.where(in_grp & (el == v1), lane, LANES), axis=-1, keepdims=True)
    rest = in_grp & (lane != i1)
    el2 = jnp.where(rest, logits, NEG)
    v2 = jnp.max(el2, axis=-1, keepdims=True)
    i2 = jnp.min(jnp.where(rest & (el2 == v2), lane, LANES), axis=-1, keepdims=True)
    t = jnp.exp(v2 - v1)
    w1 = g_prob / (1.0 + t)
    w2 = g_prob * t / (1.0 + t)
    e1 = (i1 - N_GROUPS).astype(F32)
    e2 = (i2 - N_GROUPS).astype(F32)
    rinfo_ref[...] = jnp.where(lane == 0, e1, jnp.where(lane == 1, e2, jnp.where(
        lane == 2, w1, jnp.where(lane == 3, w2, 0.0))))
    sel = lane + N_GROUPS
    hot = jnp.where((sel == i1) | (sel == i2), 1.0, 0.0)
    cnt_ref[...] = jnp.broadcast_to(jnp.sum(hot, axis=0, keepdims=True), cnt_ref.shape)


def _merge(x2, a, c, om, wg, bg, wa, wc, wm, wo, g1, b1, wr, br, alpha, tm=MERGE_TM):
    T, D = x2.shape
    row = lambda w: pl.BlockSpec((tm, w), lambda i: (i, 0))
    kern = functools.partial(_merge_kernel, alpha=alpha)
    return pl.pallas_call(
        kern,
        grid=(T // tm,),
        in_specs=[row(D), row(ATTN_OUT), row(CONV_CH), row(MEM_W),
                  _full((D, N_BRANCH * D)), _full((1, N_BRANCH * D)),
                  _full((ATTN_OUT, D)), _full((CONV_CH, D)), _full((MEM_W, D)), _full((D, D)),
                  _full((1, D)), _full((1, D)), _full((D, LANES)), _full((1, LANES))],
        out_specs=[row(D), row(D), row(LANES), pl.BlockSpec((None, SUBLANES, LANES), lambda i: (i, 0, 0))],
        out_shape=[jax.ShapeDtypeStruct((T, D), F32), jax.ShapeDtypeStruct((T, D), BF16),
                   jax.ShapeDtypeStruct((T, LANES), F32),
                   jax.ShapeDtypeStruct((T // tm, SUBLANES, LANES), F32)],
        compiler_params=_cparams(("parallel",)),
        name="merge_ln1_router",
    )(x2, a, c, om, wg, bg, wa, wc, wm, wo, g1, b1, wr, br)


SORT_UNITS = SORT_ROWS // BF16_ROWS
GAP_BITS = tuple(range((EXPERT_TM // BF16_ROWS - 1).bit_length() - 1, -1, -1))


def _segment_copies(units, bits, make_copy, act):
    for b in bits:
        v = 1 << b

        @pl.when((units & v) != 0)
        def _():
            start = pl.multiple_of((units & (-2 * v)) * BF16_ROWS, BF16_ROWS)
            act(make_copy(start, v * BF16_ROWS))


def _sort_positions(rinfo, lo_vec, tri_ref):
    lane = lax.broadcasted_iota(jnp.int32, rinfo.shape, 1)
    e1 = rinfo[:, 0:1].astype(jnp.int32)
    e2 = rinfo[:, 1:2].astype(jnp.int32)
    hot = jnp.where((lane == e1) | (lane == e2), 1.0, 0.0).astype(BF16)
    before = jnp.dot(tri_ref[...], hot, preferred_element_type=F32) + lo_vec
    lp1 = jnp.sum(jnp.where(lane == e1, before, 0.0), axis=-1, keepdims=True)
    lp2 = jnp.sum(jnp.where(lane == e2, before, 0.0), axis=-1, keepdims=True)
    return lp1, lp2


def _for_each_unit(fn):
    for u in range(SORT_UNITS):
        fn(u)


def _dispatch_kernel(dst_ref, gap_ref, gapdst_ref,
                     xb_ref, rinfo_ref, lov_ref, tri_ref, xs_ref, pos_ref, sorted_s, zero_s, sem):
    i = pl.program_id(0)
    last = pl.num_programs(0) - 1
    slot = i % 2
    zsem = sem.at[2]

    def unit_copy(tile, buf):
        def make(u):
            dst = pl.multiple_of(dst_ref[tile * SORT_UNITS + u], BF16_ROWS)
            return pltpu.make_async_copy(
                sorted_s.at[buf, pl.ds(pl.multiple_of(u * BF16_ROWS, BF16_ROWS), BF16_ROWS), :],
                xs_ref.at[pl.ds(dst, BF16_ROWS), :], sem.at[buf])
        return make

    def drain(tile, buf):
        _for_each_unit(lambda u: unit_copy(tile, buf)(u).wait())

    def gap(e):
        dst0 = pl.multiple_of(gapdst_ref[e], BF16_ROWS)
        return lambda start, rows: pltpu.make_async_copy(
            zero_s.at[pl.ds(0, rows), :], xs_ref.at[pl.ds(dst0 + start, rows), :], zsem)

    zrows = zero_s.shape[0]

    def tail(r):
        return pltpu.make_async_copy(zero_s, xs_ref.at[pl.ds(pl.multiple_of(r * zrows, zrows), zrows), :], zsem)

    @pl.when(i == 0)
    def _():
        zero_s[...] = jnp.zeros_like(zero_s)
        for e in range(N_EXPERTS):
            _segment_copies(gap_ref[e], GAP_BITS, gap(e), lambda cp: cp.start())
        tail_lo = gap_ref[N_EXPERTS] * (EXPERT_TM // zrows)
        tail_hi = xs_ref.shape[0] // zrows
        lax.fori_loop(tail_lo, tail_hi, lambda r, c: (tail(r).start(), c)[1], 0)
        for e in range(N_EXPERTS):
            _segment_copies(gap_ref[e], GAP_BITS, gap(e), lambda cp: cp.wait())
        lax.fori_loop(tail_lo, tail_hi, lambda r, c: (tail(r).wait(), c)[1], 0)

    rinfo = rinfo_ref[...]
    lp1, lp2 = _sort_positions(rinfo, lov_ref[...], tri_ref)
    lane = lax.broadcasted_iota(jnp.int32, rinfo.shape, 1)
    pos = jnp.where(lane == 0, lp1, jnp.where(lane == 1, lp2, 0.0))
    pos_ref[...] = pos
    pos_t = pos.T
    p_iota = lax.broadcasted_iota(jnp.int32, (SORT_ROWS, SORT_TM), 0)
    perm = (p_iota == pos_t[0:1, :].astype(jnp.int32)) | (p_iota == pos_t[1:2, :].astype(jnp.int32))
    perm = jnp.where(perm, 1.0, 0.0).astype(BF16)
    srt = jnp.dot(perm, xb_ref[...], preferred_element_type=F32).astype(BF16)

    @pl.when(i >= 2)
    def _():
        drain(i - 2, slot)

    sorted_s[slot] = srt
    _for_each_unit(lambda u: unit_copy(i, slot)(u).start())

    @pl.when(i == last)
    def _():
        @pl.when(i >= 1)
        def _():
            drain(i - 1, 1 - slot)
        drain(i, slot)


def _dispatch(x1b, rinfo, dst_t, gap_t, gapdst_t, lo_vec, n_rows):
    T, D = x1b.shape
    n_tiles = T // SORT_TM
    tri = (jnp.arange(SORT_TM)[:, None] > jnp.arange(SORT_TM)[None, :]).astype(BF16)
    gap_rows = BF16_ROWS << GAP_BITS[0]
    return pl.pallas_call(
        _dispatch_kernel,
        grid_spec=pltpu.PrefetchScalarGridSpec(
            num_scalar_prefetch=3,
            grid=(n_tiles,),
            in_specs=[
                pl.BlockSpec((SORT_TM, D), lambda i, *_: (i, 0)),
                pl.BlockSpec((SORT_TM, LANES), lambda i, *_: (i, 0)),
                pl.BlockSpec((None, 1, LANES), lambda i, *_: (i, 0, 0)),
                pl.BlockSpec((SORT_TM, SORT_TM), lambda i, *_: (0, 0)),
            ],
            out_specs=[pl.BlockSpec(memory_space=pl.ANY),
                       pl.BlockSpec((SORT_TM, LANES), lambda i, *_: (i, 0))],
            scratch_shapes=[pltpu.VMEM((2, SORT_ROWS, D), BF16), pltpu.VMEM((gap_rows, D), BF16),
                            pltpu.SemaphoreType.DMA((3,))],
        ),
        out_shape=[jax.ShapeDtypeStruct((n_rows, D), BF16), jax.ShapeDtypeStruct((T, LANES), F32)],
        compiler_params=_cparams(("arbitrary",)),
        name="moe_dispatch",
    )(dst_t, gap_t, gapdst_t, x1b, rinfo, lo_vec, tri)


def _expert_kernel(blk_e_ref, used_ref, xs_ref, wg_ref, wu_ref, wd_ref, ys_ref):
    j = pl.program_id(0)

    @pl.when(j < used_ref[0])
    def _():
        xb = xs_ref[...]
        hg = jnp.dot(xb, wg_ref[...], preferred_element_type=F32)
        hu = jnp.dot(xb, wu_ref[...], preferred_element_type=F32)
        hid = hg * jax.nn.sigmoid(hg) * hu
        ys_ref[...] = jnp.dot(hid.astype(BF16), wd_ref[...], preferred_element_type=F32).astype(ys_ref.dtype)

    @pl.when(j >= used_ref[0])
    def _():
        ys_ref[...] = jnp.zeros_like(ys_ref)


def _experts(xs, blk_e, used, wg, wu, wd):
    n_rows, D = xs.shape
    return pl.pallas_call(
        _expert_kernel,
        grid_spec=pltpu.PrefetchScalarGridSpec(
            num_scalar_prefetch=2,
            grid=(n_rows // EXPERT_TM,),
            in_specs=[
                pl.BlockSpec((EXPERT_TM, D), lambda j, be, u: (j, 0)),
                pl.BlockSpec((None, D, EXPERT_FF), lambda j, be, u: (be[j], 0, 0)),
                pl.BlockSpec((None, D, EXPERT_FF), lambda j, be, u: (be[j], 0, 0)),
                pl.BlockSpec((None, EXPERT_FF, D), lambda j, be, u: (be[j], 0, 0)),
            ],
            out_specs=pl.BlockSpec((EXPERT_TM, D), lambda j, be, u: (j, 0)),
        ),
        out_shape=jax.ShapeDtypeStruct((n_rows, D), BF16),
        compiler_params=_cparams(("arbitrary",)),
        name="moe_experts",
    )(blk_e, used, xs, wg, wu, wd)


def _combine_kernel(dst_ref, ys_ref, rinfo_ref, pos_ref, x1_ref, g2_ref, b2_ref,
                    out_ref, sorted_s, sem, *, alpha):
    i = pl.program_id(0)
    last = pl.num_programs(0) - 1
    slot = i % 2

    def unit_copy(tile, buf):
        def make(u):
            src = pl.multiple_of(dst_ref[tile * SORT_UNITS + u], BF16_ROWS)
            return pltpu.make_async_copy(
                ys_ref.at[pl.ds(src, BF16_ROWS), :],
                sorted_s.at[buf, pl.ds(pl.multiple_of(u * BF16_ROWS, BF16_ROWS), BF16_ROWS), :], sem.at[buf])
        return make

    def fetch(tile, buf):
        _for_each_unit(lambda u: unit_copy(tile, buf)(u).start())

    @pl.when(i == 0)
    def _():
        sorted_s[...] = jnp.zeros_like(sorted_s)
        fetch(0, 0)

    fetch(jnp.minimum(i + 1, last), 1 - slot)

    rinfo = rinfo_ref[...]
    pos = pos_ref[...]
    p_iota = lax.broadcasted_iota(jnp.int32, (SORT_TM, SORT_ROWS), 1)
    sel1 = jnp.where(p_iota == pos[:, 0:1].astype(jnp.int32), 1.0, 0.0).astype(BF16)
    sel2 = jnp.where(p_iota == pos[:, 1:2].astype(jnp.int32), 1.0, 0.0).astype(BF16)
    _for_each_unit(lambda u: unit_copy(i, slot)(u).wait())
    ys = sorted_s[slot]
    y = (rinfo[:, 2:3] * jnp.dot(sel1, ys, preferred_element_type=F32)
         + rinfo[:, 3:4] * jnp.dot(sel2, ys, preferred_element_type=F32))
    out_ref[...] = _layer_norm(alpha * x1_ref[...] + y, g2_ref[...], b2_ref[...])

    @pl.when(i == last)
    def _():
        _for_each_unit(lambda u: unit_copy(last, 1 - slot)(u).wait())


def _combine(ys, rinfo, pos, x1, src_t, g2, b2, alpha):
    T, D = x1.shape
    kern = functools.partial(_combine_kernel, alpha=alpha)
    return pl.pallas_call(
        kern,
        grid_spec=pltpu.PrefetchScalarGridSpec(
            num_scalar_prefetch=1,
            grid=(T // SORT_TM,),
            in_specs=[
                pl.BlockSpec(memory_space=pl.ANY),
                pl.BlockSpec((SORT_TM, LANES), lambda i, *_: (i, 0)),
                pl.BlockSpec((SORT_TM, LANES), lambda i, *_: (i, 0)),
                pl.BlockSpec((SORT_TM, D), lambda i, *_: (i, 0)),
                pl.BlockSpec((1, D), lambda i, *_: (0, 0)),
                pl.BlockSpec((1, D), lambda i, *_: (0, 0)),
            ],
            out_specs=pl.BlockSpec((SORT_TM, D), lambda i, *_: (i, 0)),
            scratch_shapes=[pltpu.VMEM((2, SORT_ROWS, D), BF16), pltpu.SemaphoreType.DMA((2,))],
        ),
        out_shape=jax.ShapeDtypeStruct((T, D), F32),
        compiler_params=_cparams(("arbitrary",)),
        name="moe_combine_ln2",
    )(src_t, ys, rinfo, pos, x1, g2, b2)


def _ceil_to(a, m):
    return (a + m - 1) // m * m


def _moe(x1b, x1, rinfo, cnt, wg, wu, wd, g2, b2, alpha):
    T, D = x1.shape
    n_tiles = T // SORT_TM
    i32 = jnp.int32
    n = cnt[:, 0, :N_EXPERTS].astype(i32).reshape(n_tiles, SORT_TM // MERGE_TM, N_EXPERTS).sum(axis=1)
    npad = _ceil_to(n, BF16_ROWS)
    lo = jnp.cumsum(npad, axis=1) - npad
    tot = npad.sum(axis=0)
    reg = _ceil_to(tot, EXPERT_TM)
    off = jnp.cumsum(reg) - reg
    g = off[None, :] + jnp.cumsum(npad, axis=0) - npad
    n_blocks = T * 2 // EXPERT_TM + n_tiles * N_EXPERTS * (BF16_ROWS - 1) // EXPERT_TM + N_EXPERTS
    spare = n_blocks * EXPERT_TM
    n_rows = spare + 2 * SORT_ROWS
    blk_end = jnp.cumsum(reg) // EXPERT_TM
    j = jnp.arange(n_rows // EXPERT_TM, dtype=i32)
    used = blk_end[-1]
    jj = jnp.minimum(j, used - 1)
    blk_e = jnp.sum(jj[:, None] >= blk_end[None, :], axis=1).astype(i32)
    lo_vec = jnp.pad(lo.astype(F32), ((0, 0), (0, LANES - N_EXPERTS))).reshape(n_tiles, 1, LANES)
    flat = lambda a: a.astype(i32).reshape(-1)
    gap_t = jnp.concatenate([flat((reg - tot) // BF16_ROWS), flat(used)])
    u_row = jnp.arange(SORT_UNITS, dtype=i32) * BF16_ROWS
    u_e = jnp.sum(u_row[None, :, None] >= (lo + npad)[:, None, :], axis=-1)
    live = u_e < N_EXPERTS
    own = u_e[:, :, None] == jnp.arange(N_EXPERTS, dtype=i32)[None, None, :]
    u_g = jnp.sum(jnp.where(own, (g - lo)[:, None, :], 0), axis=-1) + u_row[None, :]
    slot = (jnp.arange(n_tiles, dtype=i32) % 2)[:, None]
    u_dst = jnp.where(live, u_g, spare + slot * SORT_ROWS + u_row[None, :])
    u_src = jnp.where(live, u_g, 0)

    xs, pos = _dispatch(x1b, rinfo, flat(u_dst), gap_t, flat(off + tot), lo_vec, n_rows)
    ys = _experts(xs, blk_e, flat(used), wg, wu, wd)
    return _combine(ys, rinfo, pos, x1, flat(u_src), g2, b2, alpha)


def _layer(x, mem, cos2, sin2, l, depth, w_in, b_in, conv_w, conv_b, conv_ln_g, conv_ln_b,
           w_mem_kv, w_attn_o, w_conv_o, w_mem_o, w_out, ln1_g, ln1_b,
           w_group_router, b_group_router, w_expert_router, b_expert_router,
           w_exp_gate, w_exp_up, w_exp_down, ln2_g, ln2_b):
    B, S, D = x.shape
    alpha = (2.0 * depth) ** 0.25
    w_in_b = w_in[l].astype(BF16)
    b_in_l = b_in[l].reshape(1, -1)

    cols = [slice(p * ATTN_QK + g * ATTN_OUT, p * ATTN_QK + (g + 1) * ATTN_OUT)
            for g in ATTN_GROUP_IDX for p in range(3)]
    o_attn = _attention(x, cos2, sin2, jnp.concatenate([w_in_b[:, c] for c in cols], axis=1),
                        jnp.concatenate([b_in_l[:, c] for c in cols], axis=1))

    c0 = 3 * ATTN_QK
    c1 = c0 + 2 * CONV_CH
    c2 = c1 + MEM_W
    row = lambda a: a[l].reshape(1, -1)
    c, om = _conv_mem_branches(x, mem, w_in_b[:, c0:c2], b_in_l[:, c0:c2], conv_w[l], row(conv_b),
                               row(conv_ln_g), row(conv_ln_b), w_mem_kv[l].astype(BF16))

    wr = jnp.concatenate([w_group_router[l],
                          w_expert_router[l].transpose(1, 0, 2).reshape(D, N_EXPERTS)], axis=1)
    wr = jnp.pad(wr, ((0, 0), (0, LANES - wr.shape[1])))
    br = jnp.concatenate([b_group_router[l], b_expert_router[l].reshape(-1)])
    br = jnp.pad(br, (0, LANES - br.shape[0])).reshape(1, LANES)

    x1, x1b, rinfo, cnt = _merge(
        x.reshape(B * S, D), o_attn, c, om,
        w_in_b[:, c2:], b_in_l[:, c2:], w_attn_o[l].astype(BF16), w_conv_o[l].astype(BF16),
        w_mem_o[l].astype(BF16), w_out[l].astype(BF16), row(ln1_g), row(ln1_b), wr, br, alpha)

    out = _moe(x1b, x1, rinfo, cnt, w_exp_gate[l].astype(BF16), w_exp_up[l].astype(BF16),
               w_exp_down[l].astype(BF16), row(ln2_g), row(ln2_b), alpha)
    return out.reshape(B, S, D)


def kernel(x, mem, positions, w_in, b_in, conv_w, conv_b, conv_ln_g, conv_ln_b, w_mem_kv, w_attn_o,
           w_conv_o, w_mem_o, w_out, ln1_g, ln1_b, w_group_router, b_group_router, w_expert_router,
           b_expert_router, w_exp_gate, w_exp_up, w_exp_down, ln2_g, ln2_b):
    depth = w_in.shape[0]
    cos2, sin2 = _rope_tables(positions)
    for l in range(depth):
        x = _layer(x, mem, cos2, sin2, l, depth, w_in, b_in, conv_w, conv_b, conv_ln_g, conv_ln_b,
                   w_mem_kv, w_attn_o, w_conv_o, w_mem_o, w_out, ln1_g, ln1_b,
                   w_group_router, b_group_router, w_expert_router, b_expert_router,
                   w_exp_gate, w_exp_up, w_exp_down, ln2_g, ln2_b)
    return x
```

```python
import functools
import math

import jax
import jax.numpy as jnp
from jax import lax
from jax.experimental import pallas as pl
from jax.experimental.pallas import tpu as pltpu

D_MODEL = 1024
HEAD_DIM = 128
ATTN_SLOTS = 4
DIL_PATTERNS = ((128, 1), (512, 4), (2048, 16))
N_DIL = len(DIL_PATTERNS)
ATTN_QK = N_DIL * ATTN_SLOTS * HEAD_DIM
ATTN_OUT = ATTN_SLOTS * HEAD_DIM
BAND_BLOCK = 128
ROPE_THETA = 10000.0
CONV_CH = 512
CONV_WIDTH = 31
MEM_HEADS = 4
MEM_HEAD_DIM = 128
MEM_W = MEM_HEADS * MEM_HEAD_DIM
N_BRANCH = 3
N_GROUPS = 4
EXPERTS_PER_GROUP = 4
N_EXPERTS = N_GROUPS * EXPERTS_PER_GROUP
EXPERT_FF = 512
LN_EPS = 1e-5

LANES = 128
SUBLANES = 8
BF16_ROWS = 16
BLOCKS_PER_ITER = 2
ATTN_GROUP_IDX = (2, 0, 1)
ATTN_GROUP_ORDER = tuple(DIL_PATTERNS[i][1] for i in ATTN_GROUP_IDX)
MERGE_TM = 512
SORT_TM = 512
SORT_ROWS = 2 * SORT_TM + N_EXPERTS * BF16_ROWS
EXPERT_TM = 512
CONV_PAD = 32
VMEM_LIMIT = 56 * 1024 * 1024

F32 = jnp.float32
BF16 = jnp.bfloat16
NEG = -1e30


def _cparams(sem):
    return pltpu.CompilerParams(dimension_semantics=sem, vmem_limit_bytes=VMEM_LIMIT)


def _full(shape):
    n = len(shape)
    return pl.BlockSpec(shape, lambda *_: (0,) * n)


def _once(shape):
    n = len(shape)
    return pl.BlockSpec(shape, lambda *_: (0,) * n, pipeline_mode=pl.Buffered(1))


def _rope_kernel(pos_lo_ref, pos_hi_ref, inv_ref, cos_ref, sin_ref):
    hs = pos_lo_ref.shape[0]
    half = HEAD_DIM // 2
    lane = lax.broadcasted_iota(jnp.int32, (hs, HEAD_DIM), 1)
    first = lane < half
    pos = jnp.where(first, pos_lo_ref[...], pos_hi_ref[...]).astype(F32)
    ang = pos * inv_ref[...]
    c = jnp.cos(ang)
    s = jnp.sin(ang)
    c_sw = pltpu.roll(c, half, 1)
    s_sw = pltpu.roll(s, half, 1)
    cos_ref[0:hs, :] = jnp.where(first, c, c_sw)
    cos_ref[hs:2 * hs, :] = jnp.where(first, c_sw, c)
    sin_ref[0:hs, :] = jnp.where(first, -s, s_sw)
    sin_ref[hs:2 * hs, :] = jnp.where(first, -s_sw, s)


def _rope_tables(positions):
    B, S = positions.shape
    half = HEAD_DIM // 2
    inv = ROPE_THETA ** (-jnp.arange(half, dtype=F32) / half)
    inv = jnp.concatenate([inv, inv]).reshape(1, HEAD_DIM)
    pos3 = positions.reshape(B, S, 1)
    return pl.pallas_call(
        _rope_kernel,
        grid=(B,),
        in_specs=[pl.BlockSpec((None, S // 2, 1), lambda b: (b, 0, 0)),
                  pl.BlockSpec((None, S // 2, 1), lambda b: (b, 1, 0)), _full((1, HEAD_DIM))],
        out_specs=[pl.BlockSpec((None, S, HEAD_DIM), lambda b: (b, 0, 0))] * 2,
        out_shape=[jax.ShapeDtypeStruct((B, S, HEAD_DIM), F32)] * 2,
        compiler_params=_cparams(("parallel",)),
        name="rope_tables",
    )(pos3, pos3, inv)


def _attn_kernel(x_ref, cos_ref, sin_ref, w_ref, b_ref, o_ref, q_s, k_s, v_s, acc_s, m_s, l_s, *, S, rc):
    g = pl.program_id(1)
    scale = HEAD_DIM ** -0.5
    slabs = (q_s, k_s, v_s)

    def project(i):
        r0 = pl.multiple_of(i * rc, rc)
        xs = x_ref[pl.ds(r0, rc), :].astype(BF16)
        qkv = jnp.dot(xs, w_ref[...], preferred_element_type=F32) + b_ref[...]
        c = cos_ref[pl.ds(r0, rc), :]
        s = sin_ref[pl.ds(r0, rc), :]
        out = []
        for h in range(ATTN_SLOTS):
            qh = qkv[:, h * HEAD_DIM:(h + 1) * HEAD_DIM]
            out.append((qh * c + pltpu.roll(qh, HEAD_DIM // 2, 1) * s) * scale)
        for h in range(ATTN_SLOTS):
            kh = qkv[:, ATTN_OUT + h * HEAD_DIM:ATTN_OUT + (h + 1) * HEAD_DIM]
            out.append(kh * c + pltpu.roll(kh, HEAD_DIM // 2, 1) * s)
        for h in range(ATTN_SLOTS):
            out.append(qkv[:, 2 * ATTN_OUT + h * HEAD_DIM:2 * ATTN_OUT + (h + 1) * HEAD_DIM])
        return r0, out

    def proj_token_order(i, carry):
        r0, parts = project(i)
        for k, part in enumerate(parts):
            slabs[k // ATTN_SLOTS][k % ATTN_SLOTS, pl.ds(r0, rc), :] = part
        return carry

    def proj_residue_order(r):
        pc = r * BF16_ROWS
        run = pc // r
        t_i = lax.broadcasted_iota(jnp.int32, (pc, pc), 1)
        p_i = lax.broadcasted_iota(jnp.int32, (pc, pc), 0)
        sort = jnp.where(p_i == (t_i % r) * run + t_i // r, 1.0, 0.0).astype(BF16)

        def body(i, carry):
            _, parts = project(i)
            rows = jnp.concatenate(parts, axis=1).astype(BF16)
            for sub in range(rc // pc):
                chunk = i * (rc // pc) + sub
                srt = jnp.dot(sort, rows[sub * pc:(sub + 1) * pc, :], preferred_element_type=F32)
                for res in range(r):
                    dst = pl.ds(pl.multiple_of(res * BAND_BLOCK + chunk * run, run), run)
                    for k in range(3 * ATTN_SLOTS):
                        slabs[k // ATTN_SLOTS][k % ATTN_SLOTS, dst, :] = (
                            srt[res * run:(res + 1) * run, k * HEAD_DIM:(k + 1) * HEAD_DIM])
            return carry

        return body

    qi = lax.broadcasted_iota(jnp.int32, (BAND_BLOCK, 2 * BAND_BLOCK), 0)
    kj = lax.broadcasted_iota(jnp.int32, (BAND_BLOCK, 2 * BAND_BLOCK), 1)
    dist = BAND_BLOCK + qi - kj
    band = (dist >= 0) & (dist <= BAND_BLOCK)
    ones = jnp.ones((2 * BAND_BLOCK, HEAD_DIM), BF16)

    def attend(r):
        span = BAND_BLOCK * r

        def blk(i, carry):
            loaded = []
            for u in range(BLOCKS_PER_ITER):
                c = i * BLOCKS_PER_ITER + u
                res = c % r
                n = c // r
                start = n * span + res
                prev = jnp.maximum(n - 1, 0) * span + res
                valid = band & ((kj >= BAND_BLOCK) | (n > 0))
                cur_rows = pl.ds(start, BAND_BLOCK, stride=r)
                prev_rows = pl.ds(prev, BAND_BLOCK, stride=r)
                for h in range(ATTN_SLOTS):
                    q = q_s[h, cur_rows, :].astype(BF16)
                    kk = jnp.concatenate([k_s[h, prev_rows, :], k_s[h, cur_rows, :]], axis=0).astype(BF16)
                    vv = jnp.concatenate([v_s[h, prev_rows, :], v_s[h, cur_rows, :]], axis=0).astype(BF16)
                    loaded.append((h, cur_rows, valid, q, kk, vv,
                                   acc_s[h, cur_rows, :], m_s[h, cur_rows, :], l_s[h, cur_rows, :]))
            scores = [lax.dot_general(item[3], item[4], (((1,), (1,)), ((), ())), preferred_element_type=F32)
                      for item in loaded]
            probs = []
            for (_, _, valid, _, _, _, _, m_old, _), s in zip(loaded, scores):
                s = jnp.where(valid, s, NEG)
                m_blk = jnp.max(jnp.maximum(s[:, :BAND_BLOCK], s[:, BAND_BLOCK:]), axis=-1, keepdims=True)
                m_new = jnp.maximum(m_old, m_blk)
                probs.append((m_new, jnp.exp(m_old - m_new),
                              jnp.exp((s - jnp.concatenate([m_new, m_new], axis=1)).astype(BF16))))
            updated = []
            for (h, cur_rows, _, _, _, vv, acc_old, _, l_old), (m_new, a, p) in zip(loaded, probs):
                pv = jnp.dot(p, jnp.concatenate([vv, ones], axis=1), preferred_element_type=F32)
                updated.append((h, cur_rows, a * acc_old + pv[:, :HEAD_DIM], m_new,
                                a * l_old + pv[:, HEAD_DIM:]))
            for h, cur_rows, acc_new, m_new, l_new in updated:
                acc_s[h, cur_rows, :] = acc_new
                m_s[h, cur_rows, :] = m_new
                l_s[h, cur_rows, :] = l_new
            return carry

        lax.fori_loop(0, S // rc, proj_token_order, 0)
        lax.fori_loop(0, S // (BAND_BLOCK * BLOCKS_PER_ITER), blk, 0)

    def attend_first(r):
        causal = (lax.broadcasted_iota(jnp.int32, (BAND_BLOCK, BAND_BLOCK), 0)
                  >= lax.broadcasted_iota(jnp.int32, (BAND_BLOCK, BAND_BLOCK), 1))
        ones_blk = jnp.ones((BAND_BLOCK, HEAD_DIM), BF16)

        def blk(i, carry):
            loaded = []
            for u in range(BLOCKS_PER_ITER):
                res = i * BLOCKS_PER_ITER + u
                rows = pl.ds(pl.multiple_of(res * BAND_BLOCK, BAND_BLOCK), BAND_BLOCK)
                tokens = pl.ds(res, BAND_BLOCK, stride=r)
                for h in range(ATTN_SLOTS):
                    loaded.append((h, tokens, q_s[h, rows, :].astype(BF16), k_s[h, rows, :].astype(BF16),
                                   v_s[h, rows, :].astype(BF16)))
            scores = [lax.dot_general(q, kk, (((1,), (1,)), ((), ())), preferred_element_type=F32)
                      for _, _, q, kk, _ in loaded]
            probs = []
            for s in scores:
                s = jnp.where(causal, s, NEG)
                m_new = jnp.broadcast_to(jnp.max(s, axis=-1, keepdims=True), s.shape)
                probs.append((m_new, jnp.exp((s - m_new).astype(BF16))))
            updated = []
            for (h, tokens, _, _, vv), (m_new, p) in zip(loaded, probs):
                pv = jnp.dot(p, jnp.concatenate([vv, ones_blk], axis=1), preferred_element_type=F32)
                updated.append((h, tokens, pv[:, :HEAD_DIM], m_new, pv[:, HEAD_DIM:]))
            for h, tokens, acc_new, m_new, l_new in updated:
                acc_s[h, tokens, :] = acc_new
                m_s[h, tokens, :] = m_new
                l_s[h, tokens, :] = l_new
            return carry

        lax.fori_loop(0, S // rc, proj_residue_order(r), 0)
        lax.fori_loop(0, r // BLOCKS_PER_ITER, blk, 0)

    assert S // ATTN_GROUP_ORDER[0] == BAND_BLOCK
    for gi, dilation in enumerate(ATTN_GROUP_ORDER):
        pl.when(g == gi)(functools.partial(attend_first if gi == 0 else attend, dilation))

    @pl.when(g == N_DIL - 1)
    def _():
        def fin(i, carry):
            r0 = pl.multiple_of(i * rc, rc)
            for h in range(ATTN_SLOTS):
                o = acc_s[h, pl.ds(r0, rc), :] / l_s[h, pl.ds(r0, rc), :]
                o_ref[pl.ds(r0, rc), h * HEAD_DIM:(h + 1) * HEAD_DIM] = o.astype(o_ref.dtype)
            return carry

        lax.fori_loop(0, S // rc, fin, 0)


def _attention(x, cos2, sin2, w, b):
    B, S, D = x.shape
    rc = 512
    gw = 3 * ATTN_OUT
    slab = pltpu.VMEM((ATTN_SLOTS, S, HEAD_DIM), F32)
    kern = functools.partial(_attn_kernel, S=S, rc=rc)
    o = pl.pallas_call(
        kern,
        grid=(B, N_DIL),
        in_specs=[
            pl.BlockSpec((None, S, D), lambda bb, g: (bb, 0, 0)),
            pl.BlockSpec((None, S, HEAD_DIM), lambda bb, g: (bb, 0, 0), pipeline_mode=pl.Buffered(1)),
            pl.BlockSpec((None, S, HEAD_DIM), lambda bb, g: (bb, 0, 0), pipeline_mode=pl.Buffered(1)),
            pl.BlockSpec((D, gw), lambda bb, g: (0, g)),
            pl.BlockSpec((1, gw), lambda bb, g: (0, g)),
        ],
        out_specs=pl.BlockSpec((None, S, ATTN_OUT), lambda bb, g: (bb, 0, 0)),
        out_shape=jax.ShapeDtypeStruct((B, S, ATTN_OUT), BF16),
        scratch_shapes=[slab] * 6,
        compiler_params=_cparams(("parallel", "arbitrary")),
        name="dilated_attn",
    )(x, cos2, sin2, w, b)
    return o.reshape(B * S, ATTN_OUT)


def _conv_mem_kernel(x_ref, mem_ref, w_ref, b_ref, cw_ref, cb_ref, g_ref, beta_ref, wkv_ref,
                     out_ref, om_ref, cpad, shift_s, q_s, *, S, rp, rc):
    cpad[0:CONV_PAD, :] = jnp.zeros((CONV_PAD, CONV_CH), F32)
    kv = jnp.dot(mem_ref[...].astype(BF16), wkv_ref[...], preferred_element_type=F32)
    km = kv[:, :MEM_W].astype(BF16)
    vm = kv[:, MEM_W:].astype(BF16)
    scale = MEM_HEAD_DIM ** -0.5

    def glu(i, carry):
        r0 = pl.multiple_of(i * rp, rp)
        xs = x_ref[pl.ds(r0, rp), :].astype(BF16)
        u = jnp.dot(xs, w_ref[...], preferred_element_type=F32) + b_ref[...]
        cpad[pl.ds(CONV_PAD + r0, rp), :] = u[:, :CONV_CH] * jax.nn.sigmoid(u[:, CONV_CH:2 * CONV_CH])
        q_s[pl.ds(r0, rp), :] = (u[:, 2 * CONV_CH:] * scale).astype(BF16)
        return carry

    lax.fori_loop(0, S // rp, glu, 0)

    off = CONV_PAD - (CONV_WIDTH - 1)
    srows = shift_s.shape[1]

    def conv(i, carry):
        r0 = pl.multiple_of(i * rc, rc)
        win = cpad[pl.ds(r0, rc + CONV_PAD), :]
        for ph in range(1, SUBLANES):
            shift_s[ph - 1, :, :] = win[ph:ph + srows, :]
        acc = jnp.zeros((rc, CONV_CH), F32) + cb_ref[...]
        for j in range(CONV_WIDTH):
            a, ph = divmod(j + off, SUBLANES)
            if ph == 0:
                tap = win[a * SUBLANES:a * SUBLANES + rc, :]
            else:
                tap = shift_s[ph - 1, a * SUBLANES:a * SUBLANES + rc, :]
            acc = acc + tap * cw_ref[j:j + 1, :]
        mu = jnp.mean(acc, axis=-1, keepdims=True)
        d = acc - mu
        var = jnp.mean(d * d, axis=-1, keepdims=True)
        y = d * lax.rsqrt(var + LN_EPS) * g_ref[...] + beta_ref[...]
        out_ref[pl.ds(r0, rc), :] = (y * jax.nn.sigmoid(y)).astype(out_ref.dtype)
        q = q_s[pl.ds(r0, rc), :]
        for h in range(MEM_HEADS):
            lo = h * MEM_HEAD_DIM
            s = lax.dot_general(q[:, lo:lo + MEM_HEAD_DIM], km[:, lo:lo + MEM_HEAD_DIM],
                                (((1,), (1,)), ((), ())), preferred_element_type=F32)
            m = jnp.max(s, axis=-1, keepdims=True)
            p = jnp.exp(s - m)
            l = jnp.sum(p, axis=-1, keepdims=True)
            o = jnp.dot(p.astype(BF16), vm[:, lo:lo + MEM_HEAD_DIM], preferred_element_type=F32) / l
            om_ref[pl.ds(r0, rc), lo:lo + MEM_HEAD_DIM] = o.astype(om_ref.dtype)
        return carry

    lax.fori_loop(0, S // rc, conv, 0)


def _conv_mem_branches(x, mem, w_proj, b_proj, conv_w, conv_b, ln_g, ln_b, w_kv):
    B, S, D = x.shape
    M = mem.shape[1]
    rp, rc = 512, 256
    pw = 2 * CONV_CH + MEM_W
    cw = jnp.pad(conv_w, ((0, CONV_PAD - CONV_WIDTH), (0, 0)))
    kern = functools.partial(_conv_mem_kernel, S=S, rp=rp, rc=rc)
    c, om = pl.pallas_call(
        kern,
        grid=(B,),
        in_specs=[
            pl.BlockSpec((None, S, D), lambda b: (b, 0, 0)),
            pl.BlockSpec((None, M, D), lambda b: (b, 0, 0)),
            _full((D, pw)), _full((1, pw)),
            _full((CONV_PAD, CONV_CH)), _full((1, CONV_CH)), _full((1, CONV_CH)), _full((1, CONV_CH)),
            _full((D, 2 * MEM_W)),
        ],
        out_specs=[pl.BlockSpec((None, S, CONV_CH), lambda b: (b, 0, 0)),
                   pl.BlockSpec((None, S, MEM_W), lambda b: (b, 0, 0))],
        out_shape=[jax.ShapeDtypeStruct((B, S, CONV_CH), BF16), jax.ShapeDtypeStruct((B, S, MEM_W), BF16)],
        scratch_shapes=[pltpu.VMEM((CONV_PAD + S, CONV_CH), F32),
                        pltpu.VMEM((SUBLANES - 1, rc + CONV_PAD - SUBLANES, CONV_CH), F32),
                        pltpu.VMEM((S, MEM_W), BF16)],
        compiler_params=_cparams(("parallel",)),
        name="conv_and_memory_attn",
    )(x, mem, w_proj, b_proj, cw, conv_b, ln_g, ln_b, w_kv)
    return c.reshape(B * S, CONV_CH), om.reshape(B * S, MEM_W)


def _layer_norm(z, g, b):
    mu = jnp.mean(z, axis=-1, keepdims=True)
    d = z - mu
    var = jnp.mean(d * d, axis=-1, keepdims=True)
    return d * lax.rsqrt(var + LN_EPS) * g + b


def _merge_kernel(x_ref, a_ref, c_ref, m_ref,
                  wg_ref, bg_ref, wa_ref, wc_ref, wm_ref, wo_ref, g1_ref, b1_ref, wr_ref, br_ref,
                  x1_ref, x1b_ref, rinfo_ref, cnt_ref, *, alpha):
    x = x_ref[...]
    xb = x.astype(BF16)

    def gate(k):
        lo = k * D_MODEL
        return jax.nn.sigmoid(jnp.dot(xb, wg_ref[:, lo:lo + D_MODEL], preferred_element_type=F32)
                              + bg_ref[:, lo:lo + D_MODEL])

    merged = gate(0) * jnp.dot(a_ref[...], wa_ref[...], preferred_element_type=F32)
    merged = merged + gate(1) * jnp.dot(c_ref[...], wc_ref[...], preferred_element_type=F32)
    merged = merged + gate(2) * jnp.dot(m_ref[...], wm_ref[...], preferred_element_type=F32)
    mix = jnp.dot(merged.astype(BF16), wo_ref[...], preferred_element_type=F32)
    x1 = _layer_norm(alpha * x + mix, g1_ref[...], b1_ref[...])
    x1_ref[...] = x1
    x1b_ref[...] = x1.astype(BF16)

    x1_hi = x1.astype(BF16)
    x1_lo = (x1 - x1_hi.astype(F32)).astype(BF16)
    wr = wr_ref[...]
    wr_hi = wr.astype(BF16)
    wr_lo = (wr - wr_hi.astype(F32)).astype(BF16)
    t = jnp.dot(x1_hi, jnp.concatenate([wr_hi, wr_lo], axis=1), preferred_element_type=F32)
    logits = (t[:, :LANES] + t[:, LANES:]
              + jnp.dot(x1_lo, wr_hi, preferred_element_type=F32) + br_ref[...])
    lane = lax.broadcasted_iota(jnp.int32, logits.shape, 1)
    is_g = lane < N_GROUPS
    gl = jnp.where(is_g, logits, NEG)
    gmax = jnp.max(gl, axis=-1, keepdims=True)
    g_sel = jnp.min(jnp.where(is_g & (gl == gmax), lane, LANES), axis=-1, keepdims=True)
    g_prob = 1.0 / jnp.sum(jnp.where(is_g, jnp.exp(gl - gmax), 0.0), axis=-1, keepdims=True)
    e_lo = N_GROUPS + g_sel * EXPERTS_PER_GROUP
    in_grp = (lane >= e_lo) & (lane < e_lo + EXPERTS_PER_GROUP)
    el = jnp.where(in_grp, logits, NEG)
    v1 = jnp.max(el, axis=-1, keepdims=True)
    i1 = jnp.min(jnp.where(in_grp & (el == v1), lane, LANES), axis=-1, keepdims=True)
    rest = in_grp & (lane != i1)
    el2 = jnp.where(rest, logits, NEG)
    v2 = jnp.max(el2, axis=-1, keepdims=True)
    i2 = jnp.min(jnp.where(rest & (el2 == v2), lane, LANES), axis=-1, keepdims=True)
    t = jnp.exp(v2 - v1)
    w1 = g_prob / (1.0 + t)
    w2 = g_prob * t / (1.0 + t)
    e1 = (i1 - N_GROUPS).astype(F32)
    e2 = (i2 - N_GROUPS).astype(F32)
    rinfo_ref[...] = jnp.where(lane == 0, e1, jnp.where(lane == 1, e2, jnp.where(
        lane == 2, w1, jnp.where(lane == 3, w2, 0.0))))
    sel = lane + N_GROUPS
    hot = jnp.where((sel == i1) | (sel == i2), 1.0, 0.0)
    for k in range(cnt_ref.shape[0]):
        part = jnp.sum(hot[k * SORT_TM:(k + 1) * SORT_TM], axis=0, keepdims=True)
        cnt_ref[k] = jnp.broadcast_to(part, cnt_ref.shape[1:])


def _merge(x2, a, c, om, wg, bg, wa, wc, wm, wo, g1, b1, wr, br, alpha, tm=MERGE_TM):
    T, D = x2.shape
    row = lambda w: pl.BlockSpec((tm, w), lambda i: (i, 0))
    kern = functools.partial(_merge_kernel, alpha=alpha)
    return pl.pallas_call(
        kern,
        grid=(T // tm,),
        in_specs=[row(D), row(ATTN_OUT), row(CONV_CH), row(MEM_W),
                  _once((D, N_BRANCH * D)), _once((1, N_BRANCH * D)),
                  _once((ATTN_OUT, D)), _once((CONV_CH, D)), _once((MEM_W, D)), _once((D, D)),
                  _once((1, D)), _once((1, D)), _once((D, LANES)), _once((1, LANES))],
        out_specs=[row(D), row(D), row(LANES),
                   pl.BlockSpec((tm // SORT_TM, SUBLANES, LANES), lambda i: (i, 0, 0))],
        out_shape=[jax.ShapeDtypeStruct((T, D), F32), jax.ShapeDtypeStruct((T, D), BF16),
                   jax.ShapeDtypeStruct((T, LANES), F32),
                   jax.ShapeDtypeStruct((T // SORT_TM, SUBLANES, LANES), F32)],
        compiler_params=_cparams(("parallel",)),
        name="merge_ln1_router",
    )(x2, a, c, om, wg, bg, wa, wc, wm, wo, g1, b1, wr, br)


SORT_UNITS = SORT_ROWS // BF16_ROWS
GAP_BITS = tuple(range((EXPERT_TM // BF16_ROWS - 1).bit_length() - 1, -1, -1))


def _segment_copies(units, bits, make_copy, act):
    for b in bits:
        v = 1 << b

        @pl.when((units & v) != 0)
        def _():
            start = pl.multiple_of((units & (-2 * v)) * BF16_ROWS, BF16_ROWS)
            act(make_copy(start, v * BF16_ROWS))


def _sort_positions(rinfo, lo_vec, tri_ref):
    lane = lax.broadcasted_iota(jnp.int32, rinfo.shape, 1)
    e1 = rinfo[:, 0:1].astype(jnp.int32)
    e2 = rinfo[:, 1:2].astype(jnp.int32)
    hot = jnp.where((lane == e1) | (lane == e2), 1.0, 0.0).astype(BF16)
    before = jnp.dot(tri_ref[...], hot, preferred_element_type=F32) + lo_vec
    lp1 = jnp.sum(jnp.where(lane == e1, before, 0.0), axis=-1, keepdims=True)
    lp2 = jnp.sum(jnp.where(lane == e2, before, 0.0), axis=-1, keepdims=True)
    return lp1, lp2


def _for_each_unit(fn):
    for u in range(SORT_UNITS):
        fn(u)


def _dispatch_kernel(dst_ref, gap_ref, gapdst_ref,
                     xb_ref, rinfo_ref, lov_ref, tri_ref, xs_ref, pos_ref, sorted_s, zero_s, sem):
    i = pl.program_id(0)
    last = pl.num_programs(0) - 1
    slot = i % 2
    zsem = sem.at[2]

    def unit_copy(tile, buf):
        def make(u):
            dst = pl.multiple_of(dst_ref[tile * SORT_UNITS + u], BF16_ROWS)
            return pltpu.make_async_copy(
                sorted_s.at[buf, pl.ds(pl.multiple_of(u * BF16_ROWS, BF16_ROWS), BF16_ROWS), :],
                xs_ref.at[pl.ds(dst, BF16_ROWS), :], sem.at[buf])
        return make

    def drain(tile, buf):
        _for_each_unit(lambda u: unit_copy(tile, buf)(u).wait())

    def gap(e):
        dst0 = pl.multiple_of(gapdst_ref[e], BF16_ROWS)
        return lambda start, rows: pltpu.make_async_copy(
            zero_s.at[pl.ds(0, rows), :], xs_ref.at[pl.ds(dst0 + start, rows), :], zsem)

    zrows = zero_s.shape[0]

    def tail(r):
        return pltpu.make_async_copy(zero_s, xs_ref.at[pl.ds(pl.multiple_of(r * zrows, zrows), zrows), :], zsem)

    @pl.when(i == 0)
    def _():
        zero_s[...] = jnp.zeros_like(zero_s)
        for e in range(N_EXPERTS):
            _segment_copies(gap_ref[e], GAP_BITS, gap(e), lambda cp: cp.start())
        tail_lo = gap_ref[N_EXPERTS] * (EXPERT_TM // zrows)
        tail_hi = xs_ref.shape[0] // zrows
        lax.fori_loop(tail_lo, tail_hi, lambda r, c: (tail(r).start(), c)[1], 0)
        for e in range(N_EXPERTS):
            _segment_copies(gap_ref[e], GAP_BITS, gap(e), lambda cp: cp.wait())
        lax.fori_loop(tail_lo, tail_hi, lambda r, c: (tail(r).wait(), c)[1], 0)

    rinfo = rinfo_ref[...]
    lp1, lp2 = _sort_positions(rinfo, lov_ref[...], tri_ref)
    lane = lax.broadcasted_iota(jnp.int32, rinfo.shape, 1)
    pos = jnp.where(lane == 0, lp1, jnp.where(lane == 1, lp2, 0.0))
    pos_ref[...] = pos
    pos_t = pos.T
    p_iota = lax.broadcasted_iota(jnp.int32, (SORT_ROWS, SORT_TM), 0)
    perm = (p_iota == pos_t[0:1, :].astype(jnp.int32)) | (p_iota == pos_t[1:2, :].astype(jnp.int32))
    perm = jnp.where(perm, 1.0, 0.0).astype(BF16)
    srt = jnp.dot(perm, xb_ref[...], preferred_element_type=F32).astype(BF16)

    @pl.when(i >= 2)
    def _():
        drain(i - 2, slot)

    sorted_s[slot] = srt
    _for_each_unit(lambda u: unit_copy(i, slot)(u).start())

    @pl.when(i == last)
    def _():
        @pl.when(i >= 1)
        def _():
            drain(i - 1, 1 - slot)
        drain(i, slot)


def _dispatch(x1b, rinfo, dst_t, gap_t, gapdst_t, lo_vec, n_rows):
    T, D = x1b.shape
    n_tiles = T // SORT_TM
    tri = (jnp.arange(SORT_TM)[:, None] > jnp.arange(SORT_TM)[None, :]).astype(BF16)
    gap_rows = BF16_ROWS << GAP_BITS[0]
    return pl.pallas_call(
        _dispatch_kernel,
        grid_spec=pltpu.PrefetchScalarGridSpec(
            num_scalar_prefetch=3,
            grid=(n_tiles,),
            in_specs=[
                pl.BlockSpec((SORT_TM, D), lambda i, *_: (i, 0)),
                pl.BlockSpec((SORT_TM, LANES), lambda i, *_: (i, 0)),
                pl.BlockSpec((None, 1, LANES), lambda i, *_: (i, 0, 0)),
                pl.BlockSpec((SORT_TM, SORT_TM), lambda i, *_: (0, 0)),
            ],
            out_specs=[pl.BlockSpec(memory_space=pl.ANY),
                       pl.BlockSpec((SORT_TM, LANES), lambda i, *_: (i, 0))],
            scratch_shapes=[pltpu.VMEM((2, SORT_ROWS, D), BF16), pltpu.VMEM((gap_rows, D), BF16),
                            pltpu.SemaphoreType.DMA((3,))],
        ),
        out_shape=[jax.ShapeDtypeStruct((n_rows, D), BF16), jax.ShapeDtypeStruct((T, LANES), F32)],
        compiler_params=_cparams(("arbitrary",)),
        name="moe_dispatch",
    )(dst_t, gap_t, gapdst_t, x1b, rinfo, lo_vec, tri)


def _expert_kernel(blk_e_ref, used_ref, xs_ref, wg_ref, wu_ref, wd_ref, ys_ref):
    j = pl.program_id(0)

    @pl.when(j < used_ref[0])
    def _():
        xb = xs_ref[...]
        hg = jnp.dot(xb, wg_ref[...].astype(BF16), preferred_element_type=F32)
        hu = jnp.dot(xb, wu_ref[...].astype(BF16), preferred_element_type=F32)
        hid = hg * jax.nn.sigmoid(hg) * hu
        ys_ref[...] = jnp.dot(hid.astype(BF16), wd_ref[...].astype(BF16),
                              preferred_element_type=F32).astype(ys_ref.dtype)

    @pl.when(j >= used_ref[0])
    def _():
        ys_ref[...] = jnp.zeros_like(ys_ref)


def _experts(xs, blk_e, used, wg, wu, wd):
    n_rows, D = xs.shape
    return pl.pallas_call(
        _expert_kernel,
        grid_spec=pltpu.PrefetchScalarGridSpec(
            num_scalar_prefetch=2,
            grid=(n_rows // EXPERT_TM,),
            in_specs=[
                pl.BlockSpec((EXPERT_TM, D), lambda j, be, u: (j, 0)),
                pl.BlockSpec((None, D, EXPERT_FF), lambda j, be, u: (be[j], 0, 0)),
                pl.BlockSpec((None, D, EXPERT_FF), lambda j, be, u: (be[j], 0, 0)),
                pl.BlockSpec((None, EXPERT_FF, D), lambda j, be, u: (be[j], 0, 0)),
            ],
            out_specs=pl.BlockSpec((EXPERT_TM, D), lambda j, be, u: (j, 0)),
        ),
        out_shape=jax.ShapeDtypeStruct((n_rows, D), BF16),
        compiler_params=_cparams(("arbitrary",)),
        name="moe_experts",
    )(blk_e, used, xs, wg, wu, wd)


def _combine_kernel(dst_ref, ys_ref, rinfo_ref, pos_ref, x1_ref, g2_ref, b2_ref,
                    out_ref, sorted_s, sem, *, alpha):
    i = pl.program_id(0)
    last = pl.num_programs(0) - 1
    slot = i % 2

    def unit_copy(tile, buf):
        def make(u):
            src = pl.multiple_of(dst_ref[tile * SORT_UNITS + u], BF16_ROWS)
            return pltpu.make_async_copy(
                ys_ref.at[pl.ds(src, BF16_ROWS), :],
                sorted_s.at[buf, pl.ds(pl.multiple_of(u * BF16_ROWS, BF16_ROWS), BF16_ROWS), :], sem.at[buf])
        return make

    def fetch(tile, buf):
        _for_each_unit(lambda u: unit_copy(tile, buf)(u).start())

    @pl.when(i == 0)
    def _():
        sorted_s[...] = jnp.zeros_like(sorted_s)
        fetch(0, 0)

    fetch(jnp.minimum(i + 1, last), 1 - slot)

    rinfo = rinfo_ref[...]
    pos = pos_ref[...]
    p_iota = lax.broadcasted_iota(jnp.int32, (SORT_TM, SORT_ROWS), 1)
    sel1 = jnp.where(p_iota == pos[:, 0:1].astype(jnp.int32), 1.0, 0.0).astype(BF16)
    sel2 = jnp.where(p_iota == pos[:, 1:2].astype(jnp.int32), 1.0, 0.0).astype(BF16)
    _for_each_unit(lambda u: unit_copy(i, slot)(u).wait())
    ys = sorted_s[slot]
    y = (rinfo[:, 2:3] * jnp.dot(sel1, ys, preferred_element_type=F32)
         + rinfo[:, 3:4] * jnp.dot(sel2, ys, preferred_element_type=F32))
    out_ref[...] = _layer_norm(alpha * x1_ref[...] + y, g2_ref[...], b2_ref[...])

    @pl.when(i == last)
    def _():
        _for_each_unit(lambda u: unit_copy(last, 1 - slot)(u).wait())


def _combine(ys, rinfo, pos, x1, src_t, g2, b2, alpha):
    T, D = x1.shape
    kern = functools.partial(_combine_kernel, alpha=alpha)
    return pl.pallas_call(
        kern,
        grid_spec=pltpu.PrefetchScalarGridSpec(
            num_scalar_prefetch=1,
            grid=(T // SORT_TM,),
            in_specs=[
                pl.BlockSpec(memory_space=pl.ANY),
                pl.BlockSpec((SORT_TM, LANES), lambda i, *_: (i, 0)),
                pl.BlockSpec((SORT_TM, LANES), lambda i, *_: (i, 0)),
                pl.BlockSpec((SORT_TM, D), lambda i, *_: (i, 0)),
                pl.BlockSpec((1, D), lambda i, *_: (0, 0)),
                pl.BlockSpec((1, D), lambda i, *_: (0, 0)),
            ],
            out_specs=pl.BlockSpec((SORT_TM, D), lambda i, *_: (i, 0)),
            scratch_shapes=[pltpu.VMEM((2, SORT_ROWS, D), BF16), pltpu.SemaphoreType.DMA((2,))],
        ),
        out_shape=jax.ShapeDtypeStruct((T, D), F32),
        compiler_params=_cparams(("arbitrary",)),
        name="moe_combine_ln2",
    )(src_t, ys, rinfo, pos, x1, g2, b2)


def _ceil_to(a, m):
    return (a + m - 1) // m * m


def _moe(x1b, x1, rinfo, cnt, wg, wu, wd, g2, b2, alpha):
    T, D = x1.shape
    n_tiles = T // SORT_TM
    i32 = jnp.int32
    n = cnt[:, 0, :N_EXPERTS].astype(i32)
    npad = _ceil_to(n, BF16_ROWS)
    lo = jnp.cumsum(npad, axis=1) - npad
    tot = npad.sum(axis=0)
    reg = _ceil_to(tot, EXPERT_TM)
    off = jnp.cumsum(reg) - reg
    g = off[None, :] + jnp.cumsum(npad, axis=0) - npad
    n_blocks = T * 2 // EXPERT_TM + n_tiles * N_EXPERTS * (BF16_ROWS - 1) // EXPERT_TM + N_EXPERTS
    spare = n_blocks * EXPERT_TM
    n_rows = spare + 2 * SORT_ROWS
    blk_end = jnp.cumsum(reg) // EXPERT_TM
    j = jnp.arange(n_rows // EXPERT_TM, dtype=i32)
    used = blk_end[-1]
    jj = jnp.minimum(j, used - 1)
    blk_e = jnp.sum(jj[:, None] >= blk_end[None, :], axis=1).astype(i32)
    lo_vec = jnp.pad(lo.astype(F32), ((0, 0), (0, LANES - N_EXPERTS))).reshape(n_tiles, 1, LANES)
    flat = lambda a: a.astype(i32).reshape(-1)
    gap_t = jnp.concatenate([flat((reg - tot) // BF16_ROWS), flat(used)])
    u_row = jnp.arange(SORT_UNITS, dtype=i32) * BF16_ROWS
    u_e = jnp.sum(u_row[None, :, None] >= (lo + npad)[:, None, :], axis=-1)
    live = u_e < N_EXPERTS
    own = u_e[:, :, None] == jnp.arange(N_EXPERTS, dtype=i32)[None, None, :]
    u_g = jnp.sum(jnp.where(own, (g - lo)[:, None, :], 0), axis=-1) + u_row[None, :]
    slot = (jnp.arange(n_tiles, dtype=i32) % 2)[:, None]
    u_dst = jnp.where(live, u_g, spare + slot * SORT_ROWS + u_row[None, :])
    u_src = jnp.where(live, u_g, 0)

    xs, pos = _dispatch(x1b, rinfo, flat(u_dst), gap_t, flat(off + tot), lo_vec, n_rows)
    ys = _experts(xs, blk_e, flat(used), wg, wu, wd)
    return _combine(ys, rinfo, pos, x1, flat(u_src), g2, b2, alpha)


def _layer(x, mem, cos2, sin2, l, depth, w_in, b_in, conv_w, conv_b, conv_ln_g, conv_ln_b,
           w_mem_kv, w_attn_o, w_conv_o, w_mem_o, w_out, ln1_g, ln1_b,
           w_group_router, b_group_router, w_expert_router, b_expert_router,
           w_exp_gate, w_exp_up, w_exp_down, ln2_g, ln2_b):
    B, S, D = x.shape
    alpha = (2.0 * depth) ** 0.25
    w_in_b = w_in[l].astype(BF16)
    b_in_l = b_in[l].reshape(1, -1)

    cols = [slice(p * ATTN_QK + g * ATTN_OUT, p * ATTN_QK + (g + 1) * ATTN_OUT)
            for g in ATTN_GROUP_IDX for p in range(3)]
    o_attn = _attention(x, cos2, sin2, jnp.concatenate([w_in_b[:, c] for c in cols], axis=1),
                        jnp.concatenate([b_in_l[:, c] for c in cols], axis=1))

    c0 = 3 * ATTN_QK
    c1 = c0 + 2 * CONV_CH
    c2 = c1 + MEM_W
    row = lambda a: a[l].reshape(1, -1)
    c, om = _conv_mem_branches(x, mem, w_in_b[:, c0:c2], b_in_l[:, c0:c2], conv_w[l], row(conv_b),
                               row(conv_ln_g), row(conv_ln_b), w_mem_kv[l].astype(BF16))

    wr = jnp.concatenate([w_group_router[l],
                          w_expert_router[l].transpose(1, 0, 2).reshape(D, N_EXPERTS)], axis=1)
    wr = jnp.pad(wr, ((0, 0), (0, LANES - wr.shape[1])))
    br = jnp.concatenate([b_group_router[l], b_expert_router[l].reshape(-1)])
    br = jnp.pad(br, (0, LANES - br.shape[0])).reshape(1, LANES)

    x1, x1b, rinfo, cnt = _merge(
        x.reshape(B * S, D), o_attn, c, om,
        w_in_b[:, c2:], b_in_l[:, c2:], w_attn_o[l].astype(BF16), w_conv_o[l].astype(BF16),
        w_mem_o[l].astype(BF16), w_out[l].astype(BF16), row(ln1_g), row(ln1_b), wr, br, alpha)

    out = _moe(x1b, x1, rinfo, cnt, w_exp_gate[l], w_exp_up[l], w_exp_down[l],
               row(ln2_g), row(ln2_b), alpha)
    return out.reshape(B, S, D)


def kernel(x, mem, positions, w_in, b_in, conv_w, conv_b, conv_ln_g, conv_ln_b, w_mem_kv, w_attn_o,
           w_conv_o, w_mem_o, w_out, ln1_g, ln1_b, w_group_router, b_group_router, w_expert_router,
           b_expert_router, w_exp_gate, w_exp_up, w_exp_down, ln2_g, ln2_b):
    depth = w_in.shape[0]
    cos2, sin2 = _rope_tables(positions)
    for l in range(depth):
        x = _layer(x, mem, cos2, sin2, l, depth, w_in, b_in, conv_w, conv_b, conv_ln_g, conv_ln_b,
                   w_mem_kv, w_attn_o, w_conv_o, w_mem_o, w_out, ln1_g, ln1_b,
                   w_group_router, b_group_router, w_expert_router, b_expert_router,
                   w_exp_gate, w_exp_up, w_exp_down, ln2_g, ln2_b)
    return x
```

```python
import functools
import math

import jax
import jax.numpy as jnp
from jax import lax
from jax.experimental import pallas as pl
from jax.experimental.pallas import tpu as pltpu

D_MODEL = 1024
HEAD_DIM = 128
ATTN_SLOTS = 4
DIL_PATTERNS = ((128, 1), (512, 4), (2048, 16))
N_DIL = len(DIL_PATTERNS)
ATTN_QK = N_DIL * ATTN_SLOTS * HEAD_DIM
ATTN_OUT = ATTN_SLOTS * HEAD_DIM
BAND_BLOCK = 128
ROPE_THETA = 10000.0
CONV_CH = 512
CONV_WIDTH = 31
MEM_HEADS = 4
MEM_HEAD_DIM = 128
MEM_W = MEM_HEADS * MEM_HEAD_DIM
N_BRANCH = 3
N_GROUPS = 4
EXPERTS_PER_GROUP = 4
N_EXPERTS = N_GROUPS * EXPERTS_PER_GROUP
EXPERT_FF = 512
LN_EPS = 1e-5

LANES = 128
SUBLANES = 8
BF16_ROWS = 16
BLOCKS_PER_ITER = 2
ATTN_GROUP_IDX = (2, 0, 1)
ATTN_GROUP_ORDER = tuple(DIL_PATTERNS[i][1] for i in ATTN_GROUP_IDX)
MERGE_TM = 1024
MERGE_TAIL_PARTS = 1
SORT_TM = 512
SORT_ROWS = 2 * SORT_TM + N_EXPERTS * BF16_ROWS
EXPERT_TM = 512
CONV_PAD = 32
VMEM_LIMIT = 56 * 1024 * 1024

F32 = jnp.float32
BF16 = jnp.bfloat16
NEG = -1e30


def _cparams(sem):
    return pltpu.CompilerParams(dimension_semantics=sem, vmem_limit_bytes=VMEM_LIMIT)


def _full(shape):
    n = len(shape)
    return pl.BlockSpec(shape, lambda *_: (0,) * n)


def _once(shape):
    n = len(shape)
    return pl.BlockSpec(shape, lambda *_: (0,) * n, pipeline_mode=pl.Buffered(1))


def _rope_kernel(pos_lo_ref, pos_hi_ref, inv_ref, cos_ref, sin_ref):
    hs = pos_lo_ref.shape[0]
    half = HEAD_DIM // 2
    lane = lax.broadcasted_iota(jnp.int32, (hs, HEAD_DIM), 1)
    first = lane < half
    pos = jnp.where(first, pos_lo_ref[...], pos_hi_ref[...]).astype(F32)
    ang = pos * inv_ref[...]
    c = jnp.cos(ang)
    s = jnp.sin(ang)
    c_sw = pltpu.roll(c, half, 1)
    s_sw = pltpu.roll(s, half, 1)
    cos_ref[0:hs, :] = jnp.where(first, c, c_sw)
    cos_ref[hs:2 * hs, :] = jnp.where(first, c_sw, c)
    sin_ref[0:hs, :] = jnp.where(first, -s, s_sw)
    sin_ref[hs:2 * hs, :] = jnp.where(first, -s_sw, s)


def _rope_tables(positions):
    B, S = positions.shape
    half = HEAD_DIM // 2
    inv = ROPE_THETA ** (-jnp.arange(half, dtype=F32) / half)
    inv = jnp.concatenate([inv, inv]).reshape(1, HEAD_DIM)
    pos3 = positions.reshape(B, S, 1)
    return pl.pallas_call(
        _rope_kernel,
        grid=(B,),
        in_specs=[pl.BlockSpec((None, S // 2, 1), lambda b: (b, 0, 0)),
                  pl.BlockSpec((None, S // 2, 1), lambda b: (b, 1, 0)), _full((1, HEAD_DIM))],
        out_specs=[pl.BlockSpec((None, S, HEAD_DIM), lambda b: (b, 0, 0))] * 2,
        out_shape=[jax.ShapeDtypeStruct((B, S, HEAD_DIM), F32)] * 2,
        compiler_params=_cparams(("parallel",)),
        name="rope_tables",
    )(pos3, pos3, inv)


def _attn_kernel(x_ref, cos_ref, sin_ref, w_ref, b_ref, o_ref, q_s, k_s, v_s, acc_s, m_s, l_s, *, S, rc):
    g = pl.program_id(1)
    scale = HEAD_DIM ** -0.5
    slabs = (q_s, k_s, v_s)

    def project(i):
        r0 = pl.multiple_of(i * rc, rc)
        xs = x_ref[pl.ds(r0, rc), :].astype(BF16)
        qkv = jnp.dot(xs, w_ref[...], preferred_element_type=F32) + b_ref[...]
        c = cos_ref[pl.ds(r0, rc), :]
        s = sin_ref[pl.ds(r0, rc), :]
        out = []
        for h in range(ATTN_SLOTS):
            qh = qkv[:, h * HEAD_DIM:(h + 1) * HEAD_DIM]
            out.append((qh * c + pltpu.roll(qh, HEAD_DIM // 2, 1) * s) * scale)
        for h in range(ATTN_SLOTS):
            kh = qkv[:, ATTN_OUT + h * HEAD_DIM:ATTN_OUT + (h + 1) * HEAD_DIM]
            out.append(kh * c + pltpu.roll(kh, HEAD_DIM // 2, 1) * s)
        for h in range(ATTN_SLOTS):
            out.append(qkv[:, 2 * ATTN_OUT + h * HEAD_DIM:2 * ATTN_OUT + (h + 1) * HEAD_DIM])
        return r0, out

    def proj_token_order(i, carry):
        r0, parts = project(i)
        for k, part in enumerate(parts):
            slabs[k // ATTN_SLOTS][k % ATTN_SLOTS, pl.ds(r0, rc), :] = part
        return carry

    def proj_residue_order(r):
        pc = r * BF16_ROWS
        run = pc // r
        t_i = lax.broadcasted_iota(jnp.int32, (pc, pc), 1)
        p_i = lax.broadcasted_iota(jnp.int32, (pc, pc), 0)
        sort = jnp.where(p_i == (t_i % r) * run + t_i // r, 1.0, 0.0).astype(BF16)

        def body(i, carry):
            _, parts = project(i)
            rows = jnp.concatenate(parts, axis=1).astype(BF16)
            for sub in range(rc // pc):
                chunk = i * (rc // pc) + sub
                srt = jnp.dot(sort, rows[sub * pc:(sub + 1) * pc, :], preferred_element_type=F32)
                for res in range(r):
                    dst = pl.ds(pl.multiple_of(res * BAND_BLOCK + chunk * run, run), run)
                    for k in range(3 * ATTN_SLOTS):
                        slabs[k // ATTN_SLOTS][k % ATTN_SLOTS, dst, :] = (
                            srt[res * run:(res + 1) * run, k * HEAD_DIM:(k + 1) * HEAD_DIM])
            return carry

        return body

    qi = lax.broadcasted_iota(jnp.int32, (BAND_BLOCK, 2 * BAND_BLOCK), 0)
    kj = lax.broadcasted_iota(jnp.int32, (BAND_BLOCK, 2 * BAND_BLOCK), 1)
    dist = BAND_BLOCK + qi - kj
    band = (dist >= 0) & (dist <= BAND_BLOCK)
    ones = jnp.ones((2 * BAND_BLOCK, HEAD_DIM), BF16)

    def attend(r):
        span = BAND_BLOCK * r

        def blk(i, carry):
            loaded = []
            for u in range(BLOCKS_PER_ITER):
                c = i * BLOCKS_PER_ITER + u
                res = c % r
                n = c // r
                start = n * span + res
                prev = jnp.maximum(n - 1, 0) * span + res
                valid = band & ((kj >= BAND_BLOCK) | (n > 0))
                cur_rows = pl.ds(start, BAND_BLOCK, stride=r)
                prev_rows = pl.ds(prev, BAND_BLOCK, stride=r)
                for h in range(ATTN_SLOTS):
                    q = q_s[h, cur_rows, :].astype(BF16)
                    kk = jnp.concatenate([k_s[h, prev_rows, :], k_s[h, cur_rows, :]], axis=0).astype(BF16)
                    vv = jnp.concatenate([v_s[h, prev_rows, :], v_s[h, cur_rows, :]], axis=0).astype(BF16)
                    loaded.append((h, cur_rows, valid, q, kk, vv,
                                   acc_s[h, cur_rows, :], m_s[h, cur_rows, :], l_s[h, cur_rows, :]))
            scores = [lax.dot_general(item[3], item[4], (((1,), (1,)), ((), ())), preferred_element_type=F32)
                      for item in loaded]
            probs = []
            for (_, _, valid, _, _, _, _, m_old, _), s in zip(loaded, scores):
                s = jnp.where(valid, s, NEG)
                m_blk = jnp.max(jnp.maximum(s[:, :BAND_BLOCK], s[:, BAND_BLOCK:]), axis=-1, keepdims=True)
                m_new = jnp.maximum(m_old, m_blk)
                probs.append((m_new, jnp.exp(m_old - m_new),
                              jnp.exp((s - jnp.concatenate([m_new, m_new], axis=1)).astype(BF16))))
            updated = []
            for (h, cur_rows, _, _, _, vv, acc_old, _, l_old), (m_new, a, p) in zip(loaded, probs):
                pv = jnp.dot(p, jnp.concatenate([vv, ones], axis=1), preferred_element_type=F32)
                updated.append((h, cur_rows, a * acc_old + pv[:, :HEAD_DIM], m_new,
                                a * l_old + pv[:, HEAD_DIM:]))
            for h, cur_rows, acc_new, m_new, l_new in updated:
                acc_s[h, cur_rows, :] = acc_new
                m_s[h, cur_rows, :] = m_new
                l_s[h, cur_rows, :] = l_new
            return carry

        lax.fori_loop(0, S // rc, proj_token_order, 0)
        lax.fori_loop(0, S // (BAND_BLOCK * BLOCKS_PER_ITER), blk, 0)

    def attend_first(r):
        causal = (lax.broadcasted_iota(jnp.int32, (BAND_BLOCK, BAND_BLOCK), 0)
                  >= lax.broadcasted_iota(jnp.int32, (BAND_BLOCK, BAND_BLOCK), 1))
        ones_blk = jnp.ones((BAND_BLOCK, HEAD_DIM), BF16)

        def blk(i, carry):
            loaded = []
            for u in range(BLOCKS_PER_ITER):
                res = i * BLOCKS_PER_ITER + u
                rows = pl.ds(pl.multiple_of(res * BAND_BLOCK, BAND_BLOCK), BAND_BLOCK)
                tokens = pl.ds(res, BAND_BLOCK, stride=r)
                for h in range(ATTN_SLOTS):
                    loaded.append((h, tokens, q_s[h, rows, :].astype(BF16), k_s[h, rows, :].astype(BF16),
                                   v_s[h, rows, :].astype(BF16)))
            scores = [lax.dot_general(q, kk, (((1,), (1,)), ((), ())), preferred_element_type=F32)
                      for _, _, q, kk, _ in loaded]
            probs = []
            for s in scores:
                s = jnp.where(causal, s, NEG)
                m_new = jnp.broadcast_to(jnp.max(s, axis=-1, keepdims=True), s.shape)
                probs.append((m_new, jnp.exp((s - m_new).astype(BF16))))
            updated = []
            for (h, tokens, _, _, vv), (m_new, p) in zip(loaded, probs):
                pv = jnp.dot(p, jnp.concatenate([vv, ones_blk], axis=1), preferred_element_type=F32)
                updated.append((h, tokens, pv[:, :HEAD_DIM], m_new, pv[:, HEAD_DIM:]))
            for h, tokens, acc_new, m_new, l_new in updated:
                acc_s[h, tokens, :] = acc_new
                m_s[h, tokens, :] = m_new
                l_s[h, tokens, :] = l_new
            return carry

        lax.fori_loop(0, S // rc, proj_residue_order(r), 0)
        lax.fori_loop(0, r // BLOCKS_PER_ITER, blk, 0)

    assert S // ATTN_GROUP_ORDER[0] == BAND_BLOCK
    for gi, dilation in enumerate(ATTN_GROUP_ORDER):
        pl.when(g == gi)(functools.partial(attend_first if gi == 0 else attend, dilation))

    @pl.when(g == N_DIL - 1)
    def _():
        def fin(i, carry):
            r0 = pl.multiple_of(i * rc, rc)
            for h in range(ATTN_SLOTS):
                o = acc_s[h, pl.ds(r0, rc), :] / l_s[h, pl.ds(r0, rc), :]
                o_ref[pl.ds(r0, rc), h * HEAD_DIM:(h + 1) * HEAD_DIM] = o.astype(o_ref.dtype)
            return carry

        lax.fori_loop(0, S // rc, fin, 0)


def _attention(x, cos2, sin2, w, b):
    B, S, D = x.shape
    rc = 512
    gw = 3 * ATTN_OUT
    slab = pltpu.VMEM((ATTN_SLOTS, S, HEAD_DIM), F32)
    kern = functools.partial(_attn_kernel, S=S, rc=rc)
    o = pl.pallas_call(
        kern,
        grid=(B, N_DIL),
        in_specs=[
            pl.BlockSpec((None, S, D), lambda bb, g: (bb, 0, 0)),
            pl.BlockSpec((None, S, HEAD_DIM), lambda bb, g: (bb, 0, 0), pipeline_mode=pl.Buffered(1)),
            pl.BlockSpec((None, S, HEAD_DIM), lambda bb, g: (bb, 0, 0), pipeline_mode=pl.Buffered(1)),
            pl.BlockSpec((D, gw), lambda bb, g: (0, g)),
            pl.BlockSpec((1, gw), lambda bb, g: (0, g)),
        ],
        out_specs=pl.BlockSpec((None, S, ATTN_OUT), lambda bb, g: (bb, 0, 0)),
        out_shape=jax.ShapeDtypeStruct((B, S, ATTN_OUT), BF16),
        scratch_shapes=[slab] * 6,
        compiler_params=_cparams(("parallel", "arbitrary")),
        name="dilated_attn",
    )(x, cos2, sin2, w, b)
    return o.reshape(B * S, ATTN_OUT)


def _conv_mem_kernel(x_ref, mem_ref, w_ref, b_ref, cw_ref, cb_ref, g_ref, beta_ref, wkv_ref,
                     out_ref, om_ref, cpad, shift_s, q_s, *, S, rp, rc):
    cpad[0:CONV_PAD, :] = jnp.zeros((CONV_PAD, CONV_CH), F32)
    kv = jnp.dot(mem_ref[...].astype(BF16), wkv_ref[...], preferred_element_type=F32)
    km = kv[:, :MEM_W].astype(BF16)
    vm = kv[:, MEM_W:].astype(BF16)
    scale = MEM_HEAD_DIM ** -0.5

    def glu(i, carry):
        r0 = pl.multiple_of(i * rp, rp)
        xs = x_ref[pl.ds(r0, rp), :].astype(BF16)
        u = jnp.dot(xs, w_ref[...], preferred_element_type=F32) + b_ref[...]
        cpad[pl.ds(CONV_PAD + r0, rp), :] = u[:, :CONV_CH] * jax.nn.sigmoid(u[:, CONV_CH:2 * CONV_CH])
        q_s[pl.ds(r0, rp), :] = (u[:, 2 * CONV_CH:] * scale).astype(BF16)
        return carry

    lax.fori_loop(0, S // rp, glu, 0)

    off = CONV_PAD - (CONV_WIDTH - 1)
    srows = shift_s.shape[1]

    def conv(i, carry):
        r0 = pl.multiple_of(i * rc, rc)
        win = cpad[pl.ds(r0, rc + CONV_PAD), :]
        for ph in range(1, SUBLANES):
            shift_s[ph - 1, :, :] = win[ph:ph + srows, :]
        acc = jnp.zeros((rc, CONV_CH), F32) + cb_ref[...]
        for j in range(CONV_WIDTH):
            a, ph = divmod(j + off, SUBLANES)
            if ph == 0:
                tap = win[a * SUBLANES:a * SUBLANES + rc, :]
            else:
                tap = shift_s[ph - 1, a * SUBLANES:a * SUBLANES + rc, :]
            acc = acc + tap * cw_ref[j:j + 1, :]
        mu = jnp.mean(acc, axis=-1, keepdims=True)
        d = acc - mu
        var = jnp.mean(d * d, axis=-1, keepdims=True)
        y = d * lax.rsqrt(var + LN_EPS) * g_ref[...] + beta_ref[...]
        out_ref[pl.ds(r0, rc), :] = (y * jax.nn.sigmoid(y)).astype(out_ref.dtype)
        q = q_s[pl.ds(r0, rc), :]
        for h in range(MEM_HEADS):
            lo = h * MEM_HEAD_DIM
            s = lax.dot_general(q[:, lo:lo + MEM_HEAD_DIM], km[:, lo:lo + MEM_HEAD_DIM],
                                (((1,), (1,)), ((), ())), preferred_element_type=F32)
            m = jnp.max(s, axis=-1, keepdims=True)
            p = jnp.exp(s - m)
            l = jnp.sum(p, axis=-1, keepdims=True)
            o = jnp.dot(p.astype(BF16), vm[:, lo:lo + MEM_HEAD_DIM], preferred_element_type=F32) / l
            om_ref[pl.ds(r0, rc), lo:lo + MEM_HEAD_DIM] = o.astype(om_ref.dtype)
        return carry

    lax.fori_loop(0, S // rc, conv, 0)


def _conv_mem_branches(x, mem, w_proj, b_proj, conv_w, conv_b, ln_g, ln_b, w_kv):
    B, S, D = x.shape
    M = mem.shape[1]
    rp, rc = 512, 256
    pw = 2 * CONV_CH + MEM_W
    cw = jnp.pad(conv_w, ((0, CONV_PAD - CONV_WIDTH), (0, 0)))
    kern = functools.partial(_conv_mem_kernel, S=S, rp=rp, rc=rc)
    c, om = pl.pallas_call(
        kern,
        grid=(B,),
        in_specs=[
            pl.BlockSpec((None, S, D), lambda b: (b, 0, 0)),
            pl.BlockSpec((None, M, D), lambda b: (b, 0, 0)),
            _full((D, pw)), _full((1, pw)),
            _full((CONV_PAD, CONV_CH)), _full((1, CONV_CH)), _full((1, CONV_CH)), _full((1, CONV_CH)),
            _full((D, 2 * MEM_W)),
        ],
        out_specs=[pl.BlockSpec((None, S, CONV_CH), lambda b: (b, 0, 0)),
                   pl.BlockSpec((None, S, MEM_W), lambda b: (b, 0, 0))],
        out_shape=[jax.ShapeDtypeStruct((B, S, CONV_CH), BF16), jax.ShapeDtypeStruct((B, S, MEM_W), BF16)],
        scratch_shapes=[pltpu.VMEM((CONV_PAD + S, CONV_CH), F32),
                        pltpu.VMEM((SUBLANES - 1, rc + CONV_PAD - SUBLANES, CONV_CH), F32),
                        pltpu.VMEM((S, MEM_W), BF16)],
        compiler_params=_cparams(("parallel",)),
        name="conv_and_memory_attn",
    )(x, mem, w_proj, b_proj, cw, conv_b, ln_g, ln_b, w_kv)
    return c.reshape(B * S, CONV_CH), om.reshape(B * S, MEM_W)


def _layer_norm(z, g, b):
    mu = jnp.mean(z, axis=-1, keepdims=True)
    d = z - mu
    var = jnp.mean(d * d, axis=-1, keepdims=True)
    return d * lax.rsqrt(var + LN_EPS) * g + b


def _merge_kernel(x_ref, a_ref, c_ref, m_ref,
                  wg_ref, bg_ref, wa_ref, wc_ref, wm_ref, wo_ref, g1_ref, b1_ref, wr_ref, br_ref,
                  x1_ref, x1b_ref, rinfo_ref, cnt_ref, *, alpha):
    x = x_ref[...]
    xb = x.astype(BF16)

    def gate(k):
        lo = k * D_MODEL
        return jax.nn.sigmoid(jnp.dot(xb, wg_ref[:, lo:lo + D_MODEL], preferred_element_type=F32)
                              + bg_ref[:, lo:lo + D_MODEL])

    merged = gate(0) * jnp.dot(a_ref[...], wa_ref[...], preferred_element_type=F32)
    merged = merged + gate(1) * jnp.dot(c_ref[...], wc_ref[...], preferred_element_type=F32)
    merged = merged + gate(2) * jnp.dot(m_ref[...], wm_ref[...], preferred_element_type=F32)
    merged = merged.astype(BF16)
    wr = wr_ref[...]
    wr_hi = wr.astype(BF16)
    wr_lo = (wr - wr_hi.astype(F32)).astype(BF16)
    wr_cat = jnp.concatenate([wr_hi, wr_lo], axis=1)
    part_rows = x.shape[0] // MERGE_TAIL_PARTS
    hot = jnp.concatenate([
        _merge_tail(x[p * part_rows:(p + 1) * part_rows], merged[p * part_rows:(p + 1) * part_rows],
                    slice(p * part_rows, (p + 1) * part_rows), wo_ref, g1_ref, b1_ref, wr_cat, wr_hi, br_ref,
                    x1_ref, x1b_ref, rinfo_ref, alpha)
        for p in range(MERGE_TAIL_PARTS)], axis=0)
    for k in range(cnt_ref.shape[0]):
        part = jnp.sum(hot[k * SORT_TM:(k + 1) * SORT_TM], axis=0, keepdims=True)
        cnt_ref[k] = jnp.broadcast_to(part, cnt_ref.shape[1:])


def _merge_tail(x, merged, rows, wo_ref, g1_ref, b1_ref, wr_cat, wr_hi, br_ref,
                x1_ref, x1b_ref, rinfo_ref, alpha):
    mix = jnp.dot(merged, wo_ref[...], preferred_element_type=F32)
    x1 = _layer_norm(alpha * x + mix, g1_ref[...], b1_ref[...])
    x1_ref[rows, :] = x1
    x1_hi = x1.astype(BF16)
    x1b_ref[rows, :] = x1_hi
    x1_lo = (x1 - x1_hi.astype(F32)).astype(BF16)
    t = jnp.dot(x1_hi, wr_cat, preferred_element_type=F32)
    logits = (t[:, :LANES] + t[:, LANES:]
              + jnp.dot(x1_lo, wr_hi, preferred_element_type=F32) + br_ref[...])
    lane = lax.broadcasted_iota(jnp.int32, logits.shape, 1)
    is_g = lane < N_GROUPS
    gl = jnp.where(is_g, logits, NEG)
    gmax = jnp.max(gl, axis=-1, keepdims=True)
    g_sel = jnp.min(jnp.where(is_g & (gl == gmax), lane, LANES), axis=-1, keepdims=True)
    g_prob = 1.0 / jnp.sum(jnp.where(is_g, jnp.exp(gl - gmax), 0.0), axis=-1, keepdims=True)
    e_lo = N_GROUPS + g_sel * EXPERTS_PER_GROUP
    in_grp = (lane >= e_lo) & (lane < e_lo + EXPERTS_PER_GROUP)
    el = jnp.where(in_grp, logits, NEG)
    v1 = jnp.max(el, axis=-1, keepdims=True)
    i1 = jnp.min(jnp.where(in_grp & (el == v1), lane, LANES), axis=-1, keepdims=True)
    rest = in_grp & (lane != i1)
    el2 = jnp.where(rest, logits, NEG)
    v2 = jnp.max(el2, axis=-1, keepdims=True)
    i2 = jnp.min(jnp.where(rest & (el2 == v2), lane, LANES), axis=-1, keepdims=True)
    t = jnp.exp(v2 - v1)
    w1 = g_prob / (1.0 + t)
    w2 = g_prob * t / (1.0 + t)
    e1 = (i1 - N_GROUPS).astype(F32)
    e2 = (i2 - N_GROUPS).astype(F32)
    rinfo_ref[rows, :] = jnp.where(lane == 0, e1, jnp.where(lane == 1, e2, jnp.where(
        lane == 2, w1, jnp.where(lane == 3, w2, 0.0))))
    sel = lane + N_GROUPS
    return jnp.where((sel == i1) | (sel == i2), 1.0, 0.0)


def _merge(x2, a, c, om, wg, bg, wa, wc, wm, wo, g1, b1, wr, br, alpha, tm=MERGE_TM):
    T, D = x2.shape
    row = lambda w: pl.BlockSpec((tm, w), lambda i: (i, 0))
    kern = functools.partial(_merge_kernel, alpha=alpha)
    return pl.pallas_call(
        kern,
        grid=(T // tm,),
        in_specs=[row(D), row(ATTN_OUT), row(CONV_CH), row(MEM_W),
                  _once((D, N_BRANCH * D)), _once((1, N_BRANCH * D)),
                  _once((ATTN_OUT, D)), _once((CONV_CH, D)), _once((MEM_W, D)), _once((D, D)),
                  _once((1, D)), _once((1, D)), _once((D, LANES)), _once((1, LANES))],
        out_specs=[row(D), row(D), row(LANES),
                   pl.BlockSpec((tm // SORT_TM, SUBLANES, LANES), lambda i: (i, 0, 0))],
        out_shape=[jax.ShapeDtypeStruct((T, D), F32), jax.ShapeDtypeStruct((T, D), BF16),
                   jax.ShapeDtypeStruct((T, LANES), F32),
                   jax.ShapeDtypeStruct((T // SORT_TM, SUBLANES, LANES), F32)],
        compiler_params=_cparams(("parallel",)),
        name="merge_ln1_router",
    )(x2, a, c, om, wg, bg, wa, wc, wm, wo, g1, b1, wr, br)


SORT_UNITS = SORT_ROWS // BF16_ROWS
GAP_BITS = tuple(range((EXPERT_TM // BF16_ROWS - 1).bit_length() - 1, -1, -1))


def _segment_copies(units, bits, make_copy, act):
    for b in bits:
        v = 1 << b

        @pl.when((units & v) != 0)
        def _():
            start = pl.multiple_of((units & (-2 * v)) * BF16_ROWS, BF16_ROWS)
            act(make_copy(start, v * BF16_ROWS))


def _sort_positions(rinfo, lo_vec, tri_ref):
    lane = lax.broadcasted_iota(jnp.int32, rinfo.shape, 1)
    e1 = rinfo[:, 0:1].astype(jnp.int32)
    e2 = rinfo[:, 1:2].astype(jnp.int32)
    hot = jnp.where((lane == e1) | (lane == e2), 1.0, 0.0).astype(BF16)
    before = jnp.dot(tri_ref[...], hot, preferred_element_type=F32) + lo_vec
    lp1 = jnp.sum(jnp.where(lane == e1, before, 0.0), axis=-1, keepdims=True)
    lp2 = jnp.sum(jnp.where(lane == e2, before, 0.0), axis=-1, keepdims=True)
    return lp1, lp2


def _for_each_unit(fn):
    for u in range(SORT_UNITS):
        fn(u)


def _dispatch_kernel(dst_ref, gap_ref, gapdst_ref,
                     xb_ref, rinfo_ref, lov_ref, tri_ref, xs_ref, pos_ref, sorted_s, zero_s, sem):
    i = pl.program_id(0)
    last = pl.num_programs(0) - 1
    slot = i % 2
    zsem = sem.at[2]

    def unit_copy(tile, buf):
        def make(u):
            dst = pl.multiple_of(dst_ref[tile * SORT_UNITS + u], BF16_ROWS)
            return pltpu.make_async_copy(
                sorted_s.at[buf, pl.ds(pl.multiple_of(u * BF16_ROWS, BF16_ROWS), BF16_ROWS), :],
                xs_ref.at[pl.ds(dst, BF16_ROWS), :], sem.at[buf])
        return make

    def drain(tile, buf):
        _for_each_unit(lambda u: unit_copy(tile, buf)(u).wait())

    def gap(e):
        dst0 = pl.multiple_of(gapdst_ref[e], BF16_ROWS)
        return lambda start, rows: pltpu.make_async_copy(
            zero_s.at[pl.ds(0, rows), :], xs_ref.at[pl.ds(dst0 + start, rows), :], zsem)

    zrows = zero_s.shape[0]

    def tail(r):
        return pltpu.make_async_copy(zero_s, xs_ref.at[pl.ds(pl.multiple_of(r * zrows, zrows), zrows), :], zsem)

    @pl.when(i == 0)
    def _():
        zero_s[...] = jnp.zeros_like(zero_s)
        for e in range(N_EXPERTS):
            _segment_copies(gap_ref[e], GAP_BITS, gap(e), lambda cp: cp.start())
        tail_lo = gap_ref[N_EXPERTS] * (EXPERT_TM // zrows)
        tail_hi = xs_ref.shape[0] // zrows
        lax.fori_loop(tail_lo, tail_hi, lambda r, c: (tail(r).start(), c)[1], 0)
        for e in range(N_EXPERTS):
            _segment_copies(gap_ref[e], GAP_BITS, gap(e), lambda cp: cp.wait())
        lax.fori_loop(tail_lo, tail_hi, lambda r, c: (tail(r).wait(), c)[1], 0)

    rinfo = rinfo_ref[...]
    lp1, lp2 = _sort_positions(rinfo, lov_ref[...], tri_ref)
    lane = lax.broadcasted_iota(jnp.int32, rinfo.shape, 1)
    pos = jnp.where(lane == 0, lp1, jnp.where(lane == 1, lp2, 0.0))
    pos_ref[...] = pos
    pos_t = pos.T
    p_iota = lax.broadcasted_iota(jnp.int32, (SORT_ROWS, SORT_TM), 0)
    perm = (p_iota == pos_t[0:1, :].astype(jnp.int32)) | (p_iota == pos_t[1:2, :].astype(jnp.int32))
    perm = jnp.where(perm, 1.0, 0.0).astype(BF16)
    rec_hi = rinfo.astype(BF16)
    rec_lo = (rinfo - rec_hi.astype(F32)).astype(BF16)
    srt = jnp.dot(perm, jnp.concatenate([xb_ref[...], rec_hi, rec_lo], axis=1),
                  preferred_element_type=F32).astype(BF16)

    @pl.when(i >= 2)
    def _():
        drain(i - 2, slot)

    sorted_s[slot] = srt
    _for_each_unit(lambda u: unit_copy(i, slot)(u).start())

    @pl.when(i == last)
    def _():
        @pl.when(i >= 1)
        def _():
            drain(i - 1, 1 - slot)
        drain(i, slot)


def _dispatch(x1b, rinfo, dst_t, gap_t, gapdst_t, lo_vec, n_rows):
    T, D = x1b.shape
    n_tiles = T // SORT_TM
    tri = (jnp.arange(SORT_TM)[:, None] > jnp.arange(SORT_TM)[None, :]).astype(BF16)
    gap_rows = BF16_ROWS << GAP_BITS[0]
    return pl.pallas_call(
        _dispatch_kernel,
        grid_spec=pltpu.PrefetchScalarGridSpec(
            num_scalar_prefetch=3,
            grid=(n_tiles,),
            in_specs=[
                pl.BlockSpec((SORT_TM, D), lambda i, *_: (i, 0)),
                pl.BlockSpec((SORT_TM, LANES), lambda i, *_: (i, 0)),
                pl.BlockSpec((None, 1, LANES), lambda i, *_: (i, 0, 0)),
                pl.BlockSpec((SORT_TM, SORT_TM), lambda i, *_: (0, 0)),
            ],
            out_specs=[pl.BlockSpec(memory_space=pl.ANY),
                       pl.BlockSpec((SORT_TM, LANES), lambda i, *_: (i, 0))],
            scratch_shapes=[pltpu.VMEM((2, SORT_ROWS, D + 2 * LANES), BF16),
                            pltpu.VMEM((gap_rows, D + 2 * LANES), BF16), pltpu.SemaphoreType.DMA((3,))],
        ),
        out_shape=[jax.ShapeDtypeStruct((n_rows, D + 2 * LANES), BF16),
                   jax.ShapeDtypeStruct((T, LANES), F32)],
        compiler_params=_cparams(("arbitrary",)),
        name="moe_dispatch",
    )(dst_t, gap_t, gapdst_t, x1b, rinfo, lo_vec, tri)


def _expert_kernel(blk_e_ref, used_ref, xs_ref, wg_ref, wu_ref, wd_ref, ys_ref):
    j = pl.program_id(0)

    @pl.when(j < used_ref[0])
    def _():
        xb = xs_ref[:, 0:D_MODEL]
        rec = (xs_ref[:, D_MODEL:D_MODEL + LANES].astype(F32)
               + xs_ref[:, D_MODEL + LANES:D_MODEL + 2 * LANES].astype(F32))
        gate_w = jnp.where(rec[:, 0:1] == blk_e_ref[j].astype(F32), rec[:, 2:3], rec[:, 3:4])
        hg = jnp.dot(xb, wg_ref[...].astype(BF16), preferred_element_type=F32)
        hu = jnp.dot(xb, wu_ref[...].astype(BF16), preferred_element_type=F32)
        hid = hg * jax.nn.sigmoid(hg) * hu
        down = jnp.dot(hid.astype(BF16), wd_ref[...].astype(BF16), preferred_element_type=F32)
        ys_ref[...] = (gate_w * down).astype(ys_ref.dtype)

    @pl.when(j >= used_ref[0])
    def _():
        ys_ref[...] = jnp.zeros_like(ys_ref)


def _experts(xs, blk_e, used, wg, wu, wd):
    n_rows = xs.shape[0]
    D = wg.shape[1]
    return pl.pallas_call(
        _expert_kernel,
        grid_spec=pltpu.PrefetchScalarGridSpec(
            num_scalar_prefetch=2,
            grid=(n_rows // EXPERT_TM,),
            in_specs=[
                pl.BlockSpec((EXPERT_TM, xs.shape[1]), lambda j, be, u: (j, 0)),
                pl.BlockSpec((None, D, EXPERT_FF), lambda j, be, u: (be[j], 0, 0)),
                pl.BlockSpec((None, D, EXPERT_FF), lambda j, be, u: (be[j], 0, 0)),
                pl.BlockSpec((None, EXPERT_FF, D), lambda j, be, u: (be[j], 0, 0)),
            ],
            out_specs=pl.BlockSpec((EXPERT_TM, D), lambda j, be, u: (j, 0)),
        ),
        out_shape=jax.ShapeDtypeStruct((n_rows, D), BF16),
        compiler_params=_cparams(("arbitrary",)),
        name="moe_experts",
    )(blk_e, used, xs, wg, wu, wd)


def _combine_kernel(dst_ref, ys_ref, pos_ref, x1_ref, g2_ref, b2_ref,
                    out_ref, sorted_s, sem, *, alpha):
    i = pl.program_id(0)
    last = pl.num_programs(0) - 1
    slot = i % 2

    def unit_copy(tile, buf):
        def make(u):
            src = pl.multiple_of(dst_ref[tile * SORT_UNITS + u], BF16_ROWS)
            return pltpu.make_async_copy(
                ys_ref.at[pl.ds(src, BF16_ROWS), :],
                sorted_s.at[buf, pl.ds(pl.multiple_of(u * BF16_ROWS, BF16_ROWS), BF16_ROWS), :], sem.at[buf])
        return make

    def fetch(tile, buf):
        _for_each_unit(lambda u: unit_copy(tile, buf)(u).start())

    @pl.when(i == 0)
    def _():
        sorted_s[...] = jnp.zeros_like(sorted_s)
        fetch(0, 0)

    fetch(jnp.minimum(i + 1, last), 1 - slot)

    pos = pos_ref[...]
    p_iota = lax.broadcasted_iota(jnp.int32, (SORT_TM, SORT_ROWS), 1)
    both = (p_iota == pos[:, 0:1].astype(jnp.int32)) | (p_iota == pos[:, 1:2].astype(jnp.int32))
    both = jnp.where(both, 1.0, 0.0).astype(BF16)
    _for_each_unit(lambda u: unit_copy(i, slot)(u).wait())
    y = jnp.dot(both, sorted_s[slot], preferred_element_type=F32)
    out_ref[...] = _layer_norm(alpha * x1_ref[...] + y, g2_ref[...], b2_ref[...])

    @pl.when(i == last)
    def _():
        _for_each_unit(lambda u: unit_copy(last, 1 - slot)(u).wait())


def _combine(ys, pos, x1, src_t, g2, b2, alpha):
    T, D = x1.shape
    kern = functools.partial(_combine_kernel, alpha=alpha)
    return pl.pallas_call(
        kern,
        grid_spec=pltpu.PrefetchScalarGridSpec(
            num_scalar_prefetch=1,
            grid=(T // SORT_TM,),
            in_specs=[
                pl.BlockSpec(memory_space=pl.ANY),
                pl.BlockSpec((SORT_TM, LANES), lambda i, *_: (i, 0)),
                pl.BlockSpec((SORT_TM, D), lambda i, *_: (i, 0)),
                pl.BlockSpec((1, D), lambda i, *_: (0, 0)),
                pl.BlockSpec((1, D), lambda i, *_: (0, 0)),
            ],
            out_specs=pl.BlockSpec((SORT_TM, D), lambda i, *_: (i, 0)),
            scratch_shapes=[pltpu.VMEM((2, SORT_ROWS, D), BF16), pltpu.SemaphoreType.DMA((2,))],
        ),
        out_shape=jax.ShapeDtypeStruct((T, D), F32),
        compiler_params=_cparams(("arbitrary",)),
        name="moe_combine_ln2",
    )(src_t, ys, pos, x1, g2, b2)


def _ceil_to(a, m):
    return (a + m - 1) // m * m


def _moe(x1b, x1, rinfo, cnt, wg, wu, wd, g2, b2, alpha):
    T, D = x1.shape
    n_tiles = T // SORT_TM
    i32 = jnp.int32
    n = cnt[:, 0, :N_EXPERTS].astype(i32)
    npad = _ceil_to(n, BF16_ROWS)
    lo = jnp.cumsum(npad, axis=1) - npad
    tot = npad.sum(axis=0)
    reg = _ceil_to(tot, EXPERT_TM)
    off = jnp.cumsum(reg) - reg
    g = off[None, :] + jnp.cumsum(npad, axis=0) - npad
    n_blocks = T * 2 // EXPERT_TM + n_tiles * N_EXPERTS * (BF16_ROWS - 1) // EXPERT_TM + N_EXPERTS
    spare = n_blocks * EXPERT_TM
    n_rows = spare + 2 * SORT_ROWS
    blk_end = jnp.cumsum(reg) // EXPERT_TM
    j = jnp.arange(n_rows // EXPERT_TM, dtype=i32)
    used = blk_end[-1]
    jj = jnp.minimum(j, used - 1)
    blk_e = jnp.sum(jj[:, None] >= blk_end[None, :], axis=1).astype(i32)
    lo_vec = jnp.pad(lo.astype(F32), ((0, 0), (0, LANES - N_EXPERTS))).reshape(n_tiles, 1, LANES)
    flat = lambda a: a.astype(i32).reshape(-1)
    gap_t = jnp.concatenate([flat((reg - tot) // BF16_ROWS), flat(used)])
    u_row = jnp.arange(SORT_UNITS, dtype=i32) * BF16_ROWS
    u_e = jnp.sum(u_row[None, :, None] >= (lo + npad)[:, None, :], axis=-1)
    live = u_e < N_EXPERTS
    own = u_e[:, :, None] == jnp.arange(N_EXPERTS, dtype=i32)[None, None, :]
    u_g = jnp.sum(jnp.where(own, (g - lo)[:, None, :], 0), axis=-1) + u_row[None, :]
    slot = (jnp.arange(n_tiles, dtype=i32) % 2)[:, None]
    u_dst = jnp.where(live, u_g, spare + slot * SORT_ROWS + u_row[None, :])
    u_src = jnp.where(live, u_g, 0)

    xs, pos = _dispatch(x1b, rinfo, flat(u_dst), gap_t, flat(off + tot), lo_vec, n_rows)
    ys = _experts(xs, blk_e, flat(used), wg, wu, wd)
    return _combine(ys, pos, x1, flat(u_src), g2, b2, alpha)


def _layer(x, mem, cos2, sin2, l, depth, w_in, b_in, conv_w, conv_b, conv_ln_g, conv_ln_b,
           w_mem_kv, w_attn_o, w_conv_o, w_mem_o, w_out, ln1_g, ln1_b,
           w_group_router, b_group_router, w_expert_router, b_expert_router,
           w_exp_gate, w_exp_up, w_exp_down, ln2_g, ln2_b):
    B, S, D = x.shape
    alpha = (2.0 * depth) ** 0.25
    w_in_b = w_in[l].astype(BF16)
    b_in_l = b_in[l].reshape(1, -1)

    cols = [slice(p * ATTN_QK + g * ATTN_OUT, p * ATTN_QK + (g + 1) * ATTN_OUT)
            for g in ATTN_GROUP_IDX for p in range(3)]
    o_attn = _attention(x, cos2, sin2, jnp.concatenate([w_in_b[:, c] for c in cols], axis=1),
                        jnp.concatenate([b_in_l[:, c] for c in cols], axis=1))

    c0 = 3 * ATTN_QK
    c1 = c0 + 2 * CONV_CH
    c2 = c1 + MEM_W
    row = lambda a: a[l].reshape(1, -1)
    c, om = _conv_mem_branches(x, mem, w_in_b[:, c0:c2], b_in_l[:, c0:c2], conv_w[l], row(conv_b),
                               row(conv_ln_g), row(conv_ln_b), w_mem_kv[l].astype(BF16))

    wr = jnp.concatenate([w_group_router[l],
                          w_expert_router[l].transpose(1, 0, 2).reshape(D, N_EXPERTS)], axis=1)
    wr = jnp.pad(wr, ((0, 0), (0, LANES - wr.shape[1])))
    br = jnp.concatenate([b_group_router[l], b_expert_router[l].reshape(-1)])
    br = jnp.pad(br, (0, LANES - br.shape[0])).reshape(1, LANES)

    x1, x1b, rinfo, cnt = _merge(
        x.reshape(B * S, D), o_attn, c, om,
        w_in_b[:, c2:], b_in_l[:, c2:], w_attn_o[l].astype(BF16), w_conv_o[l].astype(BF16),
        w_mem_o[l].astype(BF16), w_out[l].astype(BF16), row(ln1_g), row(ln1_b), wr, br, alpha)

    out = _moe(x1b, x1, rinfo, cnt, w_exp_gate[l], w_exp_up[l], w_exp_down[l],
               row(ln2_g), row(ln2_b), alpha)
    return out.reshape(B, S, D)


def kernel(x, mem, positions, w_in, b_in, conv_w, conv_b, conv_ln_g, conv_ln_b, w_mem_kv, w_attn_o,
           w_conv_o, w_mem_o, w_out, ln1_g, ln1_b, w_group_router, b_group_router, w_expert_router,
           b_expert_router, w_exp_gate, w_exp_up, w_exp_down, ln2_g, ln2_b):
    depth = w_in.shape[0]
    cos2, sin2 = _rope_tables(positions)
    for l in range(depth):
        x = _layer(x, mem, cos2, sin2, l, depth, w_in, b_in, conv_w, conv_b, conv_ln_g, conv_ln_b,
                   w_mem_kv, w_attn_o, w_conv_o, w_mem_o, w_out, ln1_g, ln1_b,
                   w_group_router, b_group_router, w_expert_router, b_expert_router,
                   w_exp_gate, w_exp_up, w_exp_down, ln2_g, ln2_b)
    return x
```

```python
import functools
import math

import jax
import jax.numpy as jnp
from jax import lax
from jax.experimental import pallas as pl
from jax.experimental.pallas import tpu as pltpu

D_MODEL = 1024
HEAD_DIM = 128
ATTN_SLOTS = 4
DIL_PATTERNS = ((128, 1), (512, 4), (2048, 16))
N_DIL = len(DIL_PATTERNS)
ATTN_QK = N_DIL * ATTN_SLOTS * HEAD_DIM
ATTN_OUT = ATTN_SLOTS * HEAD_DIM
BAND_BLOCK = 128
ROPE_THETA = 10000.0
CONV_CH = 512
CONV_WIDTH = 31
MEM_HEADS = 4
MEM_HEAD_DIM = 128
MEM_W = MEM_HEADS * MEM_HEAD_DIM
N_BRANCH = 3
N_GROUPS = 4
EXPERTS_PER_GROUP = 4
N_EXPERTS = N_GROUPS * EXPERTS_PER_GROUP
EXPERT_FF = 512
LN_EPS = 1e-5

LANES = 128
SUBLANES = 8
BF16_ROWS = 16
BLOCKS_PER_ITER = 2
ATTN_GROUP_IDX = (2, 0, 1)
ATTN_GROUP_ORDER = tuple(DIL_PATTERNS[i][1] for i in ATTN_GROUP_IDX)
MERGE_TM = 1024
MERGE_TAIL_PARTS = 1
SORT_TM = 512
SORT_ROWS = 2 * SORT_TM + N_EXPERTS * BF16_ROWS
EXPERT_TM = 512
CONV_PAD = 32
VMEM_LIMIT = 56 * 1024 * 1024

F32 = jnp.float32
BF16 = jnp.bfloat16
NEG = -1e30


def _cparams(sem):
    return pltpu.CompilerParams(dimension_semantics=sem, vmem_limit_bytes=VMEM_LIMIT)


def _full(shape):
    n = len(shape)
    return pl.BlockSpec(shape, lambda *_: (0,) * n)


def _once(shape):
    n = len(shape)
    return pl.BlockSpec(shape, lambda *_: (0,) * n, pipeline_mode=pl.Buffered(1))


def _rope_kernel(pos_lo_ref, pos_hi_ref, inv_ref, cos_ref, sin_ref):
    hs = pos_lo_ref.shape[0]
    half = HEAD_DIM // 2
    lane = lax.broadcasted_iota(jnp.int32, (hs, HEAD_DIM), 1)
    first = lane < half
    pos = jnp.where(first, pos_lo_ref[...], pos_hi_ref[...]).astype(F32)
    ang = pos * inv_ref[...]
    c = jnp.cos(ang)
    s = jnp.sin(ang)
    c_sw = pltpu.roll(c, half, 1)
    s_sw = pltpu.roll(s, half, 1)
    cos_ref[0:hs, :] = jnp.where(first, c, c_sw)
    cos_ref[hs:2 * hs, :] = jnp.where(first, c_sw, c)
    sin_ref[0:hs, :] = jnp.where(first, -s, s_sw)
    sin_ref[hs:2 * hs, :] = jnp.where(first, -s_sw, s)


def _rope_tables(positions):
    B, S = positions.shape
    half = HEAD_DIM // 2
    inv = ROPE_THETA ** (-jnp.arange(half, dtype=F32) / half)
    inv = jnp.concatenate([inv, inv]).reshape(1, HEAD_DIM)
    pos3 = positions.reshape(B, S, 1)
    return pl.pallas_call(
        _rope_kernel,
        grid=(B,),
        in_specs=[pl.BlockSpec((None, S // 2, 1), lambda b: (b, 0, 0)),
                  pl.BlockSpec((None, S // 2, 1), lambda b: (b, 1, 0)), _full((1, HEAD_DIM))],
        out_specs=[pl.BlockSpec((None, S, HEAD_DIM), lambda b: (b, 0, 0))] * 2,
        out_shape=[jax.ShapeDtypeStruct((B, S, HEAD_DIM), F32)] * 2,
        compiler_params=_cparams(("parallel",)),
        name="rope_tables",
    )(pos3, pos3, inv)


def _attn_kernel(x_ref, cos_ref, sin_ref, w_ref, b_ref, o_ref, q_s, k_s, v_s, acc_s, m_s, l_s, *, S, rc):
    g = pl.program_id(1)
    scale = HEAD_DIM ** -0.5
    slabs = (q_s, k_s, v_s)

    def project(i):
        r0 = pl.multiple_of(i * rc, rc)
        xs = x_ref[pl.ds(r0, rc), :].astype(BF16)
        qkv = jnp.dot(xs, w_ref[...], preferred_element_type=F32) + b_ref[...]
        c = cos_ref[pl.ds(r0, rc), :]
        s = sin_ref[pl.ds(r0, rc), :]
        out = []
        for h in range(ATTN_SLOTS):
            qh = qkv[:, h * HEAD_DIM:(h + 1) * HEAD_DIM]
            out.append((qh * c + pltpu.roll(qh, HEAD_DIM // 2, 1) * s) * scale)
        for h in range(ATTN_SLOTS):
            kh = qkv[:, ATTN_OUT + h * HEAD_DIM:ATTN_OUT + (h + 1) * HEAD_DIM]
            out.append(kh * c + pltpu.roll(kh, HEAD_DIM // 2, 1) * s)
        for h in range(ATTN_SLOTS):
            out.append(qkv[:, 2 * ATTN_OUT + h * HEAD_DIM:2 * ATTN_OUT + (h + 1) * HEAD_DIM])
        return r0, out

    def proj_token_order(i, carry):
        r0, parts = project(i)
        for k, part in enumerate(parts):
            slabs[k // ATTN_SLOTS][k % ATTN_SLOTS, pl.ds(r0, rc), :] = part
        return carry

    def proj_residue_order(r):
        pc = r * BF16_ROWS
        run = pc // r
        t_i = lax.broadcasted_iota(jnp.int32, (pc, pc), 1)
        p_i = lax.broadcasted_iota(jnp.int32, (pc, pc), 0)
        sort = jnp.where(p_i == (t_i % r) * run + t_i // r, 1.0, 0.0).astype(BF16)

        def body(i, carry):
            _, parts = project(i)
            rows = jnp.concatenate(parts, axis=1).astype(BF16)
            for sub in range(rc // pc):
                chunk = i * (rc // pc) + sub
                srt = jnp.dot(sort, rows[sub * pc:(sub + 1) * pc, :], preferred_element_type=F32)
                for res in range(r):
                    dst = pl.ds(pl.multiple_of(res * BAND_BLOCK + chunk * run, run), run)
                    for k in range(3 * ATTN_SLOTS):
                        slabs[k // ATTN_SLOTS][k % ATTN_SLOTS, dst, :] = (
                            srt[res * run:(res + 1) * run, k * HEAD_DIM:(k + 1) * HEAD_DIM])
            return carry

        return body

    qi = lax.broadcasted_iota(jnp.int32, (BAND_BLOCK, 2 * BAND_BLOCK), 0)
    kj = lax.broadcasted_iota(jnp.int32, (BAND_BLOCK, 2 * BAND_BLOCK), 1)
    dist = BAND_BLOCK + qi - kj
    band = (dist >= 0) & (dist <= BAND_BLOCK)
    ones = jnp.ones((2 * BAND_BLOCK, HEAD_DIM), BF16)

    def attend(r):
        span = BAND_BLOCK * r

        def blk(i, carry):
            loaded = []
            for u in range(BLOCKS_PER_ITER):
                c = i * BLOCKS_PER_ITER + u
                res = c % r
                n = c // r
                start = n * span + res
                prev = jnp.maximum(n - 1, 0) * span + res
                valid = band & ((kj >= BAND_BLOCK) | (n > 0))
                cur_rows = pl.ds(start, BAND_BLOCK, stride=r)
                prev_rows = pl.ds(prev, BAND_BLOCK, stride=r)
                for h in range(ATTN_SLOTS):
                    q = q_s[h, cur_rows, :].astype(BF16)
                    kk = jnp.concatenate([k_s[h, prev_rows, :], k_s[h, cur_rows, :]], axis=0).astype(BF16)
                    vv = jnp.concatenate([v_s[h, prev_rows, :], v_s[h, cur_rows, :]], axis=0).astype(BF16)
                    loaded.append((h, cur_rows, valid, q, kk, vv,
                                   acc_s[h, cur_rows, :], m_s[h, cur_rows, :], l_s[h, cur_rows, :]))
            scores = [lax.dot_general(item[3], item[4], (((1,), (1,)), ((), ())), preferred_element_type=F32)
                      for item in loaded]
            probs = []
            for (_, _, valid, _, _, _, _, m_old, _), s in zip(loaded, scores):
                s = jnp.where(valid, s, NEG)
                m_blk = jnp.max(jnp.maximum(s[:, :BAND_BLOCK], s[:, BAND_BLOCK:]), axis=-1, keepdims=True)
                m_new = jnp.maximum(m_old, m_blk)
                probs.append((m_new, jnp.exp(m_old - m_new),
                              jnp.exp((s - jnp.concatenate([m_new, m_new], axis=1)).astype(BF16))))
            updated = []
            for (h, cur_rows, _, _, _, vv, acc_old, _, l_old), (m_new, a, p) in zip(loaded, probs):
                pv = jnp.dot(p, jnp.concatenate([vv, ones], axis=1), preferred_element_type=F32)
                updated.append((h, cur_rows, a * acc_old + pv[:, :HEAD_DIM], m_new,
                                a * l_old + pv[:, HEAD_DIM:]))
            for h, cur_rows, acc_new, m_new, l_new in updated:
                acc_s[h, cur_rows, :] = acc_new
                m_s[h, cur_rows, :] = m_new
                l_s[h, cur_rows, :] = l_new
            return carry

        lax.fori_loop(0, S // rc, proj_token_order, 0)
        lax.fori_loop(0, S // (BAND_BLOCK * BLOCKS_PER_ITER), blk, 0)

    def attend_first(r):
        causal = (lax.broadcasted_iota(jnp.int32, (BAND_BLOCK, BAND_BLOCK), 0)
                  >= lax.broadcasted_iota(jnp.int32, (BAND_BLOCK, BAND_BLOCK), 1))
        ones_blk = jnp.ones((BAND_BLOCK, HEAD_DIM), BF16)

        def blk(i, carry):
            loaded = []
            for u in range(BLOCKS_PER_ITER):
                res = i * BLOCKS_PER_ITER + u
                rows = pl.ds(pl.multiple_of(res * BAND_BLOCK, BAND_BLOCK), BAND_BLOCK)
                tokens = pl.ds(res, BAND_BLOCK, stride=r)
                for h in range(ATTN_SLOTS):
                    loaded.append((h, tokens, q_s[h, rows, :].astype(BF16), k_s[h, rows, :].astype(BF16),
                                   v_s[h, rows, :].astype(BF16)))
            scores = [lax.dot_general(q, kk, (((1,), (1,)), ((), ())), preferred_element_type=F32)
                      for _, _, q, kk, _ in loaded]
            probs = []
            for s in scores:
                s = jnp.where(causal, s, NEG)
                m_new = jnp.broadcast_to(jnp.max(s, axis=-1, keepdims=True), s.shape)
                probs.append((m_new, jnp.exp((s - m_new).astype(BF16))))
            updated = []
            for (h, tokens, _, _, vv), (m_new, p) in zip(loaded, probs):
                pv = jnp.dot(p, jnp.concatenate([vv, ones_blk], axis=1), preferred_element_type=F32)
                updated.append((h, tokens, pv[:, :HEAD_DIM], m_new, pv[:, HEAD_DIM:]))
            for h, tokens, acc_new, m_new, l_new in updated:
                acc_s[h, tokens, :] = acc_new
                m_s[h, tokens, :] = m_new
                l_s[h, tokens, :] = l_new
            return carry

        lax.fori_loop(0, S // rc, proj_residue_order(r), 0)
        lax.fori_loop(0, r // BLOCKS_PER_ITER, blk, 0)

    assert S // ATTN_GROUP_ORDER[0] == BAND_BLOCK
    for gi, dilation in enumerate(ATTN_GROUP_ORDER):
        pl.when(g == gi)(functools.partial(attend_first if gi == 0 else attend, dilation))

    @pl.when(g == N_DIL - 1)
    def _():
        def fin(i, carry):
            r0 = pl.multiple_of(i * rc, rc)
            for h in range(ATTN_SLOTS):
                o = acc_s[h, pl.ds(r0, rc), :] / l_s[h, pl.ds(r0, rc), :]
                o_ref[pl.ds(r0, rc), h * HEAD_DIM:(h + 1) * HEAD_DIM] = o.astype(o_ref.dtype)
            return carry

        lax.fori_loop(0, S // rc, fin, 0)


def _attention(x, cos2, sin2, w, b):
    B, S, D = x.shape
    rc = 512
    gw = 3 * ATTN_OUT
    slab = pltpu.VMEM((ATTN_SLOTS, S, HEAD_DIM), F32)
    kern = functools.partial(_attn_kernel, S=S, rc=rc)
    o = pl.pallas_call(
        kern,
        grid=(B, N_DIL),
        in_specs=[
            pl.BlockSpec((None, S, D), lambda bb, g: (bb, 0, 0)),
            pl.BlockSpec((None, S, HEAD_DIM), lambda bb, g: (bb, 0, 0), pipeline_mode=pl.Buffered(1)),
            pl.BlockSpec((None, S, HEAD_DIM), lambda bb, g: (bb, 0, 0), pipeline_mode=pl.Buffered(1)),
            pl.BlockSpec((D, gw), lambda bb, g: (0, g)),
            pl.BlockSpec((1, gw), lambda bb, g: (0, g)),
        ],
        out_specs=pl.BlockSpec((None, S, ATTN_OUT), lambda bb, g: (bb, 0, 0)),
        out_shape=jax.ShapeDtypeStruct((B, S, ATTN_OUT), BF16),
        scratch_shapes=[slab] * 6,
        compiler_params=_cparams(("parallel", "arbitrary")),
        name="dilated_attn",
    )(x, cos2, sin2, w, b)
    return o.reshape(B * S, ATTN_OUT)


def _conv_mem_kernel(x_ref, mem_ref, w_ref, b_ref, cw_ref, cb_ref, g_ref, beta_ref, wkv_ref,
                     out_ref, om_ref, cpad, shift_s, q_s, *, S, rp, rc):
    cpad[0:CONV_PAD, :] = jnp.zeros((CONV_PAD, CONV_CH), F32)
    kv = jnp.dot(mem_ref[...].astype(BF16), wkv_ref[...], preferred_element_type=F32)
    km = kv[:, :MEM_W].astype(BF16)
    vm = kv[:, MEM_W:].astype(BF16)
    scale = MEM_HEAD_DIM ** -0.5

    def glu(i, carry):
        r0 = pl.multiple_of(i * rp, rp)
        xs = x_ref[pl.ds(r0, rp), :].astype(BF16)
        u = jnp.dot(xs, w_ref[...], preferred_element_type=F32) + b_ref[...]
        cpad[pl.ds(CONV_PAD + r0, rp), :] = u[:, :CONV_CH] * jax.nn.sigmoid(u[:, CONV_CH:2 * CONV_CH])
        q_s[pl.ds(r0, rp), :] = (u[:, 2 * CONV_CH:] * scale).astype(BF16)
        return carry

    lax.fori_loop(0, S // rp, glu, 0)

    off = CONV_PAD - (CONV_WIDTH - 1)
    srows = shift_s.shape[1]

    def conv(i, carry):
        r0 = pl.multiple_of(i * rc, rc)
        win = cpad[pl.ds(r0, rc + CONV_PAD), :]
        for ph in range(1, SUBLANES):
            shift_s[ph - 1, :, :] = win[ph:ph + srows, :]
        acc = jnp.zeros((rc, CONV_CH), F32) + cb_ref[...]
        for j in range(CONV_WIDTH):
            a, ph = divmod(j + off, SUBLANES)
            if ph == 0:
                tap = win[a * SUBLANES:a * SUBLANES + rc, :]
            else:
                tap = shift_s[ph - 1, a * SUBLANES:a * SUBLANES + rc, :]
            acc = acc + tap * cw_ref[j:j + 1, :]
        mu = jnp.mean(acc, axis=-1, keepdims=True)
        d = acc - mu
        var = jnp.mean(d * d, axis=-1, keepdims=True)
        y = d * lax.rsqrt(var + LN_EPS) * g_ref[...] + beta_ref[...]
        out_ref[pl.ds(r0, rc), :] = (y * jax.nn.sigmoid(y)).astype(out_ref.dtype)
        q = q_s[pl.ds(r0, rc), :]
        for h in range(MEM_HEADS):
            lo = h * MEM_HEAD_DIM
            s = lax.dot_general(q[:, lo:lo + MEM_HEAD_DIM], km[:, lo:lo + MEM_HEAD_DIM],
                                (((1,), (1,)), ((), ())), preferred_element_type=F32)
            m = jnp.max(s, axis=-1, keepdims=True)
            p = jnp.exp(s - m)
            l = jnp.sum(p, axis=-1, keepdims=True)
            o = jnp.dot(p.astype(BF16), vm[:, lo:lo + MEM_HEAD_DIM], preferred_element_type=F32) / l
            om_ref[pl.ds(r0, rc), lo:lo + MEM_HEAD_DIM] = o.astype(om_ref.dtype)
        return carry

    lax.fori_loop(0, S // rc, conv, 0)


def _conv_mem_branches(x, mem, w_proj, b_proj, conv_w, conv_b, ln_g, ln_b, w_kv):
    B, S, D = x.shape
    M = mem.shape[1]
    rp, rc = 512, 256
    pw = 2 * CONV_CH + MEM_W
    cw = jnp.pad(conv_w, ((0, CONV_PAD - CONV_WIDTH), (0, 0)))
    kern = functools.partial(_conv_mem_kernel, S=S, rp=rp, rc=rc)
    c, om = pl.pallas_call(
        kern,
        grid=(B,),
        in_specs=[
            pl.BlockSpec((None, S, D), lambda b: (b, 0, 0)),
            pl.BlockSpec((None, M, D), lambda b: (b, 0, 0)),
            _full((D, pw)), _full((1, pw)),
            _full((CONV_PAD, CONV_CH)), _full((1, CONV_CH)), _full((1, CONV_CH)), _full((1, CONV_CH)),
            _full((D, 2 * MEM_W)),
        ],
        out_specs=[pl.BlockSpec((None, S, CONV_CH), lambda b: (b, 0, 0)),
                   pl.BlockSpec((None, S, MEM_W), lambda b: (b, 0, 0))],
        out_shape=[jax.ShapeDtypeStruct((B, S, CONV_CH), BF16), jax.ShapeDtypeStruct((B, S, MEM_W), BF16)],
        scratch_shapes=[pltpu.VMEM((CONV_PAD + S, CONV_CH), F32),
                        pltpu.VMEM((SUBLANES - 1, rc + CONV_PAD - SUBLANES, CONV_CH), F32),
                        pltpu.VMEM((S, MEM_W), BF16)],
        compiler_params=_cparams(("parallel",)),
        name="conv_and_memory_attn",
    )(x, mem, w_proj, b_proj, cw, conv_b, ln_g, ln_b, w_kv)
    return c.reshape(B * S, CONV_CH), om.reshape(B * S, MEM_W)


def _layer_norm(z, g, b):
    mu = jnp.mean(z, axis=-1, keepdims=True)
    d = z - mu
    var = jnp.mean(d * d, axis=-1, keepdims=True)
    return d * lax.rsqrt(var + LN_EPS) * g + b


def _merge_kernel(x_ref, a_ref, c_ref, m_ref,
                  wg_ref, bg_ref, wa_ref, wc_ref, wm_ref, wo_ref, g1_ref, b1_ref, wr_ref, br_ref,
                  x1_ref, x1b_ref, rinfo_ref, cnt_ref, *, alpha):
    x = x_ref[...]
    xb = x.astype(BF16)

    def gate(k):
        lo = k * D_MODEL
        return jax.nn.sigmoid(jnp.dot(xb, wg_ref[:, lo:lo + D_MODEL], preferred_element_type=F32)
                              + bg_ref[:, lo:lo + D_MODEL])

    merged = gate(0) * jnp.dot(a_ref[...], wa_ref[...], preferred_element_type=F32)
    merged = merged + gate(1) * jnp.dot(c_ref[...], wc_ref[...], preferred_element_type=F32)
    merged = merged + gate(2) * jnp.dot(m_ref[...], wm_ref[...], preferred_element_type=F32)
    merged = merged.astype(BF16)
    wr = wr_ref[...]
    wr_hi = wr.astype(BF16)
    wr_lo = (wr - wr_hi.astype(F32)).astype(BF16)
    wr_cat = jnp.concatenate([wr_hi, wr_lo], axis=1)
    part_rows = x.shape[0] // MERGE_TAIL_PARTS
    hot = jnp.concatenate([
        _merge_tail(x[p * part_rows:(p + 1) * part_rows], merged[p * part_rows:(p + 1) * part_rows],
                    slice(p * part_rows, (p + 1) * part_rows), wo_ref, g1_ref, b1_ref, wr_cat, wr_hi, br_ref,
                    x1_ref, x1b_ref, rinfo_ref, alpha)
        for p in range(MERGE_TAIL_PARTS)], axis=0)
    for k in range(cnt_ref.shape[0]):
        part = jnp.sum(hot[k * SORT_TM:(k + 1) * SORT_TM], axis=0, keepdims=True)
        cnt_ref[k] = jnp.broadcast_to(part, cnt_ref.shape[1:])


def _merge_tail(x, merged, rows, wo_ref, g1_ref, b1_ref, wr_cat, wr_hi, br_ref,
                x1_ref, x1b_ref, rinfo_ref, alpha):
    mix = jnp.dot(merged, wo_ref[...], preferred_element_type=F32)
    x1 = _layer_norm(alpha * x + mix, g1_ref[...], b1_ref[...])
    x1_ref[rows, :] = x1
    x1_hi = x1.astype(BF16)
    x1b_ref[rows, :] = x1_hi
    x1_lo = (x1 - x1_hi.astype(F32)).astype(BF16)
    t = jnp.dot(x1_hi, wr_cat, preferred_element_type=F32)
    logits = (t[:, :LANES] + t[:, LANES:]
              + jnp.dot(x1_lo, wr_hi, preferred_element_type=F32) + br_ref[...])
    lane = lax.broadcasted_iota(jnp.int32, logits.shape, 1)
    is_g = lane < N_GROUPS
    gl = jnp.where(is_g, logits, NEG)
    gmax = jnp.max(gl, axis=-1, keepdims=True)
    g_sel = jnp.min(jnp.where(is_g & (gl == gmax), lane, LANES), axis=-1, keepdims=True)
    g_prob = 1.0 / jnp.sum(jnp.where(is_g, jnp.exp(gl - gmax), 0.0), axis=-1, keepdims=True)
    e_lo = N_GROUPS + g_sel * EXPERTS_PER_GROUP
    in_grp = (lane >= e_lo) & (lane < e_lo + EXPERTS_PER_GROUP)
    el = jnp.where(in_grp, logits, NEG)
    v1 = jnp.max(el, axis=-1, keepdims=True)
    i1 = jnp.min(jnp.where(in_grp & (el == v1), lane, LANES), axis=-1, keepdims=True)
    rest = in_grp & (lane != i1)
    el2 = jnp.where(rest, logits, NEG)
    v2 = jnp.max(el2, axis=-1, keepdims=True)
    i2 = jnp.min(jnp.where(rest & (el2 == v2), lane, LANES), axis=-1, keepdims=True)
    t = jnp.exp(v2 - v1)
    w1 = g_prob / (1.0 + t)
    w2 = g_prob * t / (1.0 + t)
    e1 = (i1 - N_GROUPS).astype(F32)
    e2 = (i2 - N_GROUPS).astype(F32)
    rinfo_ref[rows, :] = jnp.where(lane == 0, e1, jnp.where(lane == 1, e2, jnp.where(
        lane == 2, w1, jnp.where(lane == 3, w2, 0.0))))
    sel = lane + N_GROUPS
    return jnp.where((sel == i1) | (sel == i2), 1.0, 0.0)


def _merge(x2, a, c, om, wg, bg, wa, wc, wm, wo, g1, b1, wr, br, alpha, tm=MERGE_TM):
    T, D = x2.shape
    row = lambda w: pl.BlockSpec((tm, w), lambda i: (i, 0))
    kern = functools.partial(_merge_kernel, alpha=alpha)
    return pl.pallas_call(
        kern,
        grid=(T // tm,),
        in_specs=[row(D), row(ATTN_OUT), row(CONV_CH), row(MEM_W),
                  _once((D, N_BRANCH * D)), _once((1, N_BRANCH * D)),
                  _once((ATTN_OUT, D)), _once((CONV_CH, D)), _once((MEM_W, D)), _once((D, D)),
                  _once((1, D)), _once((1, D)), _once((D, LANES)), _once((1, LANES))],
        out_specs=[row(D), row(D), row(LANES),
                   pl.BlockSpec((tm // SORT_TM, SUBLANES, LANES), lambda i: (i, 0, 0))],
        out_shape=[jax.ShapeDtypeStruct((T, D), F32), jax.ShapeDtypeStruct((T, D), BF16),
                   jax.ShapeDtypeStruct((T, LANES), F32),
                   jax.ShapeDtypeStruct((T // SORT_TM, SUBLANES, LANES), F32)],
        compiler_params=_cparams(("parallel",)),
        name="merge_ln1_router",
    )(x2, a, c, om, wg, bg, wa, wc, wm, wo, g1, b1, wr, br)


SORT_UNITS = SORT_ROWS // BF16_ROWS
GAP_BITS = tuple(range((EXPERT_TM // BF16_ROWS - 1).bit_length() - 1, -1, -1))


def _segment_copies(units, bits, make_copy, act):
    for b in bits:
        v = 1 << b

        @pl.when((units & v) != 0)
        def _():
            start = pl.multiple_of((units & (-2 * v)) * BF16_ROWS, BF16_ROWS)
            act(make_copy(start, v * BF16_ROWS))


def _sort_positions(rinfo, lo_vec, tri_ref):
    lane = lax.broadcasted_iota(jnp.int32, rinfo.shape, 1)
    e1 = rinfo[:, 0:1].astype(jnp.int32)
    e2 = rinfo[:, 1:2].astype(jnp.int32)
    hot = jnp.where((lane == e1) | (lane == e2), 1.0, 0.0).astype(BF16)
    before = jnp.dot(tri_ref[...], hot, preferred_element_type=F32) + lo_vec
    lp1 = jnp.sum(jnp.where(lane == e1, before, 0.0), axis=-1, keepdims=True)
    lp2 = jnp.sum(jnp.where(lane == e2, before, 0.0), axis=-1, keepdims=True)
    return lp1, lp2


def _for_each_unit(fn):
    for u in range(SORT_UNITS):
        fn(u)


def _dispatch_kernel(dst_ref, gap_ref, gapdst_ref,
                     xb_ref, rinfo_ref, lov_ref, tri_ref, xs_ref, pos_ref, sorted_s, zero_s, sem):
    i = pl.program_id(0)
    last = pl.num_programs(0) - 1
    slot = i % 2
    zsem = sem.at[2]

    def unit_copy(tile, buf):
        def make(u):
            dst = pl.multiple_of(dst_ref[tile * SORT_UNITS + u], BF16_ROWS)
            return pltpu.make_async_copy(
                sorted_s.at[buf, pl.ds(pl.multiple_of(u * BF16_ROWS, BF16_ROWS), BF16_ROWS), :],
                xs_ref.at[pl.ds(dst, BF16_ROWS), :], sem.at[buf])
        return make

    def drain(tile, buf):
        _for_each_unit(lambda u: unit_copy(tile, buf)(u).wait())

    def gap(e):
        dst0 = pl.multiple_of(gapdst_ref[e], BF16_ROWS)
        return lambda start, rows: pltpu.make_async_copy(
            zero_s.at[pl.ds(0, rows), :], xs_ref.at[pl.ds(dst0 + start, rows), :], zsem)

    zrows = zero_s.shape[0]

    def tail(r):
        return pltpu.make_async_copy(zero_s, xs_ref.at[pl.ds(pl.multiple_of(r * zrows, zrows), zrows), :], zsem)

    @pl.when(i == 0)
    def _():
        zero_s[...] = jnp.zeros_like(zero_s)
        for e in range(N_EXPERTS):
            _segment_copies(gap_ref[e], GAP_BITS, gap(e), lambda cp: cp.start())
        tail_lo = gap_ref[N_EXPERTS] * (EXPERT_TM // zrows)
        tail_hi = xs_ref.shape[0] // zrows
        lax.fori_loop(tail_lo, tail_hi, lambda r, c: (tail(r).start(), c)[1], 0)
        for e in range(N_EXPERTS):
            _segment_copies(gap_ref[e], GAP_BITS, gap(e), lambda cp: cp.wait())
        lax.fori_loop(tail_lo, tail_hi, lambda r, c: (tail(r).wait(), c)[1], 0)

    rinfo = rinfo_ref[...]
    lp1, lp2 = _sort_positions(rinfo, lov_ref[...], tri_ref)
    lane = lax.broadcasted_iota(jnp.int32, rinfo.shape, 1)
    pos = jnp.where(lane == 0, lp1, jnp.where(lane == 1, lp2, 0.0))
    pos_ref[...] = pos
    pos_t = pos.T
    p_iota = lax.broadcasted_iota(jnp.int32, (SORT_ROWS, SORT_TM), 0)
    perm = (p_iota == pos_t[0:1, :].astype(jnp.int32)) | (p_iota == pos_t[1:2, :].astype(jnp.int32))
    perm = jnp.where(perm, 1.0, 0.0).astype(BF16)
    srt = jnp.dot(perm, xb_ref[...], preferred_element_type=F32).astype(BF16)

    @pl.when(i >= 2)
    def _():
        drain(i - 2, slot)

    sorted_s[slot] = srt
    _for_each_unit(lambda u: unit_copy(i, slot)(u).start())

    @pl.when(i == last)
    def _():
        @pl.when(i >= 1)
        def _():
            drain(i - 1, 1 - slot)
        drain(i, slot)


def _dispatch(x1b, rinfo, dst_t, gap_t, gapdst_t, lo_vec, n_rows):
    T, D = x1b.shape
    n_tiles = T // SORT_TM
    tri = (jnp.arange(SORT_TM)[:, None] > jnp.arange(SORT_TM)[None, :]).astype(BF16)
    gap_rows = BF16_ROWS << GAP_BITS[0]
    return pl.pallas_call(
        _dispatch_kernel,
        grid_spec=pltpu.PrefetchScalarGridSpec(
            num_scalar_prefetch=3,
            grid=(n_tiles,),
            in_specs=[
                pl.BlockSpec((SORT_TM, D), lambda i, *_: (i, 0)),
                pl.BlockSpec((SORT_TM, LANES), lambda i, *_: (i, 0)),
                pl.BlockSpec((None, 1, LANES), lambda i, *_: (i, 0, 0)),
                pl.BlockSpec((SORT_TM, SORT_TM), lambda i, *_: (0, 0)),
            ],
            out_specs=[pl.BlockSpec(memory_space=pl.ANY),
                       pl.BlockSpec((SORT_TM, LANES), lambda i, *_: (i, 0))],
            scratch_shapes=[pltpu.VMEM((2, SORT_ROWS, D), BF16), pltpu.VMEM((gap_rows, D), BF16),
                            pltpu.SemaphoreType.DMA((3,))],
        ),
        out_shape=[jax.ShapeDtypeStruct((n_rows, D), BF16), jax.ShapeDtypeStruct((T, LANES), F32)],
        compiler_params=_cparams(("arbitrary",)),
        name="moe_dispatch",
    )(dst_t, gap_t, gapdst_t, x1b, rinfo, lo_vec, tri)


def _expert_kernel(blk_e_ref, used_ref, xs_ref, wg_ref, wu_ref, wd_ref, ys_ref):
    j = pl.program_id(0)

    @pl.when(j < used_ref[0])
    def _():
        xb = xs_ref[...]
        hg = jnp.dot(xb, wg_ref[...].astype(BF16), preferred_element_type=F32)
        hu = jnp.dot(xb, wu_ref[...].astype(BF16), preferred_element_type=F32)
        hid = hg * jax.nn.sigmoid(hg) * hu
        ys_ref[...] = jnp.dot(hid.astype(BF16), wd_ref[...].astype(BF16),
                              preferred_element_type=F32).astype(ys_ref.dtype)

    @pl.when(j >= used_ref[0])
    def _():
        ys_ref[...] = jnp.zeros_like(ys_ref)


def _experts(xs, blk_e, used, wg, wu, wd):
    n_rows = xs.shape[0]
    D = wg.shape[1]
    return pl.pallas_call(
        _expert_kernel,
        grid_spec=pltpu.PrefetchScalarGridSpec(
            num_scalar_prefetch=2,
            grid=(n_rows // EXPERT_TM,),
            in_specs=[
                pl.BlockSpec((EXPERT_TM, xs.shape[1]), lambda j, be, u: (j, 0)),
                pl.BlockSpec((None, D, EXPERT_FF), lambda j, be, u: (be[j], 0, 0)),
                pl.BlockSpec((None, D, EXPERT_FF), lambda j, be, u: (be[j], 0, 0)),
                pl.BlockSpec((None, EXPERT_FF, D), lambda j, be, u: (be[j], 0, 0)),
            ],
            out_specs=pl.BlockSpec((EXPERT_TM, D), lambda j, be, u: (j, 0)),
        ),
        out_shape=jax.ShapeDtypeStruct((n_rows, D), BF16),
        compiler_params=_cparams(("arbitrary",)),
        name="moe_experts",
    )(blk_e, used, xs, wg, wu, wd)


def _combine_kernel(dst_ref, ys_ref, rinfo_ref, pos_ref, x1_ref, g2_ref, b2_ref,
                    out_ref, sorted_s, sem, *, alpha):
    i = pl.program_id(0)
    last = pl.num_programs(0) - 1
    slot = i % 2

    def unit_copy(tile, buf):
        def make(u):
            src = pl.multiple_of(dst_ref[tile * SORT_UNITS + u], BF16_ROWS)
            return pltpu.make_async_copy(
                ys_ref.at[pl.ds(src, BF16_ROWS), :],
                sorted_s.at[buf, pl.ds(pl.multiple_of(u * BF16_ROWS, BF16_ROWS), BF16_ROWS), :], sem.at[buf])
        return make

    def fetch(tile, buf):
        _for_each_unit(lambda u: unit_copy(tile, buf)(u).start())

    @pl.when(i == 0)
    def _():
        sorted_s[...] = jnp.zeros_like(sorted_s)
        fetch(0, 0)

    fetch(jnp.minimum(i + 1, last), 1 - slot)

    rinfo = rinfo_ref[...]
    pos = pos_ref[...]
    p_iota = lax.broadcasted_iota(jnp.int32, (SORT_TM, SORT_ROWS), 1)
    gather = (jnp.where(p_iota == pos[:, 0:1].astype(jnp.int32), rinfo[:, 2:3], 0.0)
              + jnp.where(p_iota == pos[:, 1:2].astype(jnp.int32), rinfo[:, 3:4], 0.0)).astype(BF16)
    _for_each_unit(lambda u: unit_copy(i, slot)(u).wait())
    y = jnp.dot(gather, sorted_s[slot], preferred_element_type=F32)
    out_ref[...] = _layer_norm(alpha * x1_ref[...] + y, g2_ref[...], b2_ref[...])

    @pl.when(i == last)
    def _():
        _for_each_unit(lambda u: unit_copy(last, 1 - slot)(u).wait())


def _combine(ys, rinfo, pos, x1, src_t, g2, b2, alpha):
    T, D = x1.shape
    kern = functools.partial(_combine_kernel, alpha=alpha)
    return pl.pallas_call(
        kern,
        grid_spec=pltpu.PrefetchScalarGridSpec(
            num_scalar_prefetch=1,
            grid=(T // SORT_TM,),
            in_specs=[
                pl.BlockSpec(memory_space=pl.ANY),
                pl.BlockSpec((SORT_TM, LANES), lambda i, *_: (i, 0)),
                pl.BlockSpec((SORT_TM, LANES), lambda i, *_: (i, 0)),
                pl.BlockSpec((SORT_TM, D), lambda i, *_: (i, 0)),
                pl.BlockSpec((1, D), lambda i, *_: (0, 0)),
                pl.BlockSpec((1, D), lambda i, *_: (0, 0)),
            ],
            out_specs=pl.BlockSpec((SORT_TM, D), lambda i, *_: (i, 0)),
            scratch_shapes=[pltpu.VMEM((2, SORT_ROWS, D), BF16), pltpu.SemaphoreType.DMA((2,))],
        ),
        out_shape=jax.ShapeDtypeStruct((T, D), F32),
        compiler_params=_cparams(("arbitrary",)),
        name="moe_combine_ln2",
    )(src_t, ys, rinfo, pos, x1, g2, b2)


def _ceil_to(a, m):
    return (a + m - 1) // m * m


def _moe(x1b, x1, rinfo, cnt, wg, wu, wd, g2, b2, alpha):
    T, D = x1.shape
    n_tiles = T // SORT_TM
    i32 = jnp.int32
    n = cnt[:, 0, :N_EXPERTS].astype(i32)
    npad = _ceil_to(n, BF16_ROWS)
    lo = jnp.cumsum(npad, axis=1) - npad
    tot = npad.sum(axis=0)
    reg = _ceil_to(tot, EXPERT_TM)
    off = jnp.cumsum(reg) - reg
    g = off[None, :] + jnp.cumsum(npad, axis=0) - npad
    n_blocks = T * 2 // EXPERT_TM + n_tiles * N_EXPERTS * (BF16_ROWS - 1) // EXPERT_TM + N_EXPERTS
    spare = n_blocks * EXPERT_TM
    n_rows = spare + 2 * SORT_ROWS
    blk_end = jnp.cumsum(reg) // EXPERT_TM
    j = jnp.arange(n_rows // EXPERT_TM, dtype=i32)
    used = blk_end[-1]
    jj = jnp.minimum(j, used - 1)
    blk_e = jnp.sum(jj[:, None] >= blk_end[None, :], axis=1).astype(i32)
    lo_vec = jnp.pad(lo.astype(F32), ((0, 0), (0, LANES - N_EXPERTS))).reshape(n_tiles, 1, LANES)
    flat = lambda a: a.astype(i32).reshape(-1)
    gap_t = jnp.concatenate([flat((reg - tot) // BF16_ROWS), flat(used)])
    u_row = jnp.arange(SORT_UNITS, dtype=i32) * BF16_ROWS
    u_e = jnp.sum(u_row[None, :, None] >= (lo + npad)[:, None, :], axis=-1)
    live = u_e < N_EXPERTS
    own = u_e[:, :, None] == jnp.arange(N_EXPERTS, dtype=i32)[None, None, :]
    u_g = jnp.sum(jnp.where(own, (g - lo)[:, None, :], 0), axis=-1) + u_row[None, :]
    slot = (jnp.arange(n_tiles, dtype=i32) % 2)[:, None]
    u_dst = jnp.where(live, u_g, spare + slot * SORT_ROWS + u_row[None, :])
    u_src = jnp.where(live, u_g, 0)

    xs, pos = _dispatch(x1b, rinfo, flat(u_dst), gap_t, flat(off + tot), lo_vec, n_rows)
    ys = _experts(xs, blk_e, flat(used), wg, wu, wd)
    return _combine(ys, rinfo, pos, x1, flat(u_src), g2, b2, alpha)


def _layer(x, mem, cos2, sin2, l, depth, w_in, b_in, conv_w, conv_b, conv_ln_g, conv_ln_b,
           w_mem_kv, w_attn_o, w_conv_o, w_mem_o, w_out, ln1_g, ln1_b,
           w_group_router, b_group_router, w_expert_router, b_expert_router,
           w_exp_gate, w_exp_up, w_exp_down, ln2_g, ln2_b):
    B, S, D = x.shape
    alpha = (2.0 * depth) ** 0.25
    w_in_b = w_in[l].astype(BF16)
    b_in_l = b_in[l].reshape(1, -1)

    cols = [slice(p * ATTN_QK + g * ATTN_OUT, p * ATTN_QK + (g + 1) * ATTN_OUT)
            for g in ATTN_GROUP_IDX for p in range(3)]
    o_attn = _attention(x, cos2, sin2, jnp.concatenate([w_in_b[:, c] for c in cols], axis=1),
                        jnp.concatenate([b_in_l[:, c] for c in cols], axis=1))

    c0 = 3 * ATTN_QK
    c1 = c0 + 2 * CONV_CH
    c2 = c1 + MEM_W
    row = lambda a: a[l].reshape(1, -1)
    c, om = _conv_mem_branches(x, mem, w_in_b[:, c0:c2], b_in_l[:, c0:c2], conv_w[l], row(conv_b),
                               row(conv_ln_g), row(conv_ln_b), w_mem_kv[l].astype(BF16))

    wr = jnp.concatenate([w_group_router[l],
                          w_expert_router[l].transpose(1, 0, 2).reshape(D, N_EXPERTS)], axis=1)
    wr = jnp.pad(wr, ((0, 0), (0, LANES - wr.shape[1])))
    br = jnp.concatenate([b_group_router[l], b_expert_router[l].reshape(-1)])
    br = jnp.pad(br, (0, LANES - br.shape[0])).reshape(1, LANES)

    x1, x1b, rinfo, cnt = _merge(
        x.reshape(B * S, D), o_attn, c, om,
        w_in_b[:, c2:], b_in_l[:, c2:], w_attn_o[l].astype(BF16), w_conv_o[l].astype(BF16),
        w_mem_o[l].astype(BF16), w_out[l].astype(BF16), row(ln1_g), row(ln1_b), wr, br, alpha)

    out = _moe(x1b, x1, rinfo, cnt, w_exp_gate[l], w_exp_up[l], w_exp_down[l],
               row(ln2_g), row(ln2_b), alpha)
    return out.reshape(B, S, D)


def kernel(x, mem, positions, w_in, b_in, conv_w, conv_b, conv_ln_g, conv_ln_b, w_mem_kv, w_attn_o,
           w_conv_o, w_mem_o, w_out, ln1_g, ln1_b, w_group_router, b_group_router, w_expert_router,
           b_expert_router, w_exp_gate, w_exp_up, w_exp_down, ln2_g, ln2_b):
    depth = w_in.shape[0]
    cos2, sin2 = _rope_tables(positions)
    for l in range(depth):
        x = _layer(x, mem, cos2, sin2, l, depth, w_in, b_in, conv_w, conv_b, conv_ln_g, conv_ln_b,
                   w_mem_kv, w_attn_o, w_conv_o, w_mem_o, w_out, ln1_g, ln1_b,
                   w_group_router, b_group_router, w_expert_router, b_expert_router,
                   w_exp_gate, w_exp_up, w_exp_down, ln2_g, ln2_b)
    return x
```

```python
import functools
import math

import jax
import jax.numpy as jnp
from jax import lax
from jax.experimental import pallas as pl
from jax.experimental.pallas import tpu as pltpu

D_MODEL = 1024
HEAD_DIM = 128
ATTN_SLOTS = 4
DIL_PATTERNS = ((128, 1), (512, 4), (2048, 16))
N_DIL = len(DIL_PATTERNS)
ATTN_QK = N_DIL * ATTN_SLOTS * HEAD_DIM
ATTN_OUT = ATTN_SLOTS * HEAD_DIM
BAND_BLOCK = 128
ROPE_THETA = 10000.0
CONV_CH = 512
CONV_WIDTH = 31
MEM_HEADS = 4
MEM_HEAD_DIM = 128
MEM_W = MEM_HEADS * MEM_HEAD_DIM
N_BRANCH = 3
N_GROUPS = 4
EXPERTS_PER_GROUP = 4
N_EXPERTS = N_GROUPS * EXPERTS_PER_GROUP
EXPERT_FF = 512
LN_EPS = 1e-5

LANES = 128
SUBLANES = 8
BF16_ROWS = 16
BLOCKS_PER_ITER = 2
FIRST_GROUP_BLOCKS_PER_ITER = 4
ATTN_GROUP_IDX = (2, 0, 1)
ATTN_GROUP_ORDER = tuple(DIL_PATTERNS[i][1] for i in ATTN_GROUP_IDX)
MERGE_TM = 1024
MERGE_TAIL_PARTS = 1
SORT_TM = 512
SORT_ROWS = 2 * SORT_TM + N_EXPERTS * BF16_ROWS
EXPERT_TM = 512
CONV_PAD = 32
VMEM_LIMIT = 56 * 1024 * 1024

F32 = jnp.float32
BF16 = jnp.bfloat16
NEG = -1e30


def _cparams(sem):
    return pltpu.CompilerParams(dimension_semantics=sem, vmem_limit_bytes=VMEM_LIMIT)


def _full(shape):
    n = len(shape)
    return pl.BlockSpec(shape, lambda *_: (0,) * n)


def _once(shape):
    n = len(shape)
    return pl.BlockSpec(shape, lambda *_: (0,) * n, pipeline_mode=pl.Buffered(1))


def _rope_kernel(pos_lo_ref, pos_hi_ref, inv_ref, cos_ref, sin_ref):
    hs = pos_lo_ref.shape[0]
    half = HEAD_DIM // 2
    lane = lax.broadcasted_iota(jnp.int32, (hs, HEAD_DIM), 1)
    first = lane < half
    pos = jnp.where(first, pos_lo_ref[...], pos_hi_ref[...]).astype(F32)
    ang = pos * inv_ref[...]
    c = jnp.cos(ang)
    s = jnp.sin(ang)
    c_sw = pltpu.roll(c, half, 1)
    s_sw = pltpu.roll(s, half, 1)
    cos_ref[0:hs, :] = jnp.where(first, c, c_sw)
    cos_ref[hs:2 * hs, :] = jnp.where(first, c_sw, c)
    sin_ref[0:hs, :] = jnp.where(first, -s, s_sw)
    sin_ref[hs:2 * hs, :] = jnp.where(first, -s_sw, s)


def _rope_tables(positions):
    B, S = positions.shape
    half = HEAD_DIM // 2
    inv = ROPE_THETA ** (-jnp.arange(half, dtype=F32) / half)
    inv = jnp.concatenate([inv, inv]).reshape(1, HEAD_DIM)
    pos3 = positions.reshape(B, S, 1)
    return pl.pallas_call(
        _rope_kernel,
        grid=(B,),
        in_specs=[pl.BlockSpec((None, S // 2, 1), lambda b: (b, 0, 0)),
                  pl.BlockSpec((None, S // 2, 1), lambda b: (b, 1, 0)), _full((1, HEAD_DIM))],
        out_specs=[pl.BlockSpec((None, S, HEAD_DIM), lambda b: (b, 0, 0))] * 2,
        out_shape=[jax.ShapeDtypeStruct((B, S, HEAD_DIM), F32)] * 2,
        compiler_params=_cparams(("parallel",)),
        name="rope_tables",
    )(pos3, pos3, inv)


def _attn_kernel(x_ref, cos_ref, sin_ref, w_ref, b_ref, o_ref, q_s, k_s, v_s, acc_s, m_s, l_s, *, S, rc):
    g = pl.program_id(1)
    scale = HEAD_DIM ** -0.5
    slabs = (q_s, k_s, v_s)

    def project(i):
        r0 = pl.multiple_of(i * rc, rc)
        xs = x_ref[pl.ds(r0, rc), :].astype(BF16)
        qkv = jnp.dot(xs, w_ref[...], preferred_element_type=F32) + b_ref[...]
        c = cos_ref[pl.ds(r0, rc), :]
        s = sin_ref[pl.ds(r0, rc), :]
        out = []
        for h in range(ATTN_SLOTS):
            qh = qkv[:, h * HEAD_DIM:(h + 1) * HEAD_DIM]
            out.append((qh * c + pltpu.roll(qh, HEAD_DIM // 2, 1) * s) * scale)
        for h in range(ATTN_SLOTS):
            kh = qkv[:, ATTN_OUT + h * HEAD_DIM:ATTN_OUT + (h + 1) * HEAD_DIM]
            out.append(kh * c + pltpu.roll(kh, HEAD_DIM // 2, 1) * s)
        for h in range(ATTN_SLOTS):
            out.append(qkv[:, 2 * ATTN_OUT + h * HEAD_DIM:2 * ATTN_OUT + (h + 1) * HEAD_DIM])
        return r0, out

    def proj_token_order(i, carry):
        r0, parts = project(i)
        for k, part in enumerate(parts):
            slabs[k // ATTN_SLOTS][k % ATTN_SLOTS, pl.ds(r0, rc), :] = part
        return carry

    def proj_residue_order(r):
        pc = r * BF16_ROWS
        run = pc // r
        t_i = lax.broadcasted_iota(jnp.int32, (pc, pc), 1)
        p_i = lax.broadcasted_iota(jnp.int32, (pc, pc), 0)
        sort = jnp.where(p_i == (t_i % r) * run + t_i // r, 1.0, 0.0).astype(BF16)

        def body(i, carry):
            _, parts = project(i)
            rows = jnp.concatenate(parts, axis=1).astype(BF16)
            for sub in range(rc // pc):
                chunk = i * (rc // pc) + sub
                srt = jnp.dot(sort, rows[sub * pc:(sub + 1) * pc, :], preferred_element_type=F32)
                for res in range(r):
                    dst = pl.ds(pl.multiple_of(res * BAND_BLOCK + chunk * run, run), run)
                    for k in range(3 * ATTN_SLOTS):
                        slabs[k // ATTN_SLOTS][k % ATTN_SLOTS, dst, :] = (
                            srt[res * run:(res + 1) * run, k * HEAD_DIM:(k + 1) * HEAD_DIM])
            return carry

        return body

    qi = lax.broadcasted_iota(jnp.int32, (BAND_BLOCK, 2 * BAND_BLOCK), 0)
    kj = lax.broadcasted_iota(jnp.int32, (BAND_BLOCK, 2 * BAND_BLOCK), 1)
    dist = BAND_BLOCK + qi - kj
    band = (dist >= 0) & (dist <= BAND_BLOCK)
    ones = jnp.ones((2 * BAND_BLOCK, HEAD_DIM), BF16)

    def attend(r):
        span = BAND_BLOCK * r

        def blk(i, carry):
            loaded = []
            for u in range(BLOCKS_PER_ITER):
                c = i * BLOCKS_PER_ITER + u
                res = c % r
                n = c // r
                start = n * span + res
                prev = jnp.maximum(n - 1, 0) * span + res
                valid = band & ((kj >= BAND_BLOCK) | (n > 0))
                cur_rows = pl.ds(start, BAND_BLOCK, stride=r)
                prev_rows = pl.ds(prev, BAND_BLOCK, stride=r)
                for h in range(ATTN_SLOTS):
                    q = q_s[h, cur_rows, :].astype(BF16)
                    kk = jnp.concatenate([k_s[h, prev_rows, :], k_s[h, cur_rows, :]], axis=0).astype(BF16)
                    vv = jnp.concatenate([v_s[h, prev_rows, :], v_s[h, cur_rows, :]], axis=0).astype(BF16)
                    loaded.append((h, cur_rows, valid, q, kk, vv,
                                   acc_s[h, cur_rows, :], m_s[h, cur_rows, :], l_s[h, cur_rows, :]))
            scores = [lax.dot_general(item[3], item[4], (((1,), (1,)), ((), ())), preferred_element_type=F32)
                      for item in loaded]
            probs = []
            for (_, _, valid, _, _, _, _, m_old, _), s in zip(loaded, scores):
                s = jnp.where(valid, s, NEG)
                m_blk = jnp.max(jnp.maximum(s[:, :BAND_BLOCK], s[:, BAND_BLOCK:]), axis=-1, keepdims=True)
                m_new = jnp.maximum(m_old, m_blk)
                probs.append((m_new, jnp.exp(m_old - m_new),
                              jnp.exp((s - jnp.concatenate([m_new, m_new], axis=1)).astype(BF16))))
            updated = []
            for (h, cur_rows, _, _, _, vv, acc_old, _, l_old), (m_new, a, p) in zip(loaded, probs):
                pv = jnp.dot(p, jnp.concatenate([vv, ones], axis=1), preferred_element_type=F32)
                updated.append((h, cur_rows, a * acc_old + pv[:, :HEAD_DIM], m_new,
                                a * l_old + pv[:, HEAD_DIM:]))
            for h, cur_rows, acc_new, m_new, l_new in updated:
                acc_s[h, cur_rows, :] = acc_new
                m_s[h, cur_rows, :] = m_new
                l_s[h, cur_rows, :] = l_new
            return carry

        lax.fori_loop(0, S // rc, proj_token_order, 0)
        lax.fori_loop(0, S // (BAND_BLOCK * BLOCKS_PER_ITER), blk, 0)

    def attend_first(r):
        causal = (lax.broadcasted_iota(jnp.int32, (BAND_BLOCK, BAND_BLOCK), 0)
                  >= lax.broadcasted_iota(jnp.int32, (BAND_BLOCK, BAND_BLOCK), 1))
        ones_blk = jnp.ones((BAND_BLOCK, HEAD_DIM), BF16)

        def blk(i, carry):
            loaded = []
            for u in range(FIRST_GROUP_BLOCKS_PER_ITER):
                res = i * FIRST_GROUP_BLOCKS_PER_ITER + u
                rows = pl.ds(pl.multiple_of(res * BAND_BLOCK, BAND_BLOCK), BAND_BLOCK)
                tokens = pl.ds(res, BAND_BLOCK, stride=r)
                for h in range(ATTN_SLOTS):
                    loaded.append((h, tokens, q_s[h, rows, :].astype(BF16), k_s[h, rows, :].astype(BF16),
                                   v_s[h, rows, :].astype(BF16)))
            scores = [lax.dot_general(q, kk, (((1,), (1,)), ((), ())), preferred_element_type=F32)
                      for _, _, q, kk, _ in loaded]
            probs = []
            for s in scores:
                s = jnp.where(causal, s, NEG)
                m_new = jnp.broadcast_to(jnp.max(s, axis=-1, keepdims=True), s.shape)
                probs.append((m_new, jnp.exp((s - m_new).astype(BF16))))
            updated = []
            for (h, tokens, _, _, vv), (m_new, p) in zip(loaded, probs):
                pv = jnp.dot(p, jnp.concatenate([vv, ones_blk], axis=1), preferred_element_type=F32)
                updated.append((h, tokens, pv[:, :HEAD_DIM], m_new, pv[:, HEAD_DIM:]))
            for h, tokens, acc_new, m_new, l_new in updated:
                acc_s[h, tokens, :] = acc_new
                m_s[h, tokens, :] = m_new
                l_s[h, tokens, :] = l_new
            return carry

        lax.fori_loop(0, S // rc, proj_residue_order(r), 0)
        lax.fori_loop(0, r // FIRST_GROUP_BLOCKS_PER_ITER, blk, 0)

    assert S // ATTN_GROUP_ORDER[0] == BAND_BLOCK
    for gi, dilation in enumerate(ATTN_GROUP_ORDER):
        pl.when(g == gi)(functools.partial(attend_first if gi == 0 else attend, dilation))

    @pl.when(g == N_DIL - 1)
    def _():
        def fin(i, carry):
            r0 = pl.multiple_of(i * rc, rc)
            for h in range(ATTN_SLOTS):
                o = acc_s[h, pl.ds(r0, rc), :] / l_s[h, pl.ds(r0, rc), :]
                o_ref[pl.ds(r0, rc), h * HEAD_DIM:(h + 1) * HEAD_DIM] = o.astype(o_ref.dtype)
            return carry

        lax.fori_loop(0, S // rc, fin, 0)


def _attention(x, cos2, sin2, w, b):
    B, S, D = x.shape
    rc = 512
    gw = 3 * ATTN_OUT
    slab = pltpu.VMEM((ATTN_SLOTS, S, HEAD_DIM), F32)
    kern = functools.partial(_attn_kernel, S=S, rc=rc)
    o = pl.pallas_call(
        kern,
        grid=(B, N_DIL),
        in_specs=[
            pl.BlockSpec((None, S, D), lambda bb, g: (bb, 0, 0)),
            pl.BlockSpec((None, S, HEAD_DIM), lambda bb, g: (bb, 0, 0), pipeline_mode=pl.Buffered(1)),
            pl.BlockSpec((None, S, HEAD_DIM), lambda bb, g: (bb, 0, 0), pipeline_mode=pl.Buffered(1)),
            pl.BlockSpec((D, gw), lambda bb, g: (0, g)),
            pl.BlockSpec((1, gw), lambda bb, g: (0, g)),
        ],
        out_specs=pl.BlockSpec((None, S, ATTN_OUT), lambda bb, g: (bb, 0, 0)),
        out_shape=jax.ShapeDtypeStruct((B, S, ATTN_OUT), BF16),
        scratch_shapes=[slab] * 6,
        compiler_params=_cparams(("parallel", "arbitrary")),
        name="dilated_attn",
    )(x, cos2, sin2, w, b)
    return o.reshape(B * S, ATTN_OUT)


def _conv_mem_kernel(x_ref, mem_ref, w_ref, b_ref, cw_ref, cb_ref, g_ref, beta_ref, wkv_ref,
                     out_ref, om_ref, cpad, shift_s, q_s, *, S, rp, rc):
    cpad[0:CONV_PAD, :] = jnp.zeros((CONV_PAD, CONV_CH), F32)
    kv = jnp.dot(mem_ref[...].astype(BF16), wkv_ref[...], preferred_element_type=F32)
    km = kv[:, :MEM_W].astype(BF16)
    vm = kv[:, MEM_W:].astype(BF16)
    scale = MEM_HEAD_DIM ** -0.5

    def glu(i, carry):
        r0 = pl.multiple_of(i * rp, rp)
        xs = x_ref[pl.ds(r0, rp), :].astype(BF16)
        u = jnp.dot(xs, w_ref[...], preferred_element_type=F32) + b_ref[...]
        cpad[pl.ds(CONV_PAD + r0, rp), :] = u[:, :CONV_CH] * jax.nn.sigmoid(u[:, CONV_CH:2 * CONV_CH])
        q_s[pl.ds(r0, rp), :] = (u[:, 2 * CONV_CH:] * scale).astype(BF16)
        return carry

    lax.fori_loop(0, S // rp, glu, 0)

    off = CONV_PAD - (CONV_WIDTH - 1)
    srows = shift_s.shape[1]

    def conv(i, carry):
        r0 = pl.multiple_of(i * rc, rc)
        win = cpad[pl.ds(r0, rc + CONV_PAD), :]
        for ph in range(1, SUBLANES):
            shift_s[ph - 1, :, :] = win[ph:ph + srows, :]
        acc = jnp.zeros((rc, CONV_CH), F32) + cb_ref[...]
        for j in range(CONV_WIDTH):
            a, ph = divmod(j + off, SUBLANES)
            if ph == 0:
                tap = win[a * SUBLANES:a * SUBLANES + rc, :]
            else:
                tap = shift_s[ph - 1, a * SUBLANES:a * SUBLANES + rc, :]
            acc = acc + tap * cw_ref[j:j + 1, :]
        mu = jnp.mean(acc, axis=-1, keepdims=True)
        d = acc - mu
        var = jnp.mean(d * d, axis=-1, keepdims=True)
        y = d * lax.rsqrt(var + LN_EPS) * g_ref[...] + beta_ref[...]
        out_ref[pl.ds(r0, rc), :] = (y * jax.nn.sigmoid(y)).astype(out_ref.dtype)
        q = q_s[pl.ds(r0, rc), :]
        for h in range(MEM_HEADS):
            lo = h * MEM_HEAD_DIM
            s = lax.dot_general(q[:, lo:lo + MEM_HEAD_DIM], km[:, lo:lo + MEM_HEAD_DIM],
                                (((1,), (1,)), ((), ())), preferred_element_type=F32)
            m = jnp.max(s, axis=-1, keepdims=True)
            p = jnp.exp(s - m)
            l = jnp.sum(p, axis=-1, keepdims=True)
            o = jnp.dot(p.astype(BF16), vm[:, lo:lo + MEM_HEAD_DIM], preferred_element_type=F32) / l
            om_ref[pl.ds(r0, rc), lo:lo + MEM_HEAD_DIM] = o.astype(om_ref.dtype)
        return carry

    lax.fori_loop(0, S // rc, conv, 0)


def _conv_mem_branches(x, mem, w_proj, b_proj, conv_w, conv_b, ln_g, ln_b, w_kv):
    B, S, D = x.shape
    M = mem.shape[1]
    rp, rc = 512, 256
    pw = 2 * CONV_CH + MEM_W
    cw = jnp.pad(conv_w, ((0, CONV_PAD - CONV_WIDTH), (0, 0)))
    kern = functools.partial(_conv_mem_kernel, S=S, rp=rp, rc=rc)
    c, om = pl.pallas_call(
        kern,
        grid=(B,),
        in_specs=[
            pl.BlockSpec((None, S, D), lambda b: (b, 0, 0)),
            pl.BlockSpec((None, M, D), lambda b: (b, 0, 0)),
            _full((D, pw)), _full((1, pw)),
            _full((CONV_PAD, CONV_CH)), _full((1, CONV_CH)), _full((1, CONV_CH)), _full((1, CONV_CH)),
            _full((D, 2 * MEM_W)),
        ],
        out_specs=[pl.BlockSpec((None, S, CONV_CH), lambda b: (b, 0, 0)),
                   pl.BlockSpec((None, S, MEM_W), lambda b: (b, 0, 0))],
        out_shape=[jax.ShapeDtypeStruct((B, S, CONV_CH), BF16), jax.ShapeDtypeStruct((B, S, MEM_W), BF16)],
        scratch_shapes=[pltpu.VMEM((CONV_PAD + S, CONV_CH), F32),
                        pltpu.VMEM((SUBLANES - 1, rc + CONV_PAD - SUBLANES, CONV_CH), F32),
                        pltpu.VMEM((S, MEM_W), BF16)],
        compiler_params=_cparams(("parallel",)),
        name="conv_and_memory_attn",
    )(x, mem, w_proj, b_proj, cw, conv_b, ln_g, ln_b, w_kv)
    return c.reshape(B * S, CONV_CH), om.reshape(B * S, MEM_W)


def _layer_norm(z, g, b):
    mu = jnp.mean(z, axis=-1, keepdims=True)
    d = z - mu
    var = jnp.mean(d * d, axis=-1, keepdims=True)
    return d * lax.rsqrt(var + LN_EPS) * g + b


def _merge_kernel(x_ref, a_ref, c_ref, m_ref,
                  wg_ref, bg_ref, wa_ref, wc_ref, wm_ref, wo_ref, g1_ref, b1_ref, wr_ref, br_ref,
                  x1_ref, x1b_ref, rinfo_ref, cnt_ref, *, alpha):
    x = x_ref[...]
    xb = x.astype(BF16)

    def gate(k):
        lo = k * D_MODEL
        return jax.nn.sigmoid(jnp.dot(xb, wg_ref[:, lo:lo + D_MODEL], preferred_element_type=F32)
                              + bg_ref[:, lo:lo + D_MODEL])

    merged = gate(0) * jnp.dot(a_ref[...], wa_ref[...], preferred_element_type=F32)
    merged = merged + gate(1) * jnp.dot(c_ref[...], wc_ref[...], preferred_element_type=F32)
    merged = merged + gate(2) * jnp.dot(m_ref[...], wm_ref[...], preferred_element_type=F32)
    merged = merged.astype(BF16)
    wr = wr_ref[...]
    wr_hi = wr.astype(BF16)
    wr_lo = (wr - wr_hi.astype(F32)).astype(BF16)
    wr_cat = jnp.concatenate([wr_hi, wr_lo], axis=1)
    part_rows = x.shape[0] // MERGE_TAIL_PARTS
    hot = jnp.concatenate([
        _merge_tail(x[p * part_rows:(p + 1) * part_rows], merged[p * part_rows:(p + 1) * part_rows],
                    slice(p * part_rows, (p + 1) * part_rows), wo_ref, g1_ref, b1_ref, wr_cat, wr_hi, br_ref,
                    x1_ref, x1b_ref, rinfo_ref, alpha)
        for p in range(MERGE_TAIL_PARTS)], axis=0)
    for k in range(cnt_ref.shape[0]):
        part = jnp.sum(hot[k * SORT_TM:(k + 1) * SORT_TM], axis=0, keepdims=True)
        cnt_ref[k] = jnp.broadcast_to(part, cnt_ref.shape[1:])


def _merge_tail(x, merged, rows, wo_ref, g1_ref, b1_ref, wr_cat, wr_hi, br_ref,
                x1_ref, x1b_ref, rinfo_ref, alpha):
    mix = jnp.dot(merged, wo_ref[...], preferred_element_type=F32)
    x1 = _layer_norm(alpha * x + mix, g1_ref[...], b1_ref[...])
    x1_ref[rows, :] = x1
    x1_hi = x1.astype(BF16)
    x1b_ref[rows, :] = x1_hi
    x1_lo = (x1 - x1_hi.astype(F32)).astype(BF16)
    t = jnp.dot(x1_hi, wr_cat, preferred_element_type=F32)
    logits = (t[:, :LANES] + t[:, LANES:]
              + jnp.dot(x1_lo, wr_hi, preferred_element_type=F32) + br_ref[...])
    lane = lax.broadcasted_iota(jnp.int32, logits.shape, 1)
    is_g = lane < N_GROUPS
    gl = jnp.where(is_g, logits, NEG)
    gmax = jnp.max(gl, axis=-1, keepdims=True)
    g_sel = jnp.min(jnp.where(is_g & (gl == gmax), lane, LANES), axis=-1, keepdims=True)
    g_prob = 1.0 / jnp.sum(jnp.where(is_g, jnp.exp(gl - gmax), 0.0), axis=-1, keepdims=True)
    e_lo = N_GROUPS + g_sel * EXPERTS_PER_GROUP
    in_grp = (lane >= e_lo) & (lane < e_lo + EXPERTS_PER_GROUP)
    el = jnp.where(in_grp, logits, NEG)
    v1 = jnp.max(el, axis=-1, keepdims=True)
    i1 = jnp.min(jnp.where(in_grp & (el == v1), lane, LANES), axis=-1, keepdims=True)
    rest = in_grp & (lane != i1)
    el2 = jnp.where(rest, logits, NEG)
    v2 = jnp.max(el2, axis=-1, keepdims=True)
    i2 = jnp.min(jnp.where(rest & (el2 == v2), lane, LANES), axis=-1, keepdims=True)
    t = jnp.exp(v2 - v1)
    w1 = g_prob / (1.0 + t)
    w2 = g_prob * t / (1.0 + t)
    e1 = (i1 - N_GROUPS).astype(F32)
    e2 = (i2 - N_GROUPS).astype(F32)
    rinfo_ref[rows, :] = jnp.where(lane == 0, e1, jnp.where(lane == 1, e2, jnp.where(
        lane == 2, w1, jnp.where(lane == 3, w2, 0.0))))
    sel = lane + N_GROUPS
    return jnp.where((sel == i1) | (sel == i2), 1.0, 0.0)


def _merge(x2, a, c, om, wg, bg, wa, wc, wm, wo, g1, b1, wr, br, alpha, tm=MERGE_TM):
    T, D = x2.shape
    row = lambda w: pl.BlockSpec((tm, w), lambda i: (i, 0))
    kern = functools.partial(_merge_kernel, alpha=alpha)
    return pl.pallas_call(
        kern,
        grid=(T // tm,),
        in_specs=[row(D), row(ATTN_OUT), row(CONV_CH), row(MEM_W),
                  _once((D, N_BRANCH * D)), _once((1, N_BRANCH * D)),
                  _once((ATTN_OUT, D)), _once((CONV_CH, D)), _once((MEM_W, D)), _once((D, D)),
                  _once((1, D)), _once((1, D)), _once((D, LANES)), _once((1, LANES))],
        out_specs=[row(D), row(D), row(LANES),
                   pl.BlockSpec((tm // SORT_TM, SUBLANES, LANES), lambda i: (i, 0, 0))],
        out_shape=[jax.ShapeDtypeStruct((T, D), F32), jax.ShapeDtypeStruct((T, D), BF16),
                   jax.ShapeDtypeStruct((T, LANES), F32),
                   jax.ShapeDtypeStruct((T // SORT_TM, SUBLANES, LANES), F32)],
        compiler_params=_cparams(("parallel",)),
        name="merge_ln1_router",
    )(x2, a, c, om, wg, bg, wa, wc, wm, wo, g1, b1, wr, br)


SORT_UNITS = SORT_ROWS // BF16_ROWS
GAP_BITS = tuple(range((EXPERT_TM // BF16_ROWS - 1).bit_length() - 1, -1, -1))


def _segment_copies(units, bits, make_copy, act):
    for b in bits:
        v = 1 << b

        @pl.when((units & v) != 0)
        def _():
            start = pl.multiple_of((units & (-2 * v)) * BF16_ROWS, BF16_ROWS)
            act(make_copy(start, v * BF16_ROWS))


def _sort_positions(rinfo, lo_vec, tri_ref):
    lane = lax.broadcasted_iota(jnp.int32, rinfo.shape, 1)
    e1 = rinfo[:, 0:1].astype(jnp.int32)
    e2 = rinfo[:, 1:2].astype(jnp.int32)
    hot = jnp.where((lane == e1) | (lane == e2), 1.0, 0.0).astype(BF16)
    before = jnp.dot(tri_ref[...], hot, preferred_element_type=F32) + lo_vec
    lp1 = jnp.sum(jnp.where(lane == e1, before, 0.0), axis=-1, keepdims=True)
    lp2 = jnp.sum(jnp.where(lane == e2, before, 0.0), axis=-1, keepdims=True)
    return lp1, lp2


def _for_each_unit(fn):
    for u in range(SORT_UNITS):
        fn(u)


def _dispatch_kernel(dst_ref, gap_ref, gapdst_ref,
                     xb_ref, rinfo_ref, lov_ref, tri_ref, xs_ref, pos_ref, sorted_s, zero_s, sem):
    i = pl.program_id(0)
    last = pl.num_programs(0) - 1
    slot = i % 2
    zsem = sem.at[2]

    def unit_copy(tile, buf):
        def make(u):
            dst = pl.multiple_of(dst_ref[tile * SORT_UNITS + u], BF16_ROWS)
            return pltpu.make_async_copy(
                sorted_s.at[buf, pl.ds(pl.multiple_of(u * BF16_ROWS, BF16_ROWS), BF16_ROWS), :],
                xs_ref.at[pl.ds(dst, BF16_ROWS), :], sem.at[buf])
        return make

    def drain(tile, buf):
        _for_each_unit(lambda u: unit_copy(tile, buf)(u).wait())

    def gap(e):
        dst0 = pl.multiple_of(gapdst_ref[e], BF16_ROWS)
        return lambda start, rows: pltpu.make_async_copy(
            zero_s.at[pl.ds(0, rows), :], xs_ref.at[pl.ds(dst0 + start, rows), :], zsem)

    zrows = zero_s.shape[0]

    def tail(r):
        return pltpu.make_async_copy(zero_s, xs_ref.at[pl.ds(pl.multiple_of(r * zrows, zrows), zrows), :], zsem)

    @pl.when(i == 0)
    def _():
        zero_s[...] = jnp.zeros_like(zero_s)
        for e in range(N_EXPERTS):
            _segment_copies(gap_ref[e], GAP_BITS, gap(e), lambda cp: cp.start())
        tail_lo = gap_ref[N_EXPERTS] * (EXPERT_TM // zrows)
        tail_hi = xs_ref.shape[0] // zrows
        lax.fori_loop(tail_lo, tail_hi, lambda r, c: (tail(r).start(), c)[1], 0)
        for e in range(N_EXPERTS):
            _segment_copies(gap_ref[e], GAP_BITS, gap(e), lambda cp: cp.wait())
        lax.fori_loop(tail_lo, tail_hi, lambda r, c: (tail(r).wait(), c)[1], 0)

    rinfo = rinfo_ref[...]
    lp1, lp2 = _sort_positions(rinfo, lov_ref[...], tri_ref)
    lane = lax.broadcasted_iota(jnp.int32, rinfo.shape, 1)
    pos = jnp.where(lane == 0, lp1, jnp.where(lane == 1, lp2, 0.0))
    pos_ref[...] = pos
    pos_t = pos.T
    p_iota = lax.broadcasted_iota(jnp.int32, (SORT_ROWS, SORT_TM), 0)
    perm = (p_iota == pos_t[0:1, :].astype(jnp.int32)) | (p_iota == pos_t[1:2, :].astype(jnp.int32))
    perm = jnp.where(perm, 1.0, 0.0).astype(BF16)
    srt = jnp.dot(perm, xb_ref[...], preferred_element_type=F32).astype(BF16)

    @pl.when(i >= 2)
    def _():
        drain(i - 2, slot)

    sorted_s[slot] = srt
    _for_each_unit(lambda u: unit_copy(i, slot)(u).start())

    @pl.when(i == last)
    def _():
        @pl.when(i >= 1)
        def _():
            drain(i - 1, 1 - slot)
        drain(i, slot)


def _dispatch(x1b, rinfo, dst_t, gap_t, gapdst_t, lo_vec, n_rows):
    T, D = x1b.shape
    n_tiles = T // SORT_TM
    tri = (jnp.arange(SORT_TM)[:, None] > jnp.arange(SORT_TM)[None, :]).astype(BF16)
    gap_rows = BF16_ROWS << GAP_BITS[0]
    return pl.pallas_call(
        _dispatch_kernel,
        grid_spec=pltpu.PrefetchScalarGridSpec(
            num_scalar_prefetch=3,
            grid=(n_tiles,),
            in_specs=[
                pl.BlockSpec((SORT_TM, D), lambda i, *_: (i, 0)),
                pl.BlockSpec((SORT_TM, LANES), lambda i, *_: (i, 0)),
                pl.BlockSpec((None, 1, LANES), lambda i, *_: (i, 0, 0)),
                pl.BlockSpec((SORT_TM, SORT_TM), lambda i, *_: (0, 0)),
            ],
            out_specs=[pl.BlockSpec(memory_space=pl.ANY),
                       pl.BlockSpec((SORT_TM, LANES), lambda i, *_: (i, 0))],
            scratch_shapes=[pltpu.VMEM((2, SORT_ROWS, D), BF16), pltpu.VMEM((gap_rows, D), BF16),
                            pltpu.SemaphoreType.DMA((3,))],
        ),
        out_shape=[jax.ShapeDtypeStruct((n_rows, D), BF16), jax.ShapeDtypeStruct((T, LANES), F32)],
        compiler_params=_cparams(("arbitrary",)),
        name="moe_dispatch",
    )(dst_t, gap_t, gapdst_t, x1b, rinfo, lo_vec, tri)


def _expert_kernel(blk_e_ref, used_ref, xs_ref, wg_ref, wu_ref, wd_ref, ys_ref):
    j = pl.program_id(0)

    @pl.when(j < used_ref[0])
    def _():
        xb = xs_ref[...]
        hg = jnp.dot(xb, wg_ref[...].astype(BF16), preferred_element_type=F32)
        hu = jnp.dot(xb, wu_ref[...].astype(BF16), preferred_element_type=F32)
        hid = hg * jax.nn.sigmoid(hg) * hu
        ys_ref[...] = jnp.dot(hid.astype(BF16), wd_ref[...].astype(BF16),
                              preferred_element_type=F32).astype(ys_ref.dtype)

    @pl.when(j >= used_ref[0])
    def _():
        ys_ref[...] = jnp.zeros_like(ys_ref)


def _experts(xs, blk_e, used, wg, wu, wd):
    n_rows = xs.shape[0]
    D = wg.shape[1]
    return pl.pallas_call(
        _expert_kernel,
        grid_spec=pltpu.PrefetchScalarGridSpec(
            num_scalar_prefetch=2,
            grid=(n_rows // EXPERT_TM,),
            in_specs=[
                pl.BlockSpec((EXPERT_TM, xs.shape[1]), lambda j, be, u: (j, 0)),
                pl.BlockSpec((None, D, EXPERT_FF), lambda j, be, u: (be[j], 0, 0)),
                pl.BlockSpec((None, D, EXPERT_FF), lambda j, be, u: (be[j], 0, 0)),
                pl.BlockSpec((None, EXPERT_FF, D), lambda j, be, u: (be[j], 0, 0)),
            ],
            out_specs=pl.BlockSpec((EXPERT_TM, D), lambda j, be, u: (j, 0)),
        ),
        out_shape=jax.ShapeDtypeStruct((n_rows, D), BF16),
        compiler_params=_cparams(("arbitrary",)),
        name="moe_experts",
    )(blk_e, used, xs, wg, wu, wd)


def _combine_kernel(dst_ref, ys_ref, rinfo_ref, pos_ref, x1_ref, g2_ref, b2_ref,
                    out_ref, sorted_s, sem, *, alpha):
    i = pl.program_id(0)
    last = pl.num_programs(0) - 1
    slot = i % 2

    def unit_copy(tile, buf):
        def make(u):
            src = pl.multiple_of(dst_ref[tile * SORT_UNITS + u], BF16_ROWS)
            return pltpu.make_async_copy(
                ys_ref.at[pl.ds(src, BF16_ROWS), :],
                sorted_s.at[buf, pl.ds(pl.multiple_of(u * BF16_ROWS, BF16_ROWS), BF16_ROWS), :], sem.at[buf])
        return make

    def fetch(tile, buf):
        _for_each_unit(lambda u: unit_copy(tile, buf)(u).start())

    @pl.when(i == 0)
    def _():
        sorted_s[...] = jnp.zeros_like(sorted_s)
        fetch(0, 0)

    fetch(jnp.minimum(i + 1, last), 1 - slot)

    rinfo = rinfo_ref[...]
    pos = pos_ref[...]
    p_iota = lax.broadcasted_iota(jnp.int32, (SORT_TM, SORT_ROWS), 1)
    gather = (jnp.where(p_iota == pos[:, 0:1].astype(jnp.int32), rinfo[:, 2:3], 0.0)
              + jnp.where(p_iota == pos[:, 1:2].astype(jnp.int32), rinfo[:, 3:4], 0.0)).astype(BF16)
    _for_each_unit(lambda u: unit_copy(i, slot)(u).wait())
    y = jnp.dot(gather, sorted_s[slot], preferred_element_type=F32)
    out_ref[...] = _layer_norm(alpha * x1_ref[...] + y, g2_ref[...], b2_ref[...])

    @pl.when(i == last)
    def _():
        _for_each_unit(lambda u: unit_copy(last, 1 - slot)(u).wait())


def _combine(ys, rinfo, pos, x1, src_t, g2, b2, alpha):
    T, D = x1.shape
    kern = functools.partial(_combine_kernel, alpha=alpha)
    return pl.pallas_call(
        kern,
        grid_spec=pltpu.PrefetchScalarGridSpec(
            num_scalar_prefetch=1,
            grid=(T // SORT_TM,),
            in_specs=[
                pl.BlockSpec(memory_space=pl.ANY),
                pl.BlockSpec((SORT_TM, LANES), lambda i, *_: (i, 0)),
                pl.BlockSpec((SORT_TM, LANES), lambda i, *_: (i, 0)),
                pl.BlockSpec((SORT_TM, D), lambda i, *_: (i, 0)),
                pl.BlockSpec((1, D), lambda i, *_: (0, 0)),
                pl.BlockSpec((1, D), lambda i, *_: (0, 0)),
            ],
            out_specs=pl.BlockSpec((SORT_TM, D), lambda i, *_: (i, 0)),
            scratch_shapes=[pltpu.VMEM((2, SORT_ROWS, D), BF16), pltpu.SemaphoreType.DMA((2,))],
        ),
        out_shape=jax.ShapeDtypeStruct((T, D), F32),
        compiler_params=_cparams(("arbitrary",)),
        name="moe_combine_ln2",
    )(src_t, ys, rinfo, pos, x1, g2, b2)


def _ceil_to(a, m):
    return (a + m - 1) // m * m


def _moe(x1b, x1, rinfo, cnt, wg, wu, wd, g2, b2, alpha):
    T, D = x1.shape
    n_tiles = T // SORT_TM
    i32 = jnp.int32
    n = cnt[:, 0, :N_EXPERTS].astype(i32)
    npad = _ceil_to(n, BF16_ROWS)
    lo = jnp.cumsum(npad, axis=1) - npad
    tot = npad.sum(axis=0)
    reg = _ceil_to(tot, EXPERT_TM)
    off = jnp.cumsum(reg) - reg
    g = off[None, :] + jnp.cumsum(npad, axis=0) - npad
    n_blocks = T * 2 // EXPERT_TM + n_tiles * N_EXPERTS * (BF16_ROWS - 1) // EXPERT_TM + N_EXPERTS
    spare = n_blocks * EXPERT_TM
    n_rows = spare + 2 * SORT_ROWS
    blk_end = jnp.cumsum(reg) // EXPERT_TM
    j = jnp.arange(n_rows // EXPERT_TM, dtype=i32)
    used = blk_end[-1]
    jj = jnp.minimum(j, used - 1)
    blk_e = jnp.sum(jj[:, None] >= blk_end[None, :], axis=1).astype(i32)
    lo_vec = jnp.pad(lo.astype(F32), ((0, 0), (0, LANES - N_EXPERTS))).reshape(n_tiles, 1, LANES)
    flat = lambda a: a.astype(i32).reshape(-1)
    gap_t = jnp.concatenate([flat((reg - tot) // BF16_ROWS), flat(used)])
    u_row = jnp.arange(SORT_UNITS, dtype=i32) * BF16_ROWS
    u_e = jnp.sum(u_row[None, :, None] >= (lo + npad)[:, None, :], axis=-1)
    live = u_e < N_EXPERTS
    own = u_e[:, :, None] == jnp.arange(N_EXPERTS, dtype=i32)[None, None, :]
    u_g = jnp.sum(jnp.where(own, (g - lo)[:, None, :], 0), axis=-1) + u_row[None, :]
    slot = (jnp.arange(n_tiles, dtype=i32) % 2)[:, None]
    u_dst = jnp.where(live, u_g, spare + slot * SORT_ROWS + u_row[None, :])
    u_src = jnp.where(live, u_g, 0)

    xs, pos = _dispatch(x1b, rinfo, flat(u_dst), gap_t, flat(off + tot), lo_vec, n_rows)
    ys = _experts(xs, blk_e, flat(used), wg, wu, wd)
    return _combine(ys, rinfo, pos, x1, flat(u_src), g2, b2, alpha)


def _layer(x, mem, cos2, sin2, l, depth, w_in, b_in, conv_w, conv_b, conv_ln_g, conv_ln_b,
           w_mem_kv, w_attn_o, w_conv_o, w_mem_o, w_out, ln1_g, ln1_b,
           w_group_router, b_group_router, w_expert_router, b_expert_router,
           w_exp_gate, w_exp_up, w_exp_down, ln2_g, ln2_b):
    B, S, D = x.shape
    alpha = (2.0 * depth) ** 0.25
    w_in_b = w_in[l].astype(BF16)
    b_in_l = b_in[l].reshape(1, -1)

    cols = [slice(p * ATTN_QK + g * ATTN_OUT, p * ATTN_QK + (g + 1) * ATTN_OUT)
            for g in ATTN_GROUP_IDX for p in range(3)]
    o_attn = _attention(x, cos2, sin2, jnp.concatenate([w_in_b[:, c] for c in cols], axis=1),
                        jnp.concatenate([b_in_l[:, c] for c in cols], axis=1))

    c0 = 3 * ATTN_QK
    c1 = c0 + 2 * CONV_CH
    c2 = c1 + MEM_W
    row = lambda a: a[l].reshape(1, -1)
    c, om = _conv_mem_branches(x, mem, w_in_b[:, c0:c2], b_in_l[:, c0:c2], conv_w[l], row(conv_b),
                               row(conv_ln_g), row(conv_ln_b), w_mem_kv[l].astype(BF16))

    wr = jnp.concatenate([w_group_router[l],
                          w_expert_router[l].transpose(1, 0, 2).reshape(D, N_EXPERTS)], axis=1)
    wr = jnp.pad(wr, ((0, 0), (0, LANES - wr.shape[1])))
    br = jnp.concatenate([b_group_router[l], b_expert_router[l].reshape(-1)])
    br = jnp.pad(br, (0, LANES - br.shape[0])).reshape(1, LANES)

    x1, x1b, rinfo, cnt = _merge(
        x.reshape(B * S, D), o_attn, c, om,
        w_in_b[:, c2:], b_in_l[:, c2:], w_attn_o[l].astype(BF16), w_conv_o[l].astype(BF16),
        w_mem_o[l].astype(BF16), w_out[l].astype(BF16), row(ln1_g), row(ln1_b), wr, br, alpha)

    out = _moe(x1b, x1, rinfo, cnt, w_exp_gate[l], w_exp_up[l], w_exp_down[l],
               row(ln2_g), row(ln2_b), alpha)
    return out.reshape(B, S, D)


def kernel(x, mem, positions, w_in, b_in, conv_w, conv_b, conv_ln_g, conv_ln_b, w_mem_kv, w_attn_o,
           w_conv_o, w_mem_o, w_out, ln1_g, ln1_b, w_group_router, b_group_router, w_expert_router,
           b_expert_router, w_exp_gate, w_exp_up, w_exp_down, ln2_g, ln2_b):
    depth = w_in.shape[0]
    cos2, sin2 = _rope_tables(positions)
    for l in range(depth):
        x = _layer(x, mem, cos2, sin2, l, depth, w_in, b_in, conv_w, conv_b, conv_ln_g, conv_ln_b,
                   w_mem_kv, w_attn_o, w_conv_o, w_mem_o, w_out, ln1_g, ln1_b,
                   w_group_router, b_group_router, w_expert_router, b_expert_router,
                   w_exp_gate, w_exp_up, w_exp_down, ln2_g, ln2_b)
    return x
```
